```python
import math
import jax
import jax.numpy as jnp
from jax import lax
import numpy as np

D_MODEL = 1024
BATCH = 16
SEQ = 2048
DEPTH = 1
DEC_BATCH = 32
DEC_SEQ = 1
PAST_LEN = 16384
PAGE_SIZE = 128

D_MIX = D_MODEL
D_A = D_MIX // 2
A_HEADS = 8
A_HEAD_DIM = D_A // A_HEADS
CHUNK = 128
D_B = D_MIX - D_A
B_HEADS = 4
B_HEAD_DIM = D_B // (2 * B_HEADS)
B_V_DIM = 2 * B_HEAD_DIM
D_IN = 2 * D_A + 3 * D_B
Q_BLOCK = 128
N_EXPERTS = 256
N_GROUPS = 8
TOPK_GROUPS = 4
TOP_K = 8
D_EXPERT = 256
D_SHARED = 256
ROUTED_SCALE = 2.5
MOE_BLOCK = 128
D_PLE = 256
LN_EPS = 1e-5
DEEPNORM_ALPHA = (2 * DEPTH) ** 0.25
DEEPNORM_BETA = (8 * DEPTH) ** -0.25
F32 = jnp.float32

kernel_name = 'hymba_gmlp_diffattn_moe_step'


def _layer_norm(x, g, b):
    xf = x.astype(F32)
    mu = jnp.mean(xf, axis=-1, keepdims=True)
    var = jnp.mean(jnp.square(xf - mu), axis=-1, keepdims=True)
    return ((xf - mu) * lax.rsqrt(var + LN_EPS) * g.astype(F32) + b.astype(F32)).astype(x.dtype)


def _swiglu(x, wg, wu, wd):
    return (jax.nn.silu(x @ wg) * (x @ wu)) @ wd


def _mixer_inputs(h, lp):
    z = h @ lp['w_in']
    u, va, q, k, vb = jnp.split(z, [D_A, 2 * D_A, 2 * D_A + D_B, 2 * D_A + 2 * D_B], axis=-1)
    u = jax.nn.gelu(u, approximate=False)
    va = _layer_norm(jax.nn.gelu(va, approximate=False), lp['ln_v_g'], lp['ln_v_b'])
    lead = h.shape[:-1]
    q = q.reshape(lead + (2 * B_HEADS, B_HEAD_DIM))
    k = k.reshape(lead + (2 * B_HEADS, B_HEAD_DIM))
    vb = vb.reshape(lead + (B_HEADS, B_V_DIM))
    return u, va, q, k, vb


def _causal_spatial(w_s):
    return jnp.where(jnp.tril(jnp.ones((CHUNK, CHUNK), dtype=bool)), w_s, jnp.zeros_like(w_s))


def _spatial_gate_prompt(u, va, w_s, b_s):
    b, s, _ = va.shape
    vh = va.reshape(b, s // CHUNK, CHUNK, A_HEADS, A_HEAD_DIM)
    sg = jnp.einsum('hts,bcshe->bcthe', _causal_spatial(w_s), vh) + b_s.T[:, :, None]
    return u * sg.reshape(b, s, D_A)


def _spatial_gate_sample(u, va, w_s, b_s):
    nb, t, _ = va.shape
    vh = va.reshape(nb, t, A_HEADS, A_HEAD_DIM)
    wm = _causal_spatial(w_s)[:, :t, :t]
    sg = jnp.einsum('hts,bshe->bthe', wm, vh) + b_s[:, :t].T[:, :, None]
    return u * sg.reshape(nb, t, D_A)


def _diff_weights(sc, lam):
    w = jax.nn.softmax(sc, axis=-1)
    w = w.reshape(w.shape[:-3] + (B_HEADS, 2) + w.shape[-2:])
    return w[..., 0, :, :] - lam * w[..., 1, :, :]


def _diff_attn_prompt(q, k, v, lam):
    b, s = q.shape[:2]
    nblk = s // Q_BLOCK
    qb = q.reshape(b, nblk, Q_BLOCK, 2 * B_HEADS, B_HEAD_DIM).transpose(1, 0, 2, 3, 4)
    kpos = jnp.arange(s)
    scale = B_HEAD_DIM ** -0.5

    def one_block(args):
        i, qi = args
        sc = jnp.einsum('bqhd,bkhd->bhqk', qi, k).astype(F32) * scale
        qpos = i * Q_BLOCK + jnp.arange(Q_BLOCK)
        sc = jnp.where(kpos[None, :] <= qpos[:, None], sc, -jnp.inf)
        w = _diff_weights(sc, lam).astype(v.dtype)
        return jnp.einsum('bhqk,bkhe->bqhe', w, v)

    out = lax.map(one_block, (jnp.arange(nblk), qb))
    return out.transpose(1, 0, 2, 3, 4).reshape(b, s, B_HEADS, B_V_DIM)


def _diff_attn_sample(q, k_new, v_new, cache_k, cache_v, page_table, lam):
    n_pages = page_table.shape[1]
    past = n_pages * cache_k.shape[1]
    t = q.shape[1]
    kpos = jnp.arange(past + t)
    qpos = past + jnp.arange(t)
    mask = kpos[None, :] <= qpos[:, None]
    scale = B_HEAD_DIM ** -0.5

    def one_seq(args):
        pt, qi, ki, vi = args
        kk = jnp.concatenate([cache_k[pt].reshape(past, 2 * B_HEADS, B_HEAD_DIM), ki], axis=0)
        vv = jnp.concatenate([cache_v[pt].reshape(past, B_HEADS, B_V_DIM), vi], axis=0)
        sc = jnp.einsum('qhd,khd->hqk', qi, kk).astype(F32) * scale
        sc = jnp.where(mask, sc, -jnp.inf)
        w = _diff_weights(sc, lam).astype(vv.dtype)
        return jnp.einsum('hqk,khe->qhe', w, vv)

    return lax.map(one_seq, (page_table, q, k_new, v_new))


def _diff_head_out(o, g, lam_init):
    of = o.astype(F32)
    of = of * lax.rsqrt(jnp.mean(of * of, axis=-1, keepdims=True) + LN_EPS) * g.astype(F32) * (1.0 - lam_init)
    return of.astype(o.dtype).reshape(o.shape[:-2] + (D_B,))


def _routed_experts(xf, eidx, gw, w_eg, w_eu, w_ed):
    t = xf.shape[0]
    n = t * TOP_K
    blk = max(8, min(MOE_BLOCK, n // N_EXPERTS))
    n_blocks = -(-(n + N_EXPERTS * (blk - 1)) // blk)
    m = n_blocks * blk
    e_flat = eidx.reshape(n).astype(jnp.int32)
    tok_flat = jnp.arange(n, dtype=jnp.int32) // TOP_K
    g_flat = gw.reshape(n)
    order = jnp.argsort(e_flat)
    e_sorted = e_flat[order]
    counts = jnp.bincount(e_flat, length=N_EXPERTS)
    starts = jnp.cumsum(counts) - counts
    padded = (counts + blk - 1) // blk * blk
    pends = jnp.cumsum(padded)
    dest = (pends - padded)[e_sorted] + jnp.arange(n) - starts[e_sorted]
    tok_buf = jnp.zeros((m,), jnp.int32).at[dest].set(tok_flat[order])
    gate_buf = jnp.zeros((m,), gw.dtype).at[dest].set(g_flat[order])
    blk_expert = jnp.minimum(jnp.searchsorted(pends, jnp.arange(n_blocks) * blk, side='right'), N_EXPERTS - 1)

    def one_block(args):
        e, tok, g = args
        return _swiglu(xf[tok], w_eg[e], w_eu[e], w_ed[e]) * g[:, None]

    out = lax.map(one_block, (blk_expert, tok_buf.reshape(n_blocks, blk), gate_buf.reshape(n_blocks, blk)))
    return jax.ops.segment_sum(out.reshape(m, D_MODEL), tok_buf, num_segments=t)


def _moe(h, lp):
    lead = h.shape[:-1]
    xf = h.reshape(-1, D_MODEL)
    t = xf.shape[0]
    scores = jax.nn.sigmoid((xf @ lp['w_router']).astype(F32))
    biased = scores + lp['router_bias'].astype(F32)
    grp = biased.reshape(t, N_GROUPS, N_EXPERTS // N_GROUPS)
    grp_score = jnp.sum(lax.top_k(grp, 2)[0], axis=-1)
    _, gidx = lax.top_k(grp_score, TOPK_GROUPS)
    gmask = jnp.any(gidx[:, :, None] == jnp.arange(N_GROUPS)[None, None, :], axis=1)
    emask = jnp.repeat(gmask, N_EXPERTS // N_GROUPS, axis=-1)
    _, eidx = lax.top_k(jnp.where(emask, biased, -jnp.inf), TOP_K)
    gw = jnp.take_along_axis(scores, eidx, axis=-1)
    gw = gw / jnp.sum(gw, axis=-1, keepdims=True) * ROUTED_SCALE
    routed = _routed_experts(xf, eidx, gw.astype(xf.dtype), lp['w_exp_gate'], lp['w_exp_up'], lp['w_exp_down'])
    shared = _swiglu(xf, lp['w_sh_gate'], lp['w_sh_up'], lp['w_sh_down'])
    return (routed + shared).reshape(lead + (D_MODEL,))


def _finish_layer(h, a_out, b_out, p, lp):
    mix = jnp.concatenate([a_out, b_out], axis=-1) @ lp['w_out']
    h1 = _layer_norm(DEEPNORM_ALPHA * h + mix, lp['ln1_g'], lp['ln1_b'])
    h2 = _layer_norm(DEEPNORM_ALPHA * h1 + _moe(h1, lp), lp['ln2_g'], lp['ln2_b'])
    gate = jax.nn.sigmoid((h2 @ lp['w_ple_gate'] + lp['b_ple_gate']).astype(F32)).astype(h2.dtype)
    return h2 + gate * (p @ lp['w_ple'])


def setup_inputs(seed: int = 0) -> dict:
    key = jax.random.key(seed)
    ks = jax.random.split(key, 40)
    n_pages = PAST_LEN // PAGE_SIZE
    n_used = DEC_BATCH * n_pages
    n_phys = n_used + n_used // 4
    d = D_MODEL

    def nrm(k, shape, scale=1.0):
        return jax.random.normal(k, shape, F32) * scale

    page_table = jax.random.permutation(ks[0], n_phys)[:n_used].reshape(DEC_BATCH, n_pages).astype(jnp.int32)
    return {
        'x_prompt': nrm(ks[1], (BATCH, SEQ, d)),
        'x_sample': nrm(ks[2], (DEC_BATCH, DEC_SEQ, d)),
        'cache_k': nrm(ks[3], (DEPTH, n_phys, PAGE_SIZE, 2 * B_HEADS, B_HEAD_DIM)),
        'cache_v': nrm(ks[4], (DEPTH, n_phys, PAGE_SIZE, B_HEADS, B_V_DIM)),
        'page_table': page_table,
        'p_prompt': nrm(ks[5], (DEPTH, BATCH, SEQ, D_PLE)),
        'p_sample': nrm(ks[6], (DEPTH, DEC_BATCH, DEC_SEQ, D_PLE)),
        'w_in': nrm(ks[7], (DEPTH, d, D_IN), d ** -0.5),
        'ln_v_g': 1.0 + nrm(ks[8], (DEPTH, D_A), 0.02),
        'ln_v_b': nrm(ks[9], (DEPTH, D_A), 0.02),
        'w_spatial': nrm(ks[10], (DEPTH, A_HEADS, CHUNK, CHUNK), CHUNK ** -0.5),
        'b_spatial': 1.0 + nrm(ks[11], (DEPTH, A_HEADS, CHUNK), 0.1),
        'lambda_q1': nrm(ks[12], (DEPTH, B_HEAD_DIM), 0.1),
        'lambda_k1': nrm(ks[13], (DEPTH, B_HEAD_DIM), 0.1),
        'lambda_q2': nrm(ks[14], (DEPTH, B_HEAD_DIM), 0.1),
        'lambda_k2': nrm(ks[15], (DEPTH, B_HEAD_DIM), 0.1),
        'subln_g': 1.0 + nrm(ks[16], (DEPTH, B_V_DIM), 0.02),
        'w_out': nrm(ks[17], (DEPTH, D_MIX, d), D_MIX ** -0.5 * DEEPNORM_BETA),
        'ln1_g': 1.0 + nrm(ks[18], (DEPTH, d), 0.02),
        'ln1_b': nrm(ks[19], (DEPTH, d), 0.02),
        'w_router': nrm(ks[20], (DEPTH, d, N_EXPERTS), d ** -0.5),
        'router_bias': nrm(ks[21], (DEPTH, N_EXPERTS), 0.01),
        'w_exp_gate': nrm(ks[22], (DEPTH, N_EXPERTS, d, D_EXPERT), d ** -0.5),
        'w_exp_up': nrm(ks[23], (DEPTH, N_EXPERTS, d, D_EXPERT), d ** -0.5),
        'w_exp_down': nrm(ks[24], (DEPTH, N_EXPERTS, D_EXPERT, d), D_EXPERT ** -0.5 * DEEPNORM_BETA),
        'w_sh_gate': nrm(ks[25], (DEPTH, d, D_SHARED), d ** -0.5),
        'w_sh_up': nrm(ks[26], (DEPTH, d, D_SHARED), d ** -0.5),
        'w_sh_down': nrm(ks[27], (DEPTH, D_SHARED, d), D_SHARED ** -0.5 * DEEPNORM_BETA),
        'ln2_g': 1.0 + nrm(ks[28], (DEPTH, d), 0.02),
        'ln2_b': nrm(ks[29], (DEPTH, d), 0.02),
        'w_ple': nrm(ks[30], (DEPTH, D_PLE, d), D_PLE ** -0.5),
        'w_ple_gate': nrm(ks[31], (DEPTH, d, d), d ** -0.5),
        'b_ple_gate': nrm(ks[32], (DEPTH, d), 0.02),
    }


def reference(x_prompt, x_sample, cache_k, cache_v, page_table, p_prompt, p_sample,
              w_in, ln_v_g, ln_v_b, w_spatial, b_spatial,
              lambda_q1, lambda_k1, lambda_q2, lambda_k2, subln_g, w_out, ln1_g, ln1_b,
              w_router, router_bias, w_exp_gate, w_exp_up, w_exp_down,
              w_sh_gate, w_sh_up, w_sh_down, ln2_g, ln2_b, w_ple, w_ple_gate, b_ple_gate):
    hp, hs = x_prompt, x_sample
    kp_rows, vp_rows, gp_rows, ks_rows, vs_rows, gs_rows = [], [], [], [], [], []
    for i in range(DEPTH):
        lp = {
            'w_in': w_in[i], 'ln_v_g': ln_v_g[i], 'ln_v_b': ln_v_b[i],
            'w_out': w_out[i], 'ln1_g': ln1_g[i], 'ln1_b': ln1_b[i],
            'w_router': w_router[i], 'router_bias': router_bias[i],
            'w_exp_gate': w_exp_gate[i], 'w_exp_up': w_exp_up[i], 'w_exp_down': w_exp_down[i],
            'w_sh_gate': w_sh_gate[i], 'w_sh_up': w_sh_up[i], 'w_sh_down': w_sh_down[i],
            'ln2_g': ln2_g[i], 'ln2_b': ln2_b[i],
            'w_ple': w_ple[i], 'w_ple_gate': w_ple_gate[i], 'b_ple_gate': b_ple_gate[i],
        }
        lam_init = 0.8 - 0.6 * math.exp(-0.3 * i)
        lam = (jnp.exp(jnp.sum(lambda_q1[i].astype(F32) * lambda_k1[i].astype(F32)))
               - jnp.exp(jnp.sum(lambda_q2[i].astype(F32) * lambda_k2[i].astype(F32))) + lam_init)
        u, va, q, k, vb = _mixer_inputs(hp, lp)
        a_out = _spatial_gate_prompt(u, va, w_spatial[i], b_spatial[i])
        b_out = _diff_head_out(_diff_attn_prompt(q, k, vb, lam), subln_g[i], lam_init)
        kp_rows.append(k)
        vp_rows.append(vb)
        gp_rows.append(va[:, -CHUNK:])
        hp = _finish_layer(hp, a_out, b_out, p_prompt[i], lp)
        u, va, q, k, vb = _mixer_inputs(hs, lp)
        a_out = _spatial_gate_sample(u, va, w_spatial[i], b_spatial[i])
        o = _diff_attn_sample(q, k, vb, cache_k[i], cache_v[i], page_table, lam)
        b_out = _diff_head_out(o, subln_g[i], lam_init)
        ks_rows.append(k)
        vs_rows.append(vb)
        gs_rows.append(va)
        hs = _finish_layer(hs, a_out, b_out, p_sample[i], lp)
    k_prompt = jnp.stack(kp_rows)
    v_prompt = jnp.stack(vp_rows)
    gv_prompt = jnp.stack(gp_rows)
    k_sample = jnp.stack(ks_rows)
    v_sample = jnp.stack(vs_rows)
    gv_sample = jnp.stack(gs_rows)
    return (hp, hs, k_prompt, v_prompt, gv_prompt, k_sample, v_sample, gv_sample)
```

```python
import functools
import math

import jax
import jax.numpy as jnp
from jax import lax
from jax.experimental import pallas as pl
from jax.experimental.pallas import tpu as pltpu

F32 = jnp.float32
BF16 = jnp.bfloat16

A_HEADS = 8
CHUNK = 128
B_HEADS = 4
B_HEAD_DIM = 64
B_V_DIM = 128
N_EXPERTS = 256
N_GROUPS = 8
GROUP_SIZE = N_EXPERTS // N_GROUPS
TOPK_GROUPS = 4
TOP_K = 8
ROUTED_SCALE = 2.5
LN_EPS = 1e-5
DEPTH = 1
DEEPNORM_ALPHA = (2 * DEPTH) ** 0.25
LAM_INIT = 0.8 - 0.6 * math.exp(-0.3 * 0)

LANES = 128
ROW_TILE = 256
ATTN_TQ = 256
ATTN_TK = 256
MOE_TILE = 256
VMEM_LIMIT = 56 * 1024 * 1024

_NT = (((1,), (1,)), ((), ()))


def _const_spec(shape):
    nd = len(shape)
    return pl.BlockSpec(shape, lambda *_: (0,) * nd)


def _layer_norm(x, g, b):
    mu = jnp.mean(x, axis=-1, keepdims=True)
    xc = x - mu
    var = jnp.mean(xc * xc, axis=-1, keepdims=True)
    return xc * lax.rsqrt(var + LN_EPS) * g + b


def _gelu(x):
    return 0.5 * x * (1.0 + lax.erf(x * (2.0 ** -0.5)))


_ERFC_P = (2.326819970068386e-2, -1.387039388740657e-1, 3.687424674597105e-1, -5.824733027278666e-1,
           6.210004621745983e-1, -4.944515323274145e-1, 3.404879937665872e-1, -2.741127028184656e-1,
           5.638259427386472e-1)
_ERFC_R = (-1.047766399936249e+1, 1.297719955372516e+1, -7.495518717768503e+0, 2.921019019210786e+0,
           -1.015265279202700e+0, 4.218463358204948e-1, -2.820767439740514e-1, 5.641895067754075e-1)


def _horner(y, coeffs):
    r = jnp.full(y.shape, coeffs[0], F32)
    for c in coeffs[1:]:
        r = r * y + c
    return r


def _gelu_tail_exact(x):
    t = -x * (2.0 ** -0.5)
    y = jnp.abs(t)
    q = 1.0 / jnp.maximum(y, 1.0)
    q2 = q * q
    tail = jnp.exp(-t * t) * q * jnp.where(y < 2.0, _horner(q2, _ERFC_P), _horner(q2, _ERFC_R))
    tail = jnp.where(t < 0.0, 2.0 - tail, tail)
    return 0.5 * x * jnp.where(y < 1.0, 1.0 - lax.erf(t), tail)


def _tree_sum0(x):
    n = x.shape[0]
    while n > 1 and n % 2 == 0:
        n //= 2
        x = x[:n] + x[n:]
    return jnp.sum(x, axis=0)


def _lam(lam_ref):
    lv = lam_ref[...]
    s1 = jnp.sum(lv[0:1] * lv[1:2], axis=-1, keepdims=True)
    s2 = jnp.sum(lv[2:3] * lv[3:4], axis=-1, keepdims=True)
    return jnp.exp(s1) - jnp.exp(s2) + LAM_INIT


def _proj_kernel(x_ref, w_in_ref, lng_ref, lnb_ref, wsp_ref, bsp_ref,
                 k_ref, v_ref, qb_ref, kb_ref, vb_ref, a_ref, gv_ref,
                 wbf_ref, wcat_ref, *, tiles_per_seq):
    i = pl.program_id(0)
    tm = x_ref.shape[0]
    d_a = lng_ref.shape[1]

    @pl.when(i == 0)
    def _init():
        rows = w_in_ref.shape[0]
        step = 128

        def body(r, c):
            sl = pl.ds(pl.multiple_of(r * step, step), step)
            wbf_ref[sl, :] = w_in_ref[sl, :].astype(BF16)
            return c

        lax.fori_loop(0, rows // step, body, 0)
        row = lax.broadcasted_iota(jnp.int32, (CHUNK, CHUNK), 0)
        col = lax.broadcasted_iota(jnp.int32, (CHUNK, CHUNK), 1)
        for j in range(A_HEADS // 2):
            w0 = jnp.where(col <= row, wsp_ref[2 * j], 0.0)
            w1 = jnp.where(col <= row, wsp_ref[2 * j + 1], 0.0)
            wcat_ref[j] = jnp.concatenate([w0, w1], axis=1).astype(BF16)

    xb = x_ref[...].astype(BF16)

    def seg(s):
        return jnp.dot(xb, wbf_ref[:, s * d_a:(s + 1) * d_a], preferred_element_type=F32)

    k = seg(3)
    k_ref[...] = k
    kb_ref[...] = k.astype(BF16)
    v = seg(4)
    v_ref[...] = v
    vb_ref[...] = v.astype(BF16)
    qb_ref[...] = (seg(2) * (B_HEAD_DIM ** -0.5)).astype(BF16)

    va = _layer_norm(_gelu(seg(1)), lng_ref[...], lnb_ref[...])

    @pl.when(i % tiles_per_seq == tiles_per_seq - 1)
    def _gv():
        gv_ref[0] = va[tm - CHUNK:, :]

    u = _gelu(seg(0))
    vab = va.astype(BF16)
    lane = lax.broadcasted_iota(jnp.int32, (CHUNK, LANES), 1)
    zero = jnp.zeros((CHUNK, LANES), BF16)
    for c in range(tm // CHUNK):
        r0 = c * CHUNK
        for j in range(A_HEADS // 2):
            vp = vab[r0:r0 + CHUNK, j * LANES:(j + 1) * LANES]
            stacked = jnp.concatenate([jnp.where(lane < 64, vp, zero), jnp.where(lane >= 64, vp, zero)], axis=0)
            sg = jnp.dot(wcat_ref[j], stacked, preferred_element_type=F32) + bsp_ref[:, j * LANES:(j + 1) * LANES]
            a_ref[r0:r0 + CHUNK, j * LANES:(j + 1) * LANES] = (
                u[r0:r0 + CHUNK, j * LANES:(j + 1) * LANES] * sg).astype(BF16)


def _proj_prompt(x, w_in, lng, lnb, w_sp, b_sp_full, seq):
    t, d = x.shape
    d_in = w_in.shape[1]
    d_a = lng.shape[1]
    tm = min(ROW_TILE, seq)
    tiles_per_seq = seq // tm
    n_seq = t // seq
    row = lambda w: pl.BlockSpec((tm, w), lambda i: (i, 0))
    out_shape = (
        jax.ShapeDtypeStruct((t, d_a), F32), jax.ShapeDtypeStruct((t, d_a), F32),
        jax.ShapeDtypeStruct((t, d_a), BF16), jax.ShapeDtypeStruct((t, d_a), BF16),
        jax.ShapeDtypeStruct((t, d_a), BF16), jax.ShapeDtypeStruct((t, d_a), BF16),
        jax.ShapeDtypeStruct((n_seq, CHUNK, d_a), F32),
    )
    return pl.pallas_call(
        functools.partial(_proj_kernel, tiles_per_seq=tiles_per_seq),
        grid=(t // tm,),
        in_specs=[row(d), _const_spec((d, d_in)), _const_spec((1, d_a)), _const_spec((1, d_a)),
                  _const_spec((A_HEADS, CHUNK, CHUNK)), _const_spec((CHUNK, d_a))],
        out_specs=(row(d_a), row(d_a), row(d_a), row(d_a), row(d_a), row(d_a),
                   pl.BlockSpec((1, CHUNK, d_a), lambda i: (i // tiles_per_seq, 0, 0))),
        out_shape=out_shape,
        scratch_shapes=[pltpu.VMEM((d, d_in), BF16), pltpu.VMEM((A_HEADS // 2, CHUNK, 2 * CHUNK), BF16)],
        compiler_params=pltpu.CompilerParams(dimension_semantics=("arbitrary",), vmem_limit_bytes=VMEM_LIMIT),
        name="proj_prompt",
    )(x, w_in, lng, lnb, w_sp, b_sp_full)


def _attn_kernel(lam_ref, g_ref, q_ref, k_ref, v_ref, o_ref, acc1, acc2, m1, l1, m2, l2):
    i = pl.program_id(2)
    tq = q_ref.shape[0]
    tk = ATTN_TK if k_ref.shape[0] >= ATTN_TK else k_ref.shape[0]
    ratio = tq // tk
    qb = q_ref[...]
    lane = lax.broadcasted_iota(jnp.int32, qb.shape, 1)
    zero = jnp.zeros(qb.shape, BF16)
    q_lo = jnp.where(lane < B_HEAD_DIM, qb, zero)
    q_hi = jnp.where(lane >= B_HEAD_DIM, qb, zero)

    acc1[...] = jnp.zeros(acc1.shape, F32)
    acc2[...] = jnp.zeros(acc2.shape, F32)
    m1[...] = jnp.full(m1.shape, -jnp.inf, F32)
    m2[...] = jnp.full(m2.shape, -jnp.inf, F32)
    l1[...] = jnp.zeros(l1.shape, F32)
    l2[...] = jnp.zeros(l2.shape, F32)

    def one_map(qm, kb, vb, mask, acc, m, l):
        s = lax.dot_general(qm, kb, _NT, preferred_element_type=F32)
        if mask is not None:
            s = jnp.where(mask, s, -jnp.inf)
        m_old = m[...]
        m_new = jnp.maximum(m_old, jnp.max(s, axis=-1, keepdims=True))
        alpha = jnp.exp(m_old - m_new)
        p = jnp.exp(s - m_new)
        l[...] = alpha * l[...] + jnp.sum(p, axis=-1, keepdims=True)
        acc[...] = alpha * acc[...] + jnp.dot(p.astype(BF16), vb, preferred_element_type=F32)
        m[...] = m_new

    def block(j, mask):
        sl = pl.ds(pl.multiple_of(j * tk, tk), tk)
        kb = k_ref[sl, :]
        vb = v_ref[sl, :]
        one_map(q_lo, kb, vb, mask, acc1, m1, l1)
        one_map(q_hi, kb, vb, mask, acc2, m2, l2)

    def full_body(j, c):
        block(j, None)
        return c

    lax.fori_loop(0, i * ratio, full_body, 0)
    row = lax.broadcasted_iota(jnp.int32, (tq, tk), 0)
    col = lax.broadcasted_iota(jnp.int32, (tq, tk), 1)
    for jj in range(ratio):
        block(i * ratio + jj, col + jj * tk <= row)

    lam = _lam(lam_ref)
    o = acc1[...] / l1[...] - lam * (acc2[...] / l2[...])
    ms = jnp.mean(o * o, axis=-1, keepdims=True)
    o_ref[...] = (o * lax.rsqrt(ms + LN_EPS) * g_ref[...] * (1.0 - LAM_INIT)).astype(o_ref.dtype)


def _attn_prompt(lam_rows, subln_g, qb, kb, vb, n_seq, seq):
    t, d_b = qb.shape
    tq = min(ATTN_TQ, seq)
    nq = seq // tq
    return pl.pallas_call(
        _attn_kernel,
        grid=(n_seq, B_HEADS, nq),
        in_specs=[_const_spec((4, B_HEAD_DIM)), _const_spec((1, B_V_DIM)),
                  pl.BlockSpec((tq, LANES), lambda b, h, i: (b * nq + i, h)),
                  pl.BlockSpec((seq, LANES), lambda b, h, i: (b, h)),
                  pl.BlockSpec((seq, LANES), lambda b, h, i: (b, h))],
        out_specs=pl.BlockSpec((tq, LANES), lambda b, h, i: (b * nq + i, h)),
        out_shape=jax.ShapeDtypeStruct((t, d_b), BF16),
        scratch_shapes=[pltpu.VMEM((tq, B_V_DIM), F32), pltpu.VMEM((tq, B_V_DIM), F32),
                        pltpu.VMEM((tq, 1), F32), pltpu.VMEM((tq, 1), F32),
                        pltpu.VMEM((tq, 1), F32), pltpu.VMEM((tq, 1), F32)],
        compiler_params=pltpu.CompilerParams(dimension_semantics=("arbitrary", "arbitrary", "arbitrary"),
                                             vmem_limit_bytes=VMEM_LIMIT),
        name="attn_prompt",
    )(lam_rows, subln_g, qb, kb, vb)


def _route(logits_t, bias_col):
    e, n = logits_t.shape
    scores = jax.nn.sigmoid(logits_t)
    biased = scores + bias_col
    ninf = jnp.float32(-jnp.inf)
    gs = []
    sub = lax.broadcasted_iota(jnp.int32, (GROUP_SIZE, n), 0)
    for g in range(N_GROUPS):
        blk = biased[g * GROUP_SIZE:(g + 1) * GROUP_SIZE]
        t1 = jnp.max(blk, axis=0, keepdims=True)
        i1 = jnp.min(jnp.where(blk == t1, sub, GROUP_SIZE), axis=0, keepdims=True)
        t2 = jnp.max(jnp.where(sub == i1, ninf, blk), axis=0, keepdims=True)
        gs.append(t1 + t2)
    keep = []
    for g in range(N_GROUPS):
        cnt = jnp.zeros((1, n), jnp.int32)
        for o in range(N_GROUPS):
            if o == g:
                continue
            beats = (gs[o] >= gs[g]) if o < g else (gs[o] > gs[g])
            cnt = cnt + beats.astype(jnp.int32)
        keep.append(cnt < TOPK_GROUPS)
    masked = jnp.concatenate(
        [jnp.where(keep[g], biased[g * GROUP_SIZE:(g + 1) * GROUP_SIZE], ninf) for g in range(N_GROUPS)], axis=0)
    rid = lax.broadcasted_iota(jnp.int32, (e, n), 0)
    idxs, vals = [], []
    for _ in range(TOP_K):
        top = jnp.max(masked, axis=0, keepdims=True)
        idx = jnp.min(jnp.where(masked == top, rid, e), axis=0, keepdims=True)
        hit = rid == idx
        vals.append(jnp.sum(jnp.where(hit, scores, 0.0), axis=0, keepdims=True))
        idxs.append(idx)
        masked = jnp.where(hit, ninf, masked)
    eidx = jnp.concatenate(idxs, axis=0)
    gw = jnp.concatenate(vals, axis=0)
    gw = gw / jnp.sum(gw, axis=0, keepdims=True) * ROUTED_SCALE
    return eidx, gw


def _mix_kernel(a_ref, b_ref, x_ref, wout_ref, g_ref, bb_ref, wrt_ref, rb_ref,
                h1_ref, h1b_ref, eidx_ref, gw_ref, wo_s, wr_s):
    i = pl.program_id(0)
    d_a = a_ref.shape[1]

    @pl.when(i == 0)
    def _init():
        wo_s[...] = wout_ref[...].astype(BF16)
        wr_s[...] = wrt_ref[...].astype(BF16)

    dot = functools.partial(jnp.dot, preferred_element_type=F32)
    mix = dot(a_ref[...], wo_s[:d_a, :]) + dot(b_ref[...], wo_s[d_a:, :])
    h1 = _layer_norm(DEEPNORM_ALPHA * x_ref[...] + mix, g_ref[...], bb_ref[...])
    h1_ref[...] = h1
    h1b = h1.astype(BF16)
    h1b_ref[...] = h1b
    logits_t = lax.dot_general(wr_s[...], h1b, _NT, preferred_element_type=F32)
    eidx, gw = _route(logits_t, rb_ref[...])
    eidx_ref[...] = eidx
    gw_ref[...] = gw


def _mix_route(a, b, x, w_out, g, bb, w_router_t, rbias_col):
    t, d = x.shape
    d_a = a.shape[1]
    tm = min(ROW_TILE, t)
    row = lambda w: pl.BlockSpec((tm, w), lambda i: (i, 0))
    colb = pl.BlockSpec((TOP_K, tm), lambda i: (0, i))
    return pl.pallas_call(
        _mix_kernel,
        grid=(t // tm,),
        in_specs=[row(d_a), row(d_a), row(d), _const_spec((d, d)), _const_spec((1, d)), _const_spec((1, d)),
                  _const_spec((N_EXPERTS, d)), _const_spec((N_EXPERTS, 1))],
        out_specs=(row(d), row(d), colb, colb),
        out_shape=(jax.ShapeDtypeStruct((t, d), F32), jax.ShapeDtypeStruct((t, d), BF16),
                   jax.ShapeDtypeStruct((TOP_K, t), jnp.int32), jax.ShapeDtypeStruct((TOP_K, t), F32)),
        scratch_shapes=[pltpu.VMEM((d, d), BF16), pltpu.VMEM((N_EXPERTS, d), BF16)],
        compiler_params=pltpu.CompilerParams(dimension_semantics=("arbitrary",), vmem_limit_bytes=VMEM_LIMIT),
        name="mix_route",
    )(a, b, x, w_out, g, bb, w_router_t, rbias_col)


def _experts_kernel(te_ref, nu_ref, x_ref, wg_ref, wu_ref, wd_ref, y_ref, wgu_s, wd_s):
    t = pl.program_id(0)
    prev = te_ref[jnp.maximum(t - 1, 0)]
    d_e = wg_ref.shape[2]

    @pl.when((t < nu_ref[0]) & ((t == 0) | (te_ref[t] != prev)))
    def _cast():
        wgu_s[:, :d_e] = wg_ref[0].astype(BF16)
        wgu_s[:, d_e:] = wu_ref[0].astype(BF16)
        wd_s[...] = wd_ref[0].astype(BF16)

    @pl.when(t < nu_ref[0])
    def _compute():
        gu = jnp.dot(x_ref[...], wgu_s[...], preferred_element_type=F32)
        hid = jax.nn.silu(gu[:, :d_e]) * gu[:, d_e:]
        y_ref[...] = jnp.dot(hid.astype(BF16), wd_s[...], preferred_element_type=F32).astype(y_ref.dtype)


def _experts(tile_expert, n_used, xs, w_g, w_u, w_d):
    m, d = xs.shape
    d_e = w_g.shape[2]
    n_tiles = m // MOE_TILE

    def rows(t, te, nu):
        return (jnp.minimum(t, nu[0] - 1), 0)

    def wmap(t, te, nu):
        return (te[jnp.minimum(t, nu[0] - 1)], 0, 0)

    return pl.pallas_call(
        _experts_kernel,
        grid_spec=pltpu.PrefetchScalarGridSpec(
            num_scalar_prefetch=2,
            grid=(n_tiles,),
            in_specs=[pl.BlockSpec((MOE_TILE, d), rows),
                      pl.BlockSpec((1, d, d_e), wmap), pl.BlockSpec((1, d, d_e), wmap),
                      pl.BlockSpec((1, d_e, d), wmap)],
            out_specs=pl.BlockSpec((MOE_TILE, d), rows),
            scratch_shapes=[pltpu.VMEM((d, 2 * d_e), BF16), pltpu.VMEM((d_e, d), BF16)]),
        out_shape=jax.ShapeDtypeStruct((m, d), BF16),
        compiler_params=pltpu.CompilerParams(dimension_semantics=("arbitrary",), vmem_limit_bytes=VMEM_LIMIT),
        name="experts",
    )(tile_expert, n_used, xs, w_g, w_u, w_d)


def _final_kernel(h1_ref, r_ref, p_ref, wsg_ref, wsu_ref, wsd_ref, g_ref, bb_ref, wpg_ref, bpg_ref, wp_ref,
                  y_ref, wgu_s, wd_s, wpg_s, wp_s):
    i = pl.program_id(0)
    d_s = wsg_ref.shape[1]

    @pl.when(i == 0)
    def _init():
        wgu_s[:, :d_s] = wsg_ref[...].astype(BF16)
        wgu_s[:, d_s:] = wsu_ref[...].astype(BF16)
        wd_s[...] = wsd_ref[...].astype(BF16)
        wpg_s[...] = wpg_ref[...].astype(BF16)
        wp_s[...] = wp_ref[...].astype(BF16)

    dot = functools.partial(jnp.dot, preferred_element_type=F32)
    h1 = h1_ref[...]
    gu = dot(h1.astype(BF16), wgu_s[...])
    hid = jax.nn.silu(gu[:, :d_s]) * gu[:, d_s:]
    shared = dot(hid.astype(BF16), wd_s[...])
    h2 = _layer_norm(DEEPNORM_ALPHA * h1 + (r_ref[...].astype(F32) + shared), g_ref[...], bb_ref[...])
    gate = jax.nn.sigmoid(dot(h2.astype(BF16), wpg_s[...]) + bpg_ref[...])
    y_ref[...] = h2 + gate * dot(p_ref[...].astype(BF16), wp_s[...])


def _final(h1, routed, p, w_sg, w_su, w_sd, g, bb, w_pg, b_pg, w_p):
    t, d = h1.shape
    d_s = w_sg.shape[1]
    d_p = p.shape[1]
    tm = min(ROW_TILE, t)
    row = lambda w: pl.BlockSpec((tm, w), lambda i: (i, 0))
    return pl.pallas_call(
        _final_kernel,
        grid=(t // tm,),
        in_specs=[row(d), row(d), row(d_p), _const_spec((d, d_s)), _const_spec((d, d_s)), _const_spec((d_s, d)),
                  _const_spec((1, d)), _const_spec((1, d)), _const_spec((d, d)), _const_spec((1, d)),
                  _const_spec((d_p, d))],
        out_specs=row(d),
        out_shape=jax.ShapeDtypeStruct((t, d), F32),
        scratch_shapes=[pltpu.VMEM((d, 2 * d_s), BF16), pltpu.VMEM((d_s, d), BF16), pltpu.VMEM((d, d), BF16),
                        pltpu.VMEM((d_p, d), BF16)],
        compiler_params=pltpu.CompilerParams(dimension_semantics=("arbitrary",), vmem_limit_bytes=VMEM_LIMIT),
        name="final",
    )(h1, routed, p, w_sg, w_su, w_sd, g, bb, w_pg, b_pg, w_p)


def _proj_sample_kernel(x_ref, w_in_ref, lng_ref, lnb_ref, w00_ref, b0_ref,
                        k_ref, v_ref, q_ref, a_ref, gv_ref):
    d_a = lng_ref.shape[1]
    xb = x_ref[...].astype(BF16)

    def seg(s):
        return jnp.dot(xb, w_in_ref[:, s * d_a:(s + 1) * d_a].astype(BF16), preferred_element_type=F32)

    k_ref[...] = seg(3)
    v_ref[...] = seg(4)
    q_ref[...] = seg(2) * (B_HEAD_DIM ** -0.5)
    va = _layer_norm(_gelu_tail_exact(seg(1)), lng_ref[...], lnb_ref[...])
    gv_ref[...] = va
    sg = va * w00_ref[...] + b0_ref[...]
    a_ref[...] = (_gelu_tail_exact(seg(0)) * sg).astype(a_ref.dtype)


def _proj_sample(x, w_in, lng, lnb, w00_row, b0_row):
    n, d = x.shape
    d_in = w_in.shape[1]
    d_a = lng.shape[1]
    full = lambda w: _const_spec((n, w))
    f32 = jax.ShapeDtypeStruct((n, d_a), F32)
    return pl.pallas_call(
        _proj_sample_kernel,
        grid=(1,),
        in_specs=[full(d), _const_spec((d, d_in)), _const_spec((1, d_a)), _const_spec((1, d_a)),
                  _const_spec((1, d_a)), _const_spec((1, d_a))],
        out_specs=(full(d_a),) * 5,
        out_shape=(f32, f32, f32, jax.ShapeDtypeStruct((n, d_a), BF16), f32),
        compiler_params=pltpu.CompilerParams(dimension_semantics=("arbitrary",), vmem_limit_bytes=VMEM_LIMIT),
        name="proj_sample",
    )(x, w_in, lng, lnb, w00_row, b0_row)


def _decode_kernel(pt_ref, lam_ref, g_ref, q_ref, kn_ref, vn_ref, kp_ref, vp_ref, o_ref,
                   s_scr, acc, m_scr, z_scr, snew_scr):
    ph = pl.program_id(1)
    p = pl.program_id(2)
    n_maps = 2 * B_HEADS
    d_b = q_ref.shape[2]
    lane = lax.broadcasted_iota(jnp.int32, (n_maps, d_b), 1)
    mrow = lax.broadcasted_iota(jnp.int32, (n_maps, d_b), 0)
    own_map = jnp.where(mrow < B_HEADS, 2 * mrow, 2 * (mrow - B_HEADS) + 1)
    qrow = jnp.where(lane // B_HEAD_DIM == own_map, jnp.broadcast_to(q_ref[0], (n_maps, d_b)), 0.0).astype(BF16)

    @pl.when(ph == 0)
    def _scores():
        s_scr[p] = lax.dot_general(qrow, kp_ref[0].astype(BF16), _NT, preferred_element_type=F32)

        @pl.when(p == 0)
        def _new_key():
            kn = kn_ref[0].astype(BF16).astype(F32)
            snew_scr[...] = jnp.sum(qrow.astype(F32) * kn, axis=-1, keepdims=True)

    def diff_weights(w):
        lam = _lam(lam_ref)
        return (w[:B_HEADS] - lam * w[B_HEADS:]).astype(BF16)

    @pl.when(ph == 1)
    def _values():
        @pl.when(p == 0)
        def _normalisers():
            s_all = s_scr[...]
            snew = snew_scr[...]
            m = jnp.maximum(jnp.max(jnp.max(s_all, axis=0), axis=-1, keepdims=True), snew)
            e_new = jnp.exp(snew - m)
            z = jnp.sum(_tree_sum0(jnp.exp(s_all - m)), axis=-1, keepdims=True) + e_new
            m_scr[...] = m
            z_scr[...] = z
            wd_new = diff_weights(e_new / z)
            acc[...] = wd_new.astype(F32) * vn_ref[0].astype(BF16).astype(F32)

        w = jnp.exp(s_scr[p] - m_scr[...]) / z_scr[...]
        acc[...] += jnp.dot(diff_weights(w), vp_ref[0].astype(BF16), preferred_element_type=F32)

        @pl.when(p == pl.num_programs(2) - 1)
        def _finish():
            o = acc[...]
            for h in range(B_HEADS):
                oh = o[h:h + 1, h * B_V_DIM:(h + 1) * B_V_DIM]
                ms = jnp.mean(oh * oh, axis=-1, keepdims=True)
                o_ref[0, :, h * B_V_DIM:(h + 1) * B_V_DIM] = (
                    oh * lax.rsqrt(ms + LN_EPS) * g_ref[...] * (1.0 - LAM_INIT)).astype(o_ref.dtype)


def _decode_attn(page_table, lam_rows, subln_g, q, k_new, v_new, cache_k, cache_v):
    n, d_b = q.shape
    n_pages = page_table.shape[1]
    page = cache_k.shape[1]
    q3, k3, v3 = (a.reshape(n, 1, d_b) for a in (q, k_new, v_new))
    per_seq = pl.BlockSpec((1, 1, d_b), lambda b, ph, p, pt: (b, 0, 0))
    k_pages = pl.BlockSpec((1, page, d_b), lambda b, ph, p, pt: (pt[b, jnp.where(ph == 0, p, n_pages - 1)], 0, 0))
    v_pages = pl.BlockSpec((1, page, d_b), lambda b, ph, p, pt: (pt[b, jnp.where(ph == 0, 0, p)], 0, 0))
    out = pl.pallas_call(
        _decode_kernel,
        grid_spec=pltpu.PrefetchScalarGridSpec(
            num_scalar_prefetch=1,
            grid=(n, 2, n_pages),
            in_specs=[pl.BlockSpec((4, B_HEAD_DIM), lambda b, ph, p, pt: (0, 0)),
                      pl.BlockSpec((1, B_V_DIM), lambda b, ph, p, pt: (0, 0)),
                      per_seq, per_seq, per_seq, k_pages, v_pages],
            out_specs=per_seq,
            scratch_shapes=[pltpu.VMEM((n_pages, 2 * B_HEADS, page), F32), pltpu.VMEM((B_HEADS, d_b), F32),
                            pltpu.VMEM((2 * B_HEADS, 1), F32), pltpu.VMEM((2 * B_HEADS, 1), F32),
                            pltpu.VMEM((2 * B_HEADS, 1), F32)]),
        out_shape=jax.ShapeDtypeStruct((n, 1, d_b), BF16),
        compiler_params=pltpu.CompilerParams(dimension_semantics=("arbitrary", "arbitrary", "arbitrary"),
                                             vmem_limit_bytes=VMEM_LIMIT),
        name="decode_attn",
    )(page_table, lam_rows, subln_g, q3, k3, v3, cache_k, cache_v)
    return out.reshape(n, d_b)


def _dispatch(eidx, m_rows):
    n_tok = eidx.shape[0]
    n = n_tok * TOP_K
    e_flat = eidx.reshape(n)
    order = jnp.argsort(e_flat, stable=True).astype(jnp.int32)
    e_sorted = e_flat[order]
    counts = jnp.bincount(e_flat, length=N_EXPERTS).astype(jnp.int32)
    starts = jnp.cumsum(counts) - counts
    padded = (counts + MOE_TILE - 1) // MOE_TILE * MOE_TILE
    pends = jnp.cumsum(padded)
    row_sorted = (pends - padded)[e_sorted] + jnp.arange(n, dtype=jnp.int32) - starts[e_sorted]
    tok_buf = jnp.zeros((m_rows,), jnp.int32).at[row_sorted].set(order // TOP_K)
    row_of = jnp.zeros((n,), jnp.int32).at[order].set(row_sorted).reshape(n_tok, TOP_K)
    n_tiles = m_rows // MOE_TILE
    tile_expert = jnp.minimum(
        jnp.searchsorted(pends, jnp.arange(n_tiles, dtype=jnp.int32) * MOE_TILE, side='right'),
        N_EXPERTS - 1).astype(jnp.int32)
    n_used = (pends[-1:] // MOE_TILE).astype(jnp.int32)
    return tok_buf, row_of, tile_expert, n_used


def kernel(x_prompt, x_sample, cache_k, cache_v, page_table, p_prompt, p_sample, w_in, ln_v_g, ln_v_b, w_spatial,
           b_spatial, lambda_q1, lambda_k1, lambda_q2, lambda_k2, subln_g, w_out, ln1_g, ln1_b, w_router,
           router_bias, w_exp_gate, w_exp_up, w_exp_down, w_sh_gate, w_sh_up, w_sh_down, ln2_g, ln2_b, w_ple,
           w_ple_gate, b_ple_gate):
    assert w_in.shape[0] == DEPTH and x_sample.shape[1] == 1
    n_seq, seq, d = x_prompt.shape
    n_dec = x_sample.shape[0]
    t = n_seq * seq
    d_a = ln_v_g.shape[1]
    d_b = B_HEADS * B_V_DIM
    row1 = lambda a: a.reshape(1, -1)

    lam_rows = jnp.concatenate([lambda_q1, lambda_k1, lambda_q2, lambda_k2], axis=0)
    lng, lnb = row1(ln_v_g[0]), row1(ln_v_b[0])
    b_sp_full = jnp.repeat(b_spatial[0].T, d_a // A_HEADS, axis=1)
    w00_row = row1(jnp.repeat(w_spatial[0, :, 0, 0], d_a // A_HEADS))
    b0_row = b_sp_full[0:1]
    w_router_t = w_router[0].T
    rbias_col = router_bias[0].reshape(N_EXPERTS, 1)

    xp = x_prompt.reshape(t, d)
    k_p, v_p, qb, kb, vb, a_p, gv_p = _proj_prompt(xp, w_in[0], lng, lnb, w_spatial[0], b_sp_full, seq)
    b_p = _attn_prompt(lam_rows, row1(subln_g[0]), qb, kb, vb, n_seq, seq)
    h1_p, h1b_p, eidx_p, gw_p = _mix_route(a_p, b_p, xp, w_out[0], row1(ln1_g[0]), row1(ln1_b[0]),
                                           w_router_t, rbias_col)

    xs_ = x_sample.reshape(n_dec, d)
    k_s, v_s, q_s, a_s, gv_s = _proj_sample(xs_, w_in[0], lng, lnb, w00_row, b0_row)
    ck = cache_k[0].reshape(cache_k.shape[1], cache_k.shape[2], d_b)
    cv = cache_v[0].reshape(cache_v.shape[1], cache_v.shape[2], d_b)
    b_s = _decode_attn(page_table, lam_rows, row1(subln_g[0]), q_s, k_s, v_s, ck, cv)
    h1_s, h1b_s, eidx_s, gw_s = _mix_route(a_s, b_s, xs_, w_out[0], row1(ln1_g[0]), row1(ln1_b[0]),
                                           w_router_t, rbias_col)

    n_tok = t + n_dec
    m_rows = -(-(n_tok * TOP_K + N_EXPERTS * (MOE_TILE - 1)) // MOE_TILE) * MOE_TILE
    eidx = jnp.concatenate([eidx_p.T, eidx_s.T], axis=0)
    gw = jnp.concatenate([gw_p.T, gw_s.T], axis=0)
    h1b = jnp.concatenate([h1b_p, h1b_s], axis=0)
    tok_buf, row_of, tile_expert, n_used = _dispatch(eidx, m_rows)
    xs_sorted = jnp.take(h1b, tok_buf, axis=0)
    ys = _experts(tile_expert, n_used, xs_sorted, w_exp_gate[0], w_exp_up[0], w_exp_down[0])
    routed = jnp.einsum('tk,tkd->td', gw, jnp.take(ys, row_of, axis=0).astype(F32))

    fin = functools.partial(_final, w_sg=w_sh_gate[0], w_su=w_sh_up[0], w_sd=w_sh_down[0], g=row1(ln2_g[0]),
                            bb=row1(ln2_b[0]), w_pg=w_ple_gate[0], b_pg=row1(b_ple_gate[0]), w_p=w_ple[0])
    y_p = fin(h1_p, routed[:t], p_prompt[0].reshape(t, -1))
    y_s = fin(h1_s, routed[t:], p_sample[0].reshape(n_dec, -1))

    return (y_p.reshape(n_seq, seq, d), y_s.reshape(n_dec, 1, d),
            k_p.reshape(1, n_seq, seq, 2 * B_HEADS, B_HEAD_DIM), v_p.reshape(1, n_seq, seq, B_HEADS, B_V_DIM),
            gv_p.reshape(1, n_seq, CHUNK, d_a),
            k_s.reshape(1, n_dec, 1, 2 * B_HEADS, B_HEAD_DIM), v_s.reshape(1, n_dec, 1, B_HEADS, B_V_DIM),
            gv_s.reshape(1, n_dec, 1, d_a))
```

```python
import functools
import math

import jax
import jax.numpy as jnp
from jax import lax
from jax.experimental import pallas as pl
from jax.experimental.pallas import tpu as pltpu

F32 = jnp.float32
BF16 = jnp.bfloat16

A_HEADS = 8
CHUNK = 128
B_HEADS = 4
B_HEAD_DIM = 64
B_V_DIM = 128
N_EXPERTS = 256
N_GROUPS = 8
GROUP_SIZE = N_EXPERTS // N_GROUPS
TOPK_GROUPS = 4
TOP_K = 8
ROUTED_SCALE = 2.5
LN_EPS = 1e-5
DEPTH = 1
DEEPNORM_ALPHA = (2 * DEPTH) ** 0.25
LAM_INIT = 0.8 - 0.6 * math.exp(-0.3 * 0)

LANES = 128
ROW_TILE = 256
ATTN_TQ = 256
ATTN_TK = 256
MOE_TILE = 256
DECODE_CHUNK = 8
VMEM_LIMIT = 56 * 1024 * 1024

_NT = (((1,), (1,)), ((), ()))


def _const_spec(shape):
    nd = len(shape)
    return pl.BlockSpec(shape, lambda *_: (0,) * nd)


def _layer_norm(x, g, b):
    mu = jnp.mean(x, axis=-1, keepdims=True)
    xc = x - mu
    var = jnp.mean(xc * xc, axis=-1, keepdims=True)
    return xc * lax.rsqrt(var + LN_EPS) * g + b


def _gelu(x):
    return 0.5 * x * (1.0 + lax.erf(x * (2.0 ** -0.5)))


_ERFC_P = (2.326819970068386e-2, -1.387039388740657e-1, 3.687424674597105e-1, -5.824733027278666e-1,
           6.210004621745983e-1, -4.944515323274145e-1, 3.404879937665872e-1, -2.741127028184656e-1,
           5.638259427386472e-1)
_ERFC_R = (-1.047766399936249e+1, 1.297719955372516e+1, -7.495518717768503e+0, 2.921019019210786e+0,
           -1.015265279202700e+0, 4.218463358204948e-1, -2.820767439740514e-1, 5.641895067754075e-1)


def _horner(y, coeffs):
    r = jnp.full(y.shape, coeffs[0], F32)
    for c in coeffs[1:]:
        r = r * y + c
    return r


def _gelu_tail_exact(x):
    t = -x * (2.0 ** -0.5)
    y = jnp.abs(t)
    q = 1.0 / jnp.maximum(y, 1.0)
    q2 = q * q
    tail = jnp.exp(-t * t) * q * jnp.where(y < 2.0, _horner(q2, _ERFC_P), _horner(q2, _ERFC_R))
    tail = jnp.where(t < 0.0, 2.0 - tail, tail)
    return 0.5 * x * jnp.where(y < 1.0, 1.0 - lax.erf(t), tail)


def _tree_sum0(x):
    n = x.shape[0]
    while n > 1 and n % 2 == 0:
        n //= 2
        x = x[:n] + x[n:]
    return jnp.sum(x, axis=0)


def _lam(lam_ref):
    lv = lam_ref[...]
    s1 = jnp.sum(lv[0:1] * lv[1:2], axis=-1, keepdims=True)
    s2 = jnp.sum(lv[2:3] * lv[3:4], axis=-1, keepdims=True)
    return jnp.exp(s1) - jnp.exp(s2) + LAM_INIT


def _proj_kernel(x_ref, w_in_ref, lng_ref, lnb_ref, wsp_ref, bsp_ref,
                 k_ref, v_ref, qb_ref, kb_ref, vb_ref, a_ref, gv_ref,
                 wbf_ref, wcat_ref, *, tiles_per_seq):
    i = pl.program_id(0)
    tm = x_ref.shape[0]
    d_a = lng_ref.shape[1]

    @pl.when(i == 0)
    def _init():
        rows = w_in_ref.shape[0]
        step = 128

        def body(r, c):
            sl = pl.ds(pl.multiple_of(r * step, step), step)
            wbf_ref[sl, :] = w_in_ref[sl, :].astype(BF16)
            return c

        lax.fori_loop(0, rows // step, body, 0)
        row = lax.broadcasted_iota(jnp.int32, (CHUNK, CHUNK), 0)
        col = lax.broadcasted_iota(jnp.int32, (CHUNK, CHUNK), 1)
        for j in range(A_HEADS // 2):
            w0 = jnp.where(col <= row, wsp_ref[2 * j], 0.0)
            w1 = jnp.where(col <= row, wsp_ref[2 * j + 1], 0.0)
            wcat_ref[j] = jnp.concatenate([w0, w1], axis=1).astype(BF16)

    xb = x_ref[...].astype(BF16)

    def seg(s):
        return jnp.dot(xb, wbf_ref[:, s * d_a:(s + 1) * d_a], preferred_element_type=F32)

    k = seg(3)
    k_ref[...] = k
    kb_ref[...] = k.astype(BF16)
    v = seg(4)
    v_ref[...] = v
    vb_ref[...] = v.astype(BF16)
    qb_ref[...] = (seg(2) * (B_HEAD_DIM ** -0.5)).astype(BF16)

    va = _layer_norm(_gelu(seg(1)), lng_ref[...], lnb_ref[...])

    @pl.when(i % tiles_per_seq == tiles_per_seq - 1)
    def _gv():
        gv_ref[0] = va[tm - CHUNK:, :]

    u = _gelu(seg(0))
    vab = va.astype(BF16)
    lane = lax.broadcasted_iota(jnp.int32, (CHUNK, LANES), 1)
    zero = jnp.zeros((CHUNK, LANES), BF16)
    for c in range(tm // CHUNK):
        r0 = c * CHUNK
        for j in range(A_HEADS // 2):
            vp = vab[r0:r0 + CHUNK, j * LANES:(j + 1) * LANES]
            stacked = jnp.concatenate([jnp.where(lane < 64, vp, zero), jnp.where(lane >= 64, vp, zero)], axis=0)
            sg = jnp.dot(wcat_ref[j], stacked, preferred_element_type=F32) + bsp_ref[:, j * LANES:(j + 1) * LANES]
            a_ref[r0:r0 + CHUNK, j * LANES:(j + 1) * LANES] = (
                u[r0:r0 + CHUNK, j * LANES:(j + 1) * LANES] * sg).astype(BF16)


def _proj_prompt(x, w_in, lng, lnb, w_sp, b_sp_full, seq):
    t, d = x.shape
    d_in = w_in.shape[1]
    d_a = lng.shape[1]
    tm = min(ROW_TILE, seq)
    tiles_per_seq = seq // tm
    n_seq = t // seq
    row = lambda w: pl.BlockSpec((tm, w), lambda i: (i, 0))
    out_shape = (
        jax.ShapeDtypeStruct((t, d_a), F32), jax.ShapeDtypeStruct((t, d_a), F32),
        jax.ShapeDtypeStruct((t, d_a), BF16), jax.ShapeDtypeStruct((t, d_a), BF16),
        jax.ShapeDtypeStruct((t, d_a), BF16), jax.ShapeDtypeStruct((t, d_a), BF16),
        jax.ShapeDtypeStruct((n_seq, CHUNK, d_a), F32),
    )
    return pl.pallas_call(
        functools.partial(_proj_kernel, tiles_per_seq=tiles_per_seq),
        grid=(t // tm,),
        in_specs=[row(d), _const_spec((d, d_in)), _const_spec((1, d_a)), _const_spec((1, d_a)),
                  _const_spec((A_HEADS, CHUNK, CHUNK)), _const_spec((CHUNK, d_a))],
        out_specs=(row(d_a), row(d_a), row(d_a), row(d_a), row(d_a), row(d_a),
                   pl.BlockSpec((1, CHUNK, d_a), lambda i: (i // tiles_per_seq, 0, 0))),
        out_shape=out_shape,
        scratch_shapes=[pltpu.VMEM((d, d_in), BF16), pltpu.VMEM((A_HEADS // 2, CHUNK, 2 * CHUNK), BF16)],
        compiler_params=pltpu.CompilerParams(dimension_semantics=("arbitrary",), vmem_limit_bytes=VMEM_LIMIT),
        name="proj_prompt",
    )(x, w_in, lng, lnb, w_sp, b_sp_full)


def _attn_kernel(lam_ref, g_ref, q_ref, k_ref, v_ref, o_ref, acc1, acc2, m1, l1, m2, l2):
    i = pl.program_id(2)
    tq = q_ref.shape[0]
    tk = ATTN_TK if k_ref.shape[0] >= ATTN_TK else k_ref.shape[0]
    ratio = tq // tk
    qb = q_ref[...]
    lane = lax.broadcasted_iota(jnp.int32, qb.shape, 1)
    zero = jnp.zeros(qb.shape, BF16)
    q_lo = jnp.where(lane < B_HEAD_DIM, qb, zero)
    q_hi = jnp.where(lane >= B_HEAD_DIM, qb, zero)

    acc1[...] = jnp.zeros(acc1.shape, F32)
    acc2[...] = jnp.zeros(acc2.shape, F32)
    m1[...] = jnp.full(m1.shape, -jnp.inf, F32)
    m2[...] = jnp.full(m2.shape, -jnp.inf, F32)
    l1[...] = jnp.zeros(l1.shape, F32)
    l2[...] = jnp.zeros(l2.shape, F32)

    def one_map(qm, kb, vb, mask, acc, m, l):
        s = lax.dot_general(qm, kb, _NT, preferred_element_type=F32)
        if mask is not None:
            s = jnp.where(mask, s, -jnp.inf)
        m_old = m[...]
        m_new = jnp.maximum(m_old, jnp.max(s, axis=-1, keepdims=True))
        alpha = jnp.exp(m_old - m_new)
        p = jnp.exp(s - m_new)
        l[...] = alpha * l[...] + jnp.sum(p, axis=-1, keepdims=True)
        acc[...] = alpha * acc[...] + jnp.dot(p.astype(BF16), vb, preferred_element_type=F32)
        m[...] = m_new

    def block(j, mask):
        sl = pl.ds(pl.multiple_of(j * tk, tk), tk)
        kb = k_ref[sl, :]
        vb = v_ref[sl, :]
        one_map(q_lo, kb, vb, mask, acc1, m1, l1)
        one_map(q_hi, kb, vb, mask, acc2, m2, l2)

    def full_body(j, c):
        block(j, None)
        return c

    lax.fori_loop(0, i * ratio, full_body, 0)
    row = lax.broadcasted_iota(jnp.int32, (tq, tk), 0)
    col = lax.broadcasted_iota(jnp.int32, (tq, tk), 1)
    for jj in range(ratio):
        block(i * ratio + jj, col + jj * tk <= row)

    lam = _lam(lam_ref)
    o = acc1[...] / l1[...] - lam * (acc2[...] / l2[...])
    ms = jnp.mean(o * o, axis=-1, keepdims=True)
    o_ref[...] = (o * lax.rsqrt(ms + LN_EPS) * g_ref[...] * (1.0 - LAM_INIT)).astype(o_ref.dtype)


def _attn_prompt(lam_rows, subln_g, qb, kb, vb, n_seq, seq):
    t, d_b = qb.shape
    tq = min(ATTN_TQ, seq)
    nq = seq // tq
    return pl.pallas_call(
        _attn_kernel,
        grid=(n_seq, B_HEADS, nq),
        in_specs=[_const_spec((4, B_HEAD_DIM)), _const_spec((1, B_V_DIM)),
                  pl.BlockSpec((tq, LANES), lambda b, h, i: (b * nq + i, h)),
                  pl.BlockSpec((seq, LANES), lambda b, h, i: (b, h)),
                  pl.BlockSpec((seq, LANES), lambda b, h, i: (b, h))],
        out_specs=pl.BlockSpec((tq, LANES), lambda b, h, i: (b * nq + i, h)),
        out_shape=jax.ShapeDtypeStruct((t, d_b), BF16),
        scratch_shapes=[pltpu.VMEM((tq, B_V_DIM), F32), pltpu.VMEM((tq, B_V_DIM), F32),
                        pltpu.VMEM((tq, 1), F32), pltpu.VMEM((tq, 1), F32),
                        pltpu.VMEM((tq, 1), F32), pltpu.VMEM((tq, 1), F32)],
        compiler_params=pltpu.CompilerParams(dimension_semantics=("arbitrary", "arbitrary", "arbitrary"),
                                             vmem_limit_bytes=VMEM_LIMIT),
        name="attn_prompt",
    )(lam_rows, subln_g, qb, kb, vb)


def _route(logits_t, bias_col):
    e, n = logits_t.shape
    scores = jax.nn.sigmoid(logits_t)
    biased = scores + bias_col
    ninf = jnp.float32(-jnp.inf)
    gs = []
    sub = lax.broadcasted_iota(jnp.int32, (GROUP_SIZE, n), 0)
    for g in range(N_GROUPS):
        blk = biased[g * GROUP_SIZE:(g + 1) * GROUP_SIZE]
        t1 = jnp.max(blk, axis=0, keepdims=True)
        i1 = jnp.min(jnp.where(blk == t1, sub, GROUP_SIZE), axis=0, keepdims=True)
        t2 = jnp.max(jnp.where(sub == i1, ninf, blk), axis=0, keepdims=True)
        gs.append(t1 + t2)
    keep = []
    for g in range(N_GROUPS):
        cnt = jnp.zeros((1, n), jnp.int32)
        for o in range(N_GROUPS):
            if o == g:
                continue
            beats = (gs[o] >= gs[g]) if o < g else (gs[o] > gs[g])
            cnt = cnt + beats.astype(jnp.int32)
        keep.append(cnt < TOPK_GROUPS)
    masked = jnp.concatenate(
        [jnp.where(keep[g], biased[g * GROUP_SIZE:(g + 1) * GROUP_SIZE], ninf) for g in range(N_GROUPS)], axis=0)
    rid = lax.broadcasted_iota(jnp.int32, (e, n), 0)
    idxs, vals, hits = [], [], []
    for _ in range(TOP_K):
        top = jnp.max(masked, axis=0, keepdims=True)
        idx = jnp.min(jnp.where(masked == top, rid, e), axis=0, keepdims=True)
        hit = rid == idx
        vals.append(jnp.sum(jnp.where(hit, scores, 0.0), axis=0, keepdims=True))
        idxs.append(idx)
        hits.append(hit)
        masked = jnp.where(hit, ninf, masked)
    eidx = jnp.concatenate(idxs, axis=0)
    gw = jnp.concatenate(vals, axis=0)
    gw = gw / jnp.sum(gw, axis=0, keepdims=True) * ROUTED_SCALE
    return eidx, gw, hits


def _mix_kernel(a_ref, b_ref, x_ref, wout_ref, g_ref, bb_ref, wrt_ref, rb_ref, base_ref,
                h1_ref, eidx_ref, gw_ref, rank_ref, cnt_ref, wo_s, wr_s, before_s, seen_s):
    i = pl.program_id(0)
    d_a = a_ref.shape[1]
    tm = x_ref.shape[0]

    @pl.when(i == 0)
    def _init():
        wo_s[...] = wout_ref[...].astype(BF16)
        wr_s[...] = wrt_ref[...].astype(BF16)
        r = lax.broadcasted_iota(jnp.int32, (tm, tm), 0)
        c = lax.broadcasted_iota(jnp.int32, (tm, tm), 1)
        before_s[...] = jnp.where(r < c, 1.0, 0.0).astype(BF16)
        seen_s[...] = base_ref[...]

    dot = functools.partial(jnp.dot, preferred_element_type=F32)
    mix = dot(a_ref[...], wo_s[:d_a, :]) + dot(b_ref[...], wo_s[d_a:, :])
    h1 = _layer_norm(DEEPNORM_ALPHA * x_ref[...] + mix, g_ref[...], bb_ref[...])
    h1_ref[...] = h1
    logits_t = lax.dot_general(wr_s[...], h1.astype(BF16), _NT, preferred_element_type=F32)
    eidx, gw, hits = _route(logits_t, rb_ref[...])
    eidx_ref[...] = eidx
    gw_ref[...] = gw
    onehot = jnp.zeros(logits_t.shape, F32)
    for hit in hits:
        onehot = jnp.where(hit, 1.0, onehot)
    earlier = dot(onehot.astype(BF16), before_s[...]) + seen_s[...]
    rank_ref[...] = jnp.concatenate(
        [jnp.sum(jnp.where(hit, earlier, 0.0), axis=0, keepdims=True) for hit in hits], axis=0).astype(jnp.int32)
    seen = seen_s[...] + jnp.sum(onehot, axis=1, keepdims=True)
    seen_s[...] = seen
    cnt_ref[...] = seen


def _mix_route(a, b, x, w_out, g, bb, w_router_t, rbias_col, base_counts):
    t, d = x.shape
    d_a = a.shape[1]
    tm = min(ROW_TILE, t)
    row = lambda w: pl.BlockSpec((tm, w), lambda i: (i, 0))
    colb = pl.BlockSpec((TOP_K, tm), lambda i: (0, i))
    return pl.pallas_call(
        _mix_kernel,
        grid=(t // tm,),
        in_specs=[row(d_a), row(d_a), row(d), _const_spec((d, d)), _const_spec((1, d)), _const_spec((1, d)),
                  _const_spec((N_EXPERTS, d)), _const_spec((N_EXPERTS, 1)), _const_spec((N_EXPERTS, 1))],
        out_specs=(row(d), colb, colb, colb, _const_spec((N_EXPERTS, 1))),
        out_shape=(jax.ShapeDtypeStruct((t, d), F32),
                   jax.ShapeDtypeStruct((TOP_K, t), jnp.int32), jax.ShapeDtypeStruct((TOP_K, t), F32),
                   jax.ShapeDtypeStruct((TOP_K, t), jnp.int32), jax.ShapeDtypeStruct((N_EXPERTS, 1), F32)),
        scratch_shapes=[pltpu.VMEM((d, d), BF16), pltpu.VMEM((N_EXPERTS, d), BF16), pltpu.VMEM((tm, tm), BF16),
                        pltpu.VMEM((N_EXPERTS, 1), F32)],
        compiler_params=pltpu.CompilerParams(dimension_semantics=("arbitrary",), vmem_limit_bytes=VMEM_LIMIT),
        name="mix_route",
    )(a, b, x, w_out, g, bb, w_router_t, rbias_col, base_counts)


def _row_copy(src, src_row, dst, dst_row, sem):
    return pltpu.make_async_copy(src.at[pl.ds(src_row, 1), :], dst.at[pl.ds(dst_row, 1), :], sem)


def _dispatch_rows(pstart_ref, eidx_ref, rank_ref, h1_ref, xs_ref, row_ref, sem):
    tm = h1_ref.shape[0]

    def start_token(j, c):
        for k in range(TOP_K):
            row = pstart_ref[eidx_ref[k, j]] + rank_ref[k, j]
            row_ref[k, j] = row
            _row_copy(h1_ref, j, xs_ref, row, sem).start()
        return c

    def wait_token(j, c):
        for k in range(TOP_K):
            _row_copy(h1_ref, 0, xs_ref, 0, sem).wait()
        return c

    lax.fori_loop(0, tm, start_token, 0)
    lax.fori_loop(0, tm, wait_token, 0)


def _dispatch_first_kernel(pstart_ref, cnt_ref, eidx_ref, rank_ref, h1_ref, xs_ref, row_ref, zero_s, sem):
    @pl.when(pl.program_id(0) == 0)
    def _pad():
        zero_s[...] = jnp.zeros(zero_s.shape, F32)

        def per_expert(action):
            def body(e, c):
                lo = pstart_ref[e] + cnt_ref[e]
                hi = pstart_ref[e] + (cnt_ref[e] + MOE_TILE - 1) // MOE_TILE * MOE_TILE

                def one(r, cc):
                    action(_row_copy(zero_s, 0, xs_ref, r, sem))
                    return cc

                lax.fori_loop(lo, hi, one, 0)
                return c
            return body

        lax.fori_loop(0, N_EXPERTS, per_expert(lambda cp: cp.start()), 0)
        lax.fori_loop(0, N_EXPERTS, per_expert(lambda cp: cp.wait()), 0)

    _dispatch_rows(pstart_ref, eidx_ref, rank_ref, h1_ref, xs_ref, row_ref, sem)


def _dispatch_more_kernel(pstart_ref, cnt_ref, eidx_ref, rank_ref, h1_ref, xs_in_ref, xs_ref, row_ref, sem):
    del cnt_ref, xs_in_ref
    _dispatch_rows(pstart_ref, eidx_ref, rank_ref, h1_ref, xs_ref, row_ref, sem)


def _dispatch(pstart, counts, eidx, rank, h1, xs, m_rows):
    t, d = h1.shape
    tm = min(ROW_TILE, t)
    first = xs is None
    smem_blk = pl.BlockSpec((TOP_K, tm), lambda i, *_: (0, i), memory_space=pltpu.SMEM)
    any_spec = pl.BlockSpec(memory_space=pl.ANY)
    in_specs = [smem_blk, smem_blk, pl.BlockSpec((tm, d), lambda i, *_: (i, 0))]
    scratch = [pltpu.SemaphoreType.DMA(())]
    return pl.pallas_call(
        _dispatch_first_kernel if first else _dispatch_more_kernel,
        grid_spec=pltpu.PrefetchScalarGridSpec(
            num_scalar_prefetch=2,
            grid=(t // tm,),
            in_specs=in_specs if first else in_specs + [any_spec],
            out_specs=(any_spec, smem_blk),
            scratch_shapes=[pltpu.VMEM((8, d), F32)] + scratch if first else scratch),
        out_shape=(jax.ShapeDtypeStruct((m_rows, d), F32), jax.ShapeDtypeStruct((TOP_K, t), jnp.int32)),
        input_output_aliases={} if first else {5: 0},
        compiler_params=pltpu.CompilerParams(dimension_semantics=("arbitrary",), vmem_limit_bytes=VMEM_LIMIT),
        name="dispatch_first" if first else "dispatch_more",
    )(*((pstart, counts, eidx, rank, h1) if first else (pstart, counts, eidx, rank, h1, xs)))


def _experts_kernel(te_ref, nu_ref, x_ref, wg_ref, wu_ref, wd_ref, y_ref, wgu_s, wd_s):
    t = pl.program_id(0)
    prev = te_ref[jnp.maximum(t - 1, 0)]
    d_e = wg_ref.shape[2]

    @pl.when((t < nu_ref[0]) & ((t == 0) | (te_ref[t] != prev)))
    def _cast():
        wgu_s[:, :d_e] = wg_ref[0].astype(BF16)
        wgu_s[:, d_e:] = wu_ref[0].astype(BF16)
        wd_s[...] = wd_ref[0].astype(BF16)

    @pl.when(t < nu_ref[0])
    def _compute():
        gu = jnp.dot(x_ref[...].astype(BF16), wgu_s[...], preferred_element_type=F32)
        hid = jax.nn.silu(gu[:, :d_e]) * gu[:, d_e:]
        y_ref[...] = jnp.dot(hid.astype(BF16), wd_s[...], preferred_element_type=F32)


def _experts(tile_expert, n_used, xs, w_g, w_u, w_d):
    m, d = xs.shape
    d_e = w_g.shape[2]
    n_tiles = m // MOE_TILE

    def rows(t, te, nu):
        return (jnp.minimum(t, nu[0] - 1), 0)

    def wmap(t, te, nu):
        return (te[jnp.minimum(t, nu[0] - 1)], 0, 0)

    return pl.pallas_call(
        _experts_kernel,
        grid_spec=pltpu.PrefetchScalarGridSpec(
            num_scalar_prefetch=2,
            grid=(n_tiles,),
            in_specs=[pl.BlockSpec((MOE_TILE, d), rows),
                      pl.BlockSpec((1, d, d_e), wmap), pl.BlockSpec((1, d, d_e), wmap),
                      pl.BlockSpec((1, d_e, d), wmap)],
            out_specs=pl.BlockSpec((MOE_TILE, d), rows),
            scratch_shapes=[pltpu.VMEM((d, 2 * d_e), BF16), pltpu.VMEM((d_e, d), BF16)]),
        out_shape=jax.ShapeDtypeStruct((m, d), F32),
        compiler_params=pltpu.CompilerParams(dimension_semantics=("arbitrary",), vmem_limit_bytes=VMEM_LIMIT),
        name="experts",
    )(tile_expert, n_used, xs, w_g, w_u, w_d)


def _final_kernel(rows_ref, rows_next_ref, h1_ref, gw_ref, p_ref, ys_ref, wsg_ref, wsu_ref, wsd_ref, g_ref, bb_ref,
                  wpg_ref, bpg_ref, wp_ref, y_ref, wgu_s, wd_s, wpg_s, wp_s, ybuf, sems):
    i = pl.program_id(0)
    n = pl.num_programs(0)
    d_s = wsg_ref.shape[1]
    tm = h1_ref.shape[0]

    def gather(idx_ref, slot, action):
        def body(j, c):
            for k in range(TOP_K):
                src_row = idx_ref[k, j] if action == "start" else 0
                cp = pltpu.make_async_copy(ys_ref.at[pl.ds(src_row, 1), :], ybuf.at[slot, k, pl.ds(j, 1), :],
                                           sems.at[slot])
                cp.start() if action == "start" else cp.wait()
            return c
        lax.fori_loop(0, tm, body, 0)

    @pl.when(i == 0)
    def _init():
        gather(rows_ref, 0, "start")
        wgu_s[:, :d_s] = wsg_ref[...].astype(BF16)
        wgu_s[:, d_s:] = wsu_ref[...].astype(BF16)
        wd_s[...] = wsd_ref[...].astype(BF16)
        wpg_s[...] = wpg_ref[...].astype(BF16)
        wp_s[...] = wp_ref[...].astype(BF16)

    slot = i % 2

    @pl.when(i + 1 < n)
    def _prefetch():
        gather(rows_next_ref, 1 - slot, "start")

    dot = functools.partial(jnp.dot, preferred_element_type=F32)
    h1 = h1_ref[...]
    gu = dot(h1.astype(BF16), wgu_s[...])
    hid = jax.nn.silu(gu[:, :d_s]) * gu[:, d_s:]
    shared = dot(hid.astype(BF16), wd_s[...])

    gather(rows_ref, slot, "wait")
    gwv = gw_ref[...]
    routed = gwv[:, 0:1] * ybuf[slot, 0]
    for k in range(1, TOP_K):
        routed = routed + gwv[:, k:k + 1] * ybuf[slot, k]

    h2 = _layer_norm(DEEPNORM_ALPHA * h1 + (routed + shared), g_ref[...], bb_ref[...])
    gate = jax.nn.sigmoid(dot(h2.astype(BF16), wpg_s[...]) + bpg_ref[...])
    y_ref[...] = h2 + gate * dot(p_ref[...].astype(BF16), wp_s[...])


def _final(row_of, h1, gw, p, ys, w_sg, w_su, w_sd, g, bb, w_pg, b_pg, w_p):
    t, d = h1.shape
    d_s = w_sg.shape[1]
    d_p = p.shape[1]
    tm = min(ROW_TILE, t)
    n = t // tm
    row = lambda w: pl.BlockSpec((tm, w), lambda i: (i, 0))
    return pl.pallas_call(
        _final_kernel,
        grid=(n,),
        in_specs=[pl.BlockSpec((TOP_K, tm), lambda i: (0, i), memory_space=pltpu.SMEM),
                  pl.BlockSpec((TOP_K, tm), lambda i: (0, jnp.minimum(i + 1, n - 1)), memory_space=pltpu.SMEM),
                  row(d), row(TOP_K), row(d_p), pl.BlockSpec(memory_space=pl.ANY),
                  _const_spec((d, d_s)), _const_spec((d, d_s)), _const_spec((d_s, d)),
                  _const_spec((1, d)), _const_spec((1, d)), _const_spec((d, d)), _const_spec((1, d)),
                  _const_spec((d_p, d))],
        out_specs=row(d),
        out_shape=jax.ShapeDtypeStruct((t, d), F32),
        scratch_shapes=[pltpu.VMEM((d, 2 * d_s), BF16), pltpu.VMEM((d_s, d), BF16), pltpu.VMEM((d, d), BF16),
                        pltpu.VMEM((d_p, d), BF16), pltpu.VMEM((2, TOP_K, tm, d), F32),
                        pltpu.SemaphoreType.DMA((2,))],
        compiler_params=pltpu.CompilerParams(dimension_semantics=("arbitrary",), vmem_limit_bytes=VMEM_LIMIT),
        name="final",
    )(row_of, row_of, h1, gw, p, ys, w_sg, w_su, w_sd, g, bb, w_pg, b_pg, w_p)


def _proj_sample_kernel(x_ref, w_in_ref, lng_ref, lnb_ref, w00_ref, b0_ref,
                        k_ref, v_ref, q_ref, a_ref, gv_ref):
    d_a = lng_ref.shape[1]
    xb = x_ref[...].astype(BF16)

    def seg(s):
        return jnp.dot(xb, w_in_ref[:, s * d_a:(s + 1) * d_a].astype(BF16), preferred_element_type=F32)

    k_ref[...] = seg(3)
    v_ref[...] = seg(4)
    q_ref[...] = seg(2) * (B_HEAD_DIM ** -0.5)
    va = _layer_norm(_gelu_tail_exact(seg(1)), lng_ref[...], lnb_ref[...])
    gv_ref[...] = va
    sg = va * w00_ref[...] + b0_ref[...]
    a_ref[...] = (_gelu_tail_exact(seg(0)) * sg).astype(a_ref.dtype)


def _proj_sample(x, w_in, lng, lnb, w00_row, b0_row):
    n, d = x.shape
    d_in = w_in.shape[1]
    d_a = lng.shape[1]
    full = lambda w: _const_spec((n, w))
    f32 = jax.ShapeDtypeStruct((n, d_a), F32)
    return pl.pallas_call(
        _proj_sample_kernel,
        grid=(1,),
        in_specs=[full(d), _const_spec((d, d_in)), _const_spec((1, d_a)), _const_spec((1, d_a)),
                  _const_spec((1, d_a)), _const_spec((1, d_a))],
        out_specs=(full(d_a),) * 5,
        out_shape=(f32, f32, f32, jax.ShapeDtypeStruct((n, d_a), BF16), f32),
        compiler_params=pltpu.CompilerParams(dimension_semantics=("arbitrary",), vmem_limit_bytes=VMEM_LIMIT),
        name="proj_sample",
    )(x, w_in, lng, lnb, w00_row, b0_row)


def _decode_kernel(pt_ref, lam_ref, g_ref, q_ref, kn_ref, vn_ref, kt_ref, vr_ref, o_ref,
                   kbuf, vbuf, ksem, vsem, s_scr, acc):
    b = pl.program_id(0)
    n_seq = pl.num_programs(0)
    n_maps = 2 * B_HEADS
    d_b = q_ref.shape[2]
    chunk = kbuf.shape[1]
    n_chunks = s_scr.shape[0] // chunk
    page = s_scr.shape[2]

    def chunk_copies(cache_ref, buf, sem, seq, c, slot):
        return [pltpu.make_async_copy(cache_ref.at[pt_ref[seq, c * chunk + j]], buf.at[slot, j], sem.at[slot])
                for j in range(chunk)]

    def start(copies):
        for cp in copies:
            cp.start()

    def wait(copies):
        for cp in copies:
            cp.wait()

    @pl.when(b == 0)
    def _first_chunk():
        start(chunk_copies(kt_ref, kbuf, ksem, 0, 0, 0))

    lane = lax.broadcasted_iota(jnp.int32, (n_maps, d_b), 1)
    mrow = lax.broadcasted_iota(jnp.int32, (n_maps, d_b), 0)
    own_map = jnp.where(mrow < B_HEADS, 2 * mrow, 2 * (mrow - B_HEADS) + 1)
    qrow = jnp.where(lane // B_HEAD_DIM == own_map, jnp.broadcast_to(q_ref[0], (n_maps, d_b)), 0.0).astype(BF16)

    def score_chunk(c, carry):
        slot = c % 2

        @pl.when(c + 1 < n_chunks)
        def _():
            start(chunk_copies(kt_ref, kbuf, ksem, b, c + 1, 1 - slot))

        @pl.when(c + 1 == n_chunks)
        def _():
            start(chunk_copies(vr_ref, vbuf, vsem, b, 0, 0))

        wait(chunk_copies(kt_ref, kbuf, ksem, b, c, slot))
        for j in range(chunk):
            s_scr[c * chunk + j] = jnp.dot(qrow, kbuf[slot, j].astype(BF16), preferred_element_type=F32)
        return carry

    lax.fori_loop(0, n_chunks, score_chunk, 0)

    lam = _lam(lam_ref)

    def diff_weights(w):
        top = w[:B_HEADS] - lam * w[B_HEADS:]
        return jnp.concatenate([top, jnp.zeros_like(top)], axis=0).astype(BF16)

    kn = kn_ref[0].astype(BF16).astype(F32)
    snew = jnp.sum(qrow.astype(F32) * kn, axis=-1, keepdims=True)
    s_all = s_scr[...]
    m = jnp.maximum(jnp.max(jnp.max(s_all, axis=0), axis=-1, keepdims=True), snew)
    e_new = jnp.exp(snew - m)
    z = jnp.sum(_tree_sum0(jnp.exp(s_all - m)), axis=-1, keepdims=True) + e_new
    wd_new = diff_weights(e_new / z).astype(F32)
    vn = vn_ref[0].astype(BF16).astype(F32)
    for h in range(B_HEADS):
        acc[h:h + 1, :] = wd_new[h:h + 1] * vn[:, h * B_V_DIM:(h + 1) * B_V_DIM]

    def value_chunk(c, carry):
        slot = c % 2

        @pl.when(c + 1 < n_chunks)
        def _():
            start(chunk_copies(vr_ref, vbuf, vsem, b, c + 1, 1 - slot))

        @pl.when((c + 1 == n_chunks) & (b + 1 < n_seq))
        def _():
            start(chunk_copies(kt_ref, kbuf, ksem, b + 1, 0, 0))

        wait(chunk_copies(vr_ref, vbuf, vsem, b, c, slot))
        for j in range(chunk):
            wd = diff_weights(jnp.exp(s_scr[c * chunk + j] - m) / z)
            for h in range(B_HEADS):
                vh = vbuf[slot, j, pl.ds(h, page, stride=B_HEADS), :].astype(BF16)
                acc[h:h + 1, :] += jnp.dot(wd, vh, preferred_element_type=F32)[h:h + 1]
        return carry

    lax.fori_loop(0, n_chunks, value_chunk, 0)

    for h in range(B_HEADS):
        oh = acc[h:h + 1, :]
        ms = jnp.mean(oh * oh, axis=-1, keepdims=True)
        o_ref[0, :, h * B_V_DIM:(h + 1) * B_V_DIM] = (
            oh * lax.rsqrt(ms + LN_EPS) * g_ref[...] * (1.0 - LAM_INIT)).astype(o_ref.dtype)


def _decode_attn(page_table, lam_rows, subln_g, q, k_new, v_new, cache_k, cache_v):
    n, d_b = q.shape
    n_pages = page_table.shape[1]
    page = cache_k.shape[2]
    chunk = math.gcd(n_pages, DECODE_CHUNK)
    q3, k3, v3 = (a.reshape(n, 1, d_b) for a in (q, k_new, v_new))
    per_seq = pl.BlockSpec((1, 1, d_b), lambda b, pt: (b, 0, 0))
    any_spec = pl.BlockSpec(memory_space=pl.ANY)
    out = pl.pallas_call(
        _decode_kernel,
        grid_spec=pltpu.PrefetchScalarGridSpec(
            num_scalar_prefetch=1,
            grid=(n,),
            in_specs=[pl.BlockSpec((4, B_HEAD_DIM), lambda b, pt: (0, 0)),
                      pl.BlockSpec((1, B_V_DIM), lambda b, pt: (0, 0)),
                      per_seq, per_seq, per_seq, any_spec, any_spec],
            out_specs=per_seq,
            scratch_shapes=[pltpu.VMEM((2, chunk) + cache_k.shape[1:], F32),
                            pltpu.VMEM((2, chunk) + cache_v.shape[1:], F32),
                            pltpu.SemaphoreType.DMA((2,)), pltpu.SemaphoreType.DMA((2,)),
                            pltpu.VMEM((n_pages, 2 * B_HEADS, page), F32), pltpu.VMEM((B_HEADS, B_V_DIM), F32)]),
        out_shape=jax.ShapeDtypeStruct((n, 1, d_b), BF16),
        compiler_params=pltpu.CompilerParams(dimension_semantics=("arbitrary",), vmem_limit_bytes=VMEM_LIMIT),
        name="decode_attn",
    )(page_table, lam_rows, subln_g, q3, k3, v3, cache_k, cache_v)
    return out.reshape(n, d_b)


def _tile_plan(counts, m_rows):
    padded = (counts + MOE_TILE - 1) // MOE_TILE * MOE_TILE
    pends = jnp.cumsum(padded)
    n_tiles = m_rows // MOE_TILE
    tile_expert = jnp.minimum(
        jnp.searchsorted(pends, jnp.arange(n_tiles, dtype=jnp.int32) * MOE_TILE, side='right'),
        N_EXPERTS - 1).astype(jnp.int32)
    n_used = (pends[-1:] // MOE_TILE).astype(jnp.int32)
    return (pends - padded).astype(jnp.int32), tile_expert, n_used


def kernel(x_prompt, x_sample, cache_k, cache_v, page_table, p_prompt, p_sample, w_in, ln_v_g, ln_v_b, w_spatial,
           b_spatial, lambda_q1, lambda_k1, lambda_q2, lambda_k2, subln_g, w_out, ln1_g, ln1_b, w_router,
           router_bias, w_exp_gate, w_exp_up, w_exp_down, w_sh_gate, w_sh_up, w_sh_down, ln2_g, ln2_b, w_ple,
           w_ple_gate, b_ple_gate):
    assert w_in.shape[0] == DEPTH and x_sample.shape[1] == 1
    n_seq, seq, d = x_prompt.shape
    n_dec = x_sample.shape[0]
    t = n_seq * seq
    d_a = ln_v_g.shape[1]
    d_b = B_HEADS * B_V_DIM
    row1 = lambda a: a.reshape(1, -1)

    lam_rows = jnp.concatenate([lambda_q1, lambda_k1, lambda_q2, lambda_k2], axis=0)
    lng, lnb = row1(ln_v_g[0]), row1(ln_v_b[0])
    b_sp_full = jnp.repeat(b_spatial[0].T, d_a // A_HEADS, axis=1)
    w00_row = row1(jnp.repeat(w_spatial[0, :, 0, 0], d_a // A_HEADS))
    b0_row = b_sp_full[0:1]
    w_router_t = w_router[0].T
    rbias_col = router_bias[0].reshape(N_EXPERTS, 1)

    xp = x_prompt.reshape(t, d)
    k_p, v_p, qb, kb, vb, a_p, gv_p = _proj_prompt(xp, w_in[0], lng, lnb, w_spatial[0], b_sp_full, seq)
    b_p = _attn_prompt(lam_rows, row1(subln_g[0]), qb, kb, vb, n_seq, seq)
    mix = functools.partial(_mix_route, w_out=w_out[0], g=row1(ln1_g[0]), bb=row1(ln1_b[0]),
                            w_router_t=w_router_t, rbias_col=rbias_col)
    h1_p, eidx_p, gw_p, rank_p, counts_p = mix(a_p, b_p, xp, base_counts=jnp.zeros((N_EXPERTS, 1), F32))

    xs_ = x_sample.reshape(n_dec, d)
    k_s, v_s, q_s, a_s, gv_s = _proj_sample(xs_, w_in[0], lng, lnb, w00_row, b0_row)
    n_phys, page = cache_k.shape[1], cache_k.shape[2]
    ck = cache_k[0].transpose(0, 2, 3, 1).reshape(n_phys, 2 * B_HEADS * B_HEAD_DIM, page)
    cv = cache_v[0].reshape(n_phys, page * B_HEADS, B_V_DIM)
    b_s = _decode_attn(page_table, lam_rows, row1(subln_g[0]), q_s, k_s, v_s, ck, cv)
    h1_s, eidx_s, gw_s, rank_s, counts = mix(a_s, b_s, xs_, base_counts=counts_p)

    m_rows = -(-((t + n_dec) * TOP_K + N_EXPERTS * (MOE_TILE - 1)) // MOE_TILE) * MOE_TILE
    counts = counts.reshape(N_EXPERTS).astype(jnp.int32)
    pstart, tile_expert, n_used = _tile_plan(counts, m_rows)
    xs_sorted, rows_p = _dispatch(pstart, counts, eidx_p, rank_p, h1_p, None, m_rows)
    xs_sorted, rows_s = _dispatch(pstart, counts, eidx_s, rank_s, h1_s, xs_sorted, m_rows)
    ys = _experts(tile_expert, n_used, xs_sorted, w_exp_gate[0], w_exp_up[0], w_exp_down[0])

    fin = functools.partial(_final, ys=ys, w_sg=w_sh_gate[0], w_su=w_sh_up[0], w_sd=w_sh_down[0], g=row1(ln2_g[0]),
                            bb=row1(ln2_b[0]), w_pg=w_ple_gate[0], b_pg=row1(b_ple_gate[0]), w_p=w_ple[0])
    y_p = fin(rows_p, h1_p, gw_p.T, p_prompt[0].reshape(t, -1))
    y_s = fin(rows_s, h1_s, gw_s.T, p_sample[0].reshape(n_dec, -1))

    return (y_p.reshape(n_seq, seq, d), y_s.reshape(n_dec, 1, d),
            k_p.reshape(1, n_seq, seq, 2 * B_HEADS, B_HEAD_DIM), v_p.reshape(1, n_seq, seq, B_HEADS, B_V_DIM),
            gv_p.reshape(1, n_seq, CHUNK, d_a),
            k_s.reshape(1, n_dec, 1, 2 * B_HEADS, B_HEAD_DIM), v_s.reshape(1, n_dec, 1, B_HEADS, B_V_DIM),
            gv_s.reshape(1, n_dec, 1, d_a))
```

```python
import functools
import math

import jax
import jax.numpy as jnp
from jax import lax
from jax.experimental import pallas as pl
from jax.experimental.pallas import tpu as pltpu

F32 = jnp.float32
BF16 = jnp.bfloat16

A_HEADS = 8
CHUNK = 128
B_HEADS = 4
B_HEAD_DIM = 64
B_V_DIM = 128
N_EXPERTS = 256
N_GROUPS = 8
GROUP_SIZE = N_EXPERTS // N_GROUPS
TOPK_GROUPS = 4
TOP_K = 8
ROUTED_SCALE = 2.5
LN_EPS = 1e-5
DEPTH = 1
DEEPNORM_ALPHA = (2 * DEPTH) ** 0.25
LAM_INIT = 0.8 - 0.6 * math.exp(-0.3 * 0)

LANES = 128
ROW_TILE = 256
ATTN_TQ = 256
ATTN_TK = 512
MOE_TILE = 256
DECODE_CHUNK = 8
VMEM_LIMIT = 56 * 1024 * 1024

_NT = (((1,), (1,)), ((), ()))


def _const_spec(shape):
    nd = len(shape)
    return pl.BlockSpec(shape, lambda *_: (0,) * nd)


def _layer_norm(x, g, b):
    mu = jnp.mean(x, axis=-1, keepdims=True)
    xc = x - mu
    var = jnp.mean(xc * xc, axis=-1, keepdims=True)
    return xc * lax.rsqrt(var + LN_EPS) * g + b


def _gelu(x):
    return 0.5 * x * (1.0 + lax.erf(x * (2.0 ** -0.5)))


_ERFC_P = (2.326819970068386e-2, -1.387039388740657e-1, 3.687424674597105e-1, -5.824733027278666e-1,
           6.210004621745983e-1, -4.944515323274145e-1, 3.404879937665872e-1, -2.741127028184656e-1,
           5.638259427386472e-1)
_ERFC_R = (-1.047766399936249e+1, 1.297719955372516e+1, -7.495518717768503e+0, 2.921019019210786e+0,
           -1.015265279202700e+0, 4.218463358204948e-1, -2.820767439740514e-1, 5.641895067754075e-1)


def _horner(y, coeffs):
    r = jnp.full(y.shape, coeffs[0], F32)
    for c in coeffs[1:]:
        r = r * y + c
    return r


def _gelu_tail_exact(x):
    t = -x * (2.0 ** -0.5)
    y = jnp.abs(t)
    q = 1.0 / jnp.maximum(y, 1.0)
    q2 = q * q
    tail = jnp.exp(-t * t) * q * jnp.where(y < 2.0, _horner(q2, _ERFC_P), _horner(q2, _ERFC_R))
    tail = jnp.where(t < 0.0, 2.0 - tail, tail)
    return 0.5 * x * jnp.where(y < 1.0, 1.0 - lax.erf(t), tail)


def _tree_sum0(x):
    n = x.shape[0]
    while n > 1 and n % 2 == 0:
        n //= 2
        x = x[:n] + x[n:]
    return jnp.sum(x, axis=0)


def _lam(lam_ref):
    lv = lam_ref[...]
    s1 = jnp.sum(lv[0:1] * lv[1:2], axis=-1, keepdims=True)
    s2 = jnp.sum(lv[2:3] * lv[3:4], axis=-1, keepdims=True)
    return jnp.exp(s1) - jnp.exp(s2) + LAM_INIT


def _proj_kernel(x_ref, w_in_ref, lng_ref, lnb_ref, wsp_ref, bsp_ref,
                 k_ref, v_ref, qb_ref, kb_ref, vb_ref, a_ref, gv_ref,
                 wbf_ref, wcat_ref, *, tiles_per_seq):
    i = pl.program_id(0)
    tm = x_ref.shape[0]
    d_a = lng_ref.shape[1]

    @pl.when(i == 0)
    def _init():
        rows = w_in_ref.shape[0]
        step = 128

        def body(r, c):
            sl = pl.ds(pl.multiple_of(r * step, step), step)
            wbf_ref[sl, :] = w_in_ref[sl, :].astype(BF16)
            return c

        lax.fori_loop(0, rows // step, body, 0)
        row = lax.broadcasted_iota(jnp.int32, (CHUNK, CHUNK), 0)
        col = lax.broadcasted_iota(jnp.int32, (CHUNK, CHUNK), 1)
        for j in range(A_HEADS // 2):
            w0 = jnp.where(col <= row, wsp_ref[2 * j], 0.0)
            w1 = jnp.where(col <= row, wsp_ref[2 * j + 1], 0.0)
            wcat_ref[j] = jnp.concatenate([w0, w1], axis=1).astype(BF16)

    xb = x_ref[...].astype(BF16)

    def seg(s):
        return jnp.dot(xb, wbf_ref[:, s * d_a:(s + 1) * d_a], preferred_element_type=F32)

    k = seg(3)
    k_ref[...] = k
    kb_ref[...] = k.astype(BF16)
    v = seg(4)
    v_ref[...] = v
    vb_ref[...] = v.astype(BF16)
    qb_ref[...] = (seg(2) * (B_HEAD_DIM ** -0.5)).astype(BF16)

    va = _layer_norm(_gelu(seg(1)), lng_ref[...], lnb_ref[...])

    @pl.when(i % tiles_per_seq == tiles_per_seq - 1)
    def _gv():
        gv_ref[0] = va[tm - CHUNK:, :]

    u = _gelu(seg(0))
    vab = va.astype(BF16)
    lane = lax.broadcasted_iota(jnp.int32, (CHUNK, LANES), 1)
    zero = jnp.zeros((CHUNK, LANES), BF16)
    for c in range(tm // CHUNK):
        r0 = c * CHUNK
        for j in range(A_HEADS // 2):
            vp = vab[r0:r0 + CHUNK, j * LANES:(j + 1) * LANES]
            stacked = jnp.concatenate([jnp.where(lane < 64, vp, zero), jnp.where(lane >= 64, vp, zero)], axis=0)
            sg = jnp.dot(wcat_ref[j], stacked, preferred_element_type=F32) + bsp_ref[:, j * LANES:(j + 1) * LANES]
            a_ref[r0:r0 + CHUNK, j * LANES:(j + 1) * LANES] = (
                u[r0:r0 + CHUNK, j * LANES:(j + 1) * LANES] * sg).astype(BF16)


def _proj_prompt(x, w_in, lng, lnb, w_sp, b_sp_full, seq):
    t, d = x.shape
    d_in = w_in.shape[1]
    d_a = lng.shape[1]
    tm = min(ROW_TILE, seq)
    tiles_per_seq = seq // tm
    n_seq = t // seq
    row = lambda w: pl.BlockSpec((tm, w), lambda i: (i, 0))
    out_shape = (
        jax.ShapeDtypeStruct((t, d_a), F32), jax.ShapeDtypeStruct((t, d_a), F32),
        jax.ShapeDtypeStruct((t, d_a), BF16), jax.ShapeDtypeStruct((t, d_a), BF16),
        jax.ShapeDtypeStruct((t, d_a), BF16), jax.ShapeDtypeStruct((t, d_a), BF16),
        jax.ShapeDtypeStruct((n_seq, CHUNK, d_a), F32),
    )
    return pl.pallas_call(
        functools.partial(_proj_kernel, tiles_per_seq=tiles_per_seq),
        grid=(t // tm,),
        in_specs=[row(d), _const_spec((d, d_in)), _const_spec((1, d_a)), _const_spec((1, d_a)),
                  _const_spec((A_HEADS, CHUNK, CHUNK)), _const_spec((CHUNK, d_a))],
        out_specs=(row(d_a), row(d_a), row(d_a), row(d_a), row(d_a), row(d_a),
                   pl.BlockSpec((1, CHUNK, d_a), lambda i: (i // tiles_per_seq, 0, 0))),
        out_shape=out_shape,
        scratch_shapes=[pltpu.VMEM((d, d_in), BF16), pltpu.VMEM((A_HEADS // 2, CHUNK, 2 * CHUNK), BF16)],
        compiler_params=pltpu.CompilerParams(dimension_semantics=("arbitrary",), vmem_limit_bytes=VMEM_LIMIT),
        name="proj_prompt",
    )(x, w_in, lng, lnb, w_sp, b_sp_full)


def _attn_kernel(lam_ref, g_ref, q_ref, k_ref, v_ref, o_ref, acc, m, l):
    i = pl.program_id(2)
    tq = q_ref.shape[0]
    seq = k_ref.shape[0]
    wide = ATTN_TK if seq % ATTN_TK == 0 and ATTN_TK % tq == 0 else tq
    per_wide = wide // tq
    qb = q_ref[...]
    lane = lax.broadcasted_iota(jnp.int32, qb.shape, 1)
    zero = jnp.zeros(qb.shape, BF16)
    q2 = jnp.concatenate([jnp.where(lane < B_HEAD_DIM, qb, zero), jnp.where(lane >= B_HEAD_DIM, qb, zero)], axis=0)

    acc[...] = jnp.zeros(acc.shape, F32)
    m[...] = jnp.full(m.shape, -jnp.inf, F32)
    l[...] = jnp.zeros(l.shape, F32)

    def block(start, width, causal):
        sl = pl.ds(pl.multiple_of(start, tq), width)
        s = lax.dot_general(q2, k_ref[sl, :], _NT, preferred_element_type=F32)
        if causal:
            row = lax.broadcasted_iota(jnp.int32, (2 * tq, width), 0) % tq
            col = lax.broadcasted_iota(jnp.int32, (2 * tq, width), 1)
            s = jnp.where(col <= row, s, -jnp.inf)
        m_old = m[...]
        m_new = jnp.maximum(m_old, jnp.max(s, axis=-1, keepdims=True))
        alpha = jnp.exp(m_old - m_new)
        p = jnp.exp(s - m_new)
        l[...] = alpha * l[...] + jnp.sum(p, axis=-1, keepdims=True)
        acc[...] = alpha * acc[...] + jnp.dot(p.astype(BF16), v_ref[sl, :], preferred_element_type=F32)
        m[...] = m_new

    n_wide = i // per_wide

    def wide_body(j, c):
        block(j * wide, wide, False)
        return c

    lax.fori_loop(0, n_wide, wide_body, 0)
    for r in range(per_wide - 1):
        @pl.when(n_wide * per_wide + r < i)
        def _():
            block((n_wide * per_wide + r) * tq, tq, False)
    block(i * tq, tq, True)

    lam = _lam(lam_ref)
    w = acc[...] / l[...]
    o = w[:tq] - lam * w[tq:]
    ms = jnp.mean(o * o, axis=-1, keepdims=True)
    o_ref[...] = (o * lax.rsqrt(ms + LN_EPS) * g_ref[...] * (1.0 - LAM_INIT)).astype(o_ref.dtype)


def _attn_prompt(lam_rows, subln_g, qb, kb, vb, n_seq, seq):
    t, d_b = qb.shape
    tq = min(ATTN_TQ, seq)
    nq = seq // tq
    return pl.pallas_call(
        _attn_kernel,
        grid=(n_seq, B_HEADS, nq),
        in_specs=[_const_spec((4, B_HEAD_DIM)), _const_spec((1, B_V_DIM)),
                  pl.BlockSpec((tq, LANES), lambda b, h, i: (b * nq + i, h)),
                  pl.BlockSpec((seq, LANES), lambda b, h, i: (b, h)),
                  pl.BlockSpec((seq, LANES), lambda b, h, i: (b, h))],
        out_specs=pl.BlockSpec((tq, LANES), lambda b, h, i: (b * nq + i, h)),
        out_shape=jax.ShapeDtypeStruct((t, d_b), BF16),
        scratch_shapes=[pltpu.VMEM((2 * tq, B_V_DIM), F32), pltpu.VMEM((2 * tq, 1), F32),
                        pltpu.VMEM((2 * tq, 1), F32)],
        compiler_params=pltpu.CompilerParams(dimension_semantics=("arbitrary", "arbitrary", "arbitrary"),
                                             vmem_limit_bytes=VMEM_LIMIT),
        name="attn_prompt",
    )(lam_rows, subln_g, qb, kb, vb)


def _route(logits_t, bias_col):
    e, n = logits_t.shape
    scores = jax.nn.sigmoid(logits_t)
    biased = scores + bias_col
    ninf = jnp.float32(-jnp.inf)
    gs = []
    sub = lax.broadcasted_iota(jnp.int32, (GROUP_SIZE, n), 0)
    for g in range(N_GROUPS):
        blk = biased[g * GROUP_SIZE:(g + 1) * GROUP_SIZE]
        t1 = jnp.max(blk, axis=0, keepdims=True)
        i1 = jnp.min(jnp.where(blk == t1, sub, GROUP_SIZE), axis=0, keepdims=True)
        t2 = jnp.max(jnp.where(sub == i1, ninf, blk), axis=0, keepdims=True)
        gs.append(t1 + t2)
    keep = []
    for g in range(N_GROUPS):
        cnt = jnp.zeros((1, n), jnp.int32)
        for o in range(N_GROUPS):
            if o == g:
                continue
            beats = (gs[o] >= gs[g]) if o < g else (gs[o] > gs[g])
            cnt = cnt + beats.astype(jnp.int32)
        keep.append(cnt < TOPK_GROUPS)
    masked = jnp.concatenate(
        [jnp.where(keep[g], biased[g * GROUP_SIZE:(g + 1) * GROUP_SIZE], ninf) for g in range(N_GROUPS)], axis=0)
    rid = lax.broadcasted_iota(jnp.int32, (e, n), 0)
    idxs, vals, hits = [], [], []
    for _ in range(TOP_K):
        top = jnp.max(masked, axis=0, keepdims=True)
        idx = jnp.min(jnp.where(masked == top, rid, e), axis=0, keepdims=True)
        hit = rid == idx
        vals.append(jnp.sum(jnp.where(hit, scores, 0.0), axis=0, keepdims=True))
        idxs.append(idx)
        hits.append(hit)
        masked = jnp.where(hit, ninf, masked)
    eidx = jnp.concatenate(idxs, axis=0)
    gw = jnp.concatenate(vals, axis=0)
    gw = gw / jnp.sum(gw, axis=0, keepdims=True) * ROUTED_SCALE
    return eidx, gw, hits


def _mix_kernel(a_ref, b_ref, x_ref, wout_ref, g_ref, bb_ref, wrt_ref, rb_ref, base_ref,
                h1_ref, eidx_ref, gw_ref, rank_ref, cnt_ref, wo_s, wr_s, before_s, seen_s):
    i = pl.program_id(0)
    d_a = a_ref.shape[1]
    tm = x_ref.shape[0]

    @pl.when(i == 0)
    def _init():
        wo_s[...] = wout_ref[...].astype(BF16)
        wr_s[...] = wrt_ref[...].astype(BF16)
        r = lax.broadcasted_iota(jnp.int32, (tm, tm), 0)
        c = lax.broadcasted_iota(jnp.int32, (tm, tm), 1)
        before_s[...] = jnp.where(r < c, 1.0, 0.0).astype(BF16)
        seen_s[...] = base_ref[...]

    dot = functools.partial(jnp.dot, preferred_element_type=F32)
    mix = dot(a_ref[...], wo_s[:d_a, :]) + dot(b_ref[...], wo_s[d_a:, :])
    h1 = _layer_norm(DEEPNORM_ALPHA * x_ref[...] + mix, g_ref[...], bb_ref[...])
    h1_ref[...] = h1
    logits_t = lax.dot_general(wr_s[...], h1.astype(BF16), _NT, preferred_element_type=F32)
    eidx, gw, hits = _route(logits_t, rb_ref[...])
    eidx_ref[...] = eidx
    gw_ref[...] = gw
    onehot = jnp.zeros(logits_t.shape, F32)
    for hit in hits:
        onehot = jnp.where(hit, 1.0, onehot)
    earlier = dot(onehot.astype(BF16), before_s[...]) + seen_s[...]
    rank_ref[...] = jnp.concatenate(
        [jnp.sum(jnp.where(hit, earlier, 0.0), axis=0, keepdims=True) for hit in hits], axis=0).astype(jnp.int32)
    seen = seen_s[...] + jnp.sum(onehot, axis=1, keepdims=True)
    seen_s[...] = seen
    cnt_ref[...] = seen


def _mix_route(a, b, x, w_out, g, bb, w_router_t, rbias_col, base_counts):
    t, d = x.shape
    d_a = a.shape[1]
    tm = min(ROW_TILE, t)
    row = lambda w: pl.BlockSpec((tm, w), lambda i: (i, 0))
    colb = pl.BlockSpec((TOP_K, tm), lambda i: (0, i))
    return pl.pallas_call(
        _mix_kernel,
        grid=(t // tm,),
        in_specs=[row(d_a), row(d_a), row(d), _const_spec((d, d)), _const_spec((1, d)), _const_spec((1, d)),
                  _const_spec((N_EXPERTS, d)), _const_spec((N_EXPERTS, 1)), _const_spec((N_EXPERTS, 1))],
        out_specs=(row(d), colb, colb, colb, _const_spec((N_EXPERTS, 1))),
        out_shape=(jax.ShapeDtypeStruct((t, d), F32),
                   jax.ShapeDtypeStruct((TOP_K, t), jnp.int32), jax.ShapeDtypeStruct((TOP_K, t), F32),
                   jax.ShapeDtypeStruct((TOP_K, t), jnp.int32), jax.ShapeDtypeStruct((N_EXPERTS, 1), F32)),
        scratch_shapes=[pltpu.VMEM((d, d), BF16), pltpu.VMEM((N_EXPERTS, d), BF16), pltpu.VMEM((tm, tm), BF16),
                        pltpu.VMEM((N_EXPERTS, 1), F32)],
        compiler_params=pltpu.CompilerParams(dimension_semantics=("arbitrary",), vmem_limit_bytes=VMEM_LIMIT),
        name="mix_route",
    )(a, b, x, w_out, g, bb, w_router_t, rbias_col, base_counts)


def _row_copy(src, src_row, dst, dst_row, sem):
    return pltpu.make_async_copy(src.at[pl.ds(src_row, 1), :], dst.at[pl.ds(dst_row, 1), :], sem)


def _dispatch_rows(pstart_ref, eidx_ref, rank_ref, h1_ref, xs_ref, row_ref, sem):
    tm = h1_ref.shape[0]

    def start_token(j, c):
        for k in range(TOP_K):
            row = pstart_ref[eidx_ref[k, j]] + rank_ref[k, j]
            row_ref[k, j] = row
            _row_copy(h1_ref, j, xs_ref, row, sem).start(priority=k % 2)
        return c

    def wait_token(j, c):
        for k in range(TOP_K):
            _row_copy(h1_ref, 0, xs_ref, 0, sem).wait()
        return c

    lax.fori_loop(0, tm, start_token, 0)
    lax.fori_loop(0, tm, wait_token, 0)


def _dispatch_first_kernel(pstart_ref, cnt_ref, eidx_ref, rank_ref, h1_ref, xs_ref, row_ref, zero_s, sem):
    @pl.when(pl.program_id(0) == 0)
    def _pad():
        zero_s[...] = jnp.zeros(zero_s.shape, F32)

        def per_expert(action):
            def body(e, c):
                lo = pstart_ref[e] + cnt_ref[e]
                hi = pstart_ref[e] + (cnt_ref[e] + MOE_TILE - 1) // MOE_TILE * MOE_TILE

                def one(r, cc):
                    action(_row_copy(zero_s, 0, xs_ref, r, sem))
                    return cc

                lax.fori_loop(lo, hi, one, 0)
                return c
            return body

        lax.fori_loop(0, N_EXPERTS, per_expert(lambda cp: cp.start()), 0)
        lax.fori_loop(0, N_EXPERTS, per_expert(lambda cp: cp.wait()), 0)

    _dispatch_rows(pstart_ref, eidx_ref, rank_ref, h1_ref, xs_ref, row_ref, sem)


def _dispatch_more_kernel(pstart_ref, cnt_ref, eidx_ref, rank_ref, h1_ref, xs_in_ref, xs_ref, row_ref, sem):
    del cnt_ref, xs_in_ref
    _dispatch_rows(pstart_ref, eidx_ref, rank_ref, h1_ref, xs_ref, row_ref, sem)


def _dispatch(pstart, counts, eidx, rank, h1, xs, m_rows):
    t, d = h1.shape
    tm = min(ROW_TILE, t)
    first = xs is None
    smem_blk = pl.BlockSpec((TOP_K, tm), lambda i, *_: (0, i), memory_space=pltpu.SMEM)
    any_spec = pl.BlockSpec(memory_space=pl.ANY)
    in_specs = [smem_blk, smem_blk, pl.BlockSpec((tm, d), lambda i, *_: (i, 0))]
    scratch = [pltpu.SemaphoreType.DMA(())]
    return pl.pallas_call(
        _dispatch_first_kernel if first else _dispatch_more_kernel,
        grid_spec=pltpu.PrefetchScalarGridSpec(
            num_scalar_prefetch=2,
            grid=(t // tm,),
            in_specs=in_specs if first else in_specs + [any_spec],
            out_specs=(any_spec, smem_blk),
            scratch_shapes=[pltpu.VMEM((8, d), F32)] + scratch if first else scratch),
        out_shape=(jax.ShapeDtypeStruct((m_rows, d), F32), jax.ShapeDtypeStruct((TOP_K, t), jnp.int32)),
        input_output_aliases={} if first else {5: 0},
        compiler_params=pltpu.CompilerParams(dimension_semantics=("arbitrary",), vmem_limit_bytes=VMEM_LIMIT),
        name="dispatch_first" if first else "dispatch_more",
    )(*((pstart, counts, eidx, rank, h1) if first else (pstart, counts, eidx, rank, h1, xs)))


def _experts_kernel(te_ref, nu_ref, x_ref, wg_ref, wu_ref, wd_ref, y_ref, wgu_s, wd_s):
    t = pl.program_id(0)
    prev = te_ref[jnp.maximum(t - 1, 0)]
    d_e = wg_ref.shape[2]

    @pl.when((t < nu_ref[0]) & ((t == 0) | (te_ref[t] != prev)))
    def _cast():
        wgu_s[:, :d_e] = wg_ref[0].astype(BF16)
        wgu_s[:, d_e:] = wu_ref[0].astype(BF16)
        wd_s[...] = wd_ref[0].astype(BF16)

    @pl.when(t < nu_ref[0])
    def _compute():
        gu = jnp.dot(x_ref[...].astype(BF16), wgu_s[...], preferred_element_type=F32)
        hid = jax.nn.silu(gu[:, :d_e]) * gu[:, d_e:]
        y_ref[...] = jnp.dot(hid.astype(BF16), wd_s[...], preferred_element_type=F32)


def _experts(tile_expert, n_used, xs, w_g, w_u, w_d):
    m, d = xs.shape
    d_e = w_g.shape[2]
    n_tiles = m // MOE_TILE

    def rows(t, te, nu):
        return (jnp.minimum(t, nu[0] - 1), 0)

    def wmap(t, te, nu):
        return (te[jnp.minimum(t, nu[0] - 1)], 0, 0)

    return pl.pallas_call(
        _experts_kernel,
        grid_spec=pltpu.PrefetchScalarGridSpec(
            num_scalar_prefetch=2,
            grid=(n_tiles,),
            in_specs=[pl.BlockSpec((MOE_TILE, d), rows),
                      pl.BlockSpec((1, d, d_e), wmap), pl.BlockSpec((1, d, d_e), wmap),
                      pl.BlockSpec((1, d_e, d), wmap)],
            out_specs=pl.BlockSpec((MOE_TILE, d), rows),
            scratch_shapes=[pltpu.VMEM((d, 2 * d_e), BF16), pltpu.VMEM((d_e, d), BF16)]),
        out_shape=jax.ShapeDtypeStruct((m, d), F32),
        compiler_params=pltpu.CompilerParams(dimension_semantics=("arbitrary",), vmem_limit_bytes=VMEM_LIMIT),
        name="experts",
    )(tile_expert, n_used, xs, w_g, w_u, w_d)


def _final_kernel(rows_ref, rows_next_ref, h1_ref, gw_ref, p_ref, ys_ref, wsg_ref, wsu_ref, wsd_ref, g_ref, bb_ref,
                  wpg_ref, bpg_ref, wp_ref, y_ref, wgu_s, wd_s, wpg_s, wp_s, ybuf, sems):
    i = pl.program_id(0)
    n = pl.num_programs(0)
    d_s = wsg_ref.shape[1]
    tm = h1_ref.shape[0]

    def gather(idx_ref, slot, action):
        def body(j, c):
            for k in range(TOP_K):
                src_row = idx_ref[k, j] if action == "start" else 0
                cp = pltpu.make_async_copy(ys_ref.at[pl.ds(src_row, 1), :], ybuf.at[slot, k, pl.ds(j, 1), :],
                                           sems.at[slot])
                cp.start(priority=k % 2) if action == "start" else cp.wait()
            return c
        lax.fori_loop(0, tm, body, 0)

    @pl.when(i == 0)
    def _init():
        gather(rows_ref, 0, "start")
        wgu_s[:, :d_s] = wsg_ref[...].astype(BF16)
        wgu_s[:, d_s:] = wsu_ref[...].astype(BF16)
        wd_s[...] = wsd_ref[...].astype(BF16)
        wpg_s[...] = wpg_ref[...].astype(BF16)
        wp_s[...] = wp_ref[...].astype(BF16)

    slot = i % 2

    @pl.when(i + 1 < n)
    def _prefetch():
        gather(rows_next_ref, 1 - slot, "start")

    dot = functools.partial(jnp.dot, preferred_element_type=F32)
    h1 = h1_ref[...]
    gu = dot(h1.astype(BF16), wgu_s[...])
    hid = jax.nn.silu(gu[:, :d_s]) * gu[:, d_s:]
    shared = dot(hid.astype(BF16), wd_s[...])

    gather(rows_ref, slot, "wait")
    gwv = gw_ref[...]
    routed = gwv[:, 0:1] * ybuf[slot, 0]
    for k in range(1, TOP_K):
        routed = routed + gwv[:, k:k + 1] * ybuf[slot, k]

    h2 = _layer_norm(DEEPNORM_ALPHA * h1 + (routed + shared), g_ref[...], bb_ref[...])
    gate = jax.nn.sigmoid(dot(h2.astype(BF16), wpg_s[...]) + bpg_ref[...])
    y_ref[...] = h2 + gate * dot(p_ref[...].astype(BF16), wp_s[...])


def _final(row_of, h1, gw, p, ys, w_sg, w_su, w_sd, g, bb, w_pg, b_pg, w_p):
    t, d = h1.shape
    d_s = w_sg.shape[1]
    d_p = p.shape[1]
    tm = min(ROW_TILE, t)
    n = t // tm
    row = lambda w: pl.BlockSpec((tm, w), lambda i: (i, 0))
    return pl.pallas_call(
        _final_kernel,
        grid=(n,),
        in_specs=[pl.BlockSpec((TOP_K, tm), lambda i: (0, i), memory_space=pltpu.SMEM),
                  pl.BlockSpec((TOP_K, tm), lambda i: (0, jnp.minimum(i + 1, n - 1)), memory_space=pltpu.SMEM),
                  row(d), row(TOP_K), row(d_p), pl.BlockSpec(memory_space=pl.ANY),
                  _const_spec((d, d_s)), _const_spec((d, d_s)), _const_spec((d_s, d)),
                  _const_spec((1, d)), _const_spec((1, d)), _const_spec((d, d)), _const_spec((1, d)),
                  _const_spec((d_p, d))],
        out_specs=row(d),
        out_shape=jax.ShapeDtypeStruct((t, d), F32),
        scratch_shapes=[pltpu.VMEM((d, 2 * d_s), BF16), pltpu.VMEM((d_s, d), BF16), pltpu.VMEM((d, d), BF16),
                        pltpu.VMEM((d_p, d), BF16), pltpu.VMEM((2, TOP_K, tm, d), F32),
                        pltpu.SemaphoreType.DMA((2,))],
        compiler_params=pltpu.CompilerParams(dimension_semantics=("arbitrary",), vmem_limit_bytes=VMEM_LIMIT),
        name="final",
    )(row_of, row_of, h1, gw, p, ys, w_sg, w_su, w_sd, g, bb, w_pg, b_pg, w_p)


def _proj_sample_kernel(x_ref, w_in_ref, lng_ref, lnb_ref, w00_ref, b0_ref,
                        k_ref, v_ref, q_ref, a_ref, gv_ref):
    d_a = lng_ref.shape[1]
    xb = x_ref[...].astype(BF16)

    def seg(s):
        return jnp.dot(xb, w_in_ref[:, s * d_a:(s + 1) * d_a].astype(BF16), preferred_element_type=F32)

    k_ref[...] = seg(3)
    v_ref[...] = seg(4)
    q_ref[...] = seg(2) * (B_HEAD_DIM ** -0.5)
    va = _layer_norm(_gelu_tail_exact(seg(1)), lng_ref[...], lnb_ref[...])
    gv_ref[...] = va
    sg = va * w00_ref[...] + b0_ref[...]
    a_ref[...] = (_gelu_tail_exact(seg(0)) * sg).astype(a_ref.dtype)


def _proj_sample(x, w_in, lng, lnb, w00_row, b0_row):
    n, d = x.shape
    d_in = w_in.shape[1]
    d_a = lng.shape[1]
    full = lambda w: _const_spec((n, w))
    f32 = jax.ShapeDtypeStruct((n, d_a), F32)
    return pl.pallas_call(
        _proj_sample_kernel,
        grid=(1,),
        in_specs=[full(d), _const_spec((d, d_in)), _const_spec((1, d_a)), _const_spec((1, d_a)),
                  _const_spec((1, d_a)), _const_spec((1, d_a))],
        out_specs=(full(d_a),) * 5,
        out_shape=(f32, f32, f32, jax.ShapeDtypeStruct((n, d_a), BF16), f32),
        compiler_params=pltpu.CompilerParams(dimension_semantics=("arbitrary",), vmem_limit_bytes=VMEM_LIMIT),
        name="proj_sample",
    )(x, w_in, lng, lnb, w00_row, b0_row)


def _decode_kernel(pt_ref, lam_ref, g_ref, q_ref, kn_ref, vn_ref, kt_ref, vr_ref, o_ref,
                   kbuf, vbuf, ksem, vsem, s_scr, acc):
    b = pl.program_id(0)
    n_seq = pl.num_programs(0)
    n_maps = 2 * B_HEADS
    d_b = q_ref.shape[2]
    chunk = kbuf.shape[1]
    n_chunks = s_scr.shape[0] // chunk
    page = s_scr.shape[2]

    def chunk_copies(cache_ref, buf, sem, seq, c, slot):
        return [pltpu.make_async_copy(cache_ref.at[pt_ref[seq, c * chunk + j]], buf.at[slot, j], sem.at[slot])
                for j in range(chunk)]

    def start(copies):
        for cp in copies:
            cp.start()

    def wait(copies):
        for cp in copies:
            cp.wait()

    @pl.when(b == 0)
    def _first_chunk():
        start(chunk_copies(kt_ref, kbuf, ksem, 0, 0, 0))

    lane = lax.broadcasted_iota(jnp.int32, (n_maps, d_b), 1)
    mrow = lax.broadcasted_iota(jnp.int32, (n_maps, d_b), 0)
    own_map = jnp.where(mrow < B_HEADS, 2 * mrow, 2 * (mrow - B_HEADS) + 1)
    qrow = jnp.where(lane // B_HEAD_DIM == own_map, jnp.broadcast_to(q_ref[0], (n_maps, d_b)), 0.0).astype(BF16)

    def score_chunk(c, carry):
        slot = c % 2

        @pl.when(c + 1 < n_chunks)
        def _():
            start(chunk_copies(kt_ref, kbuf, ksem, b, c + 1, 1 - slot))

        @pl.when(c + 1 == n_chunks)
        def _():
            start(chunk_copies(vr_ref, vbuf, vsem, b, 0, 0))

        wait(chunk_copies(kt_ref, kbuf, ksem, b, c, slot))
        for j in range(chunk):
            s_scr[c * chunk + j] = jnp.dot(qrow, kbuf[slot, j].astype(BF16), preferred_element_type=F32)
        return carry

    lax.fori_loop(0, n_chunks, score_chunk, 0)

    lam = _lam(lam_ref)

    def diff_weights(w):
        top = w[:B_HEADS] - lam * w[B_HEADS:]
        return jnp.concatenate([top, jnp.zeros_like(top)], axis=0).astype(BF16)

    kn = kn_ref[0].astype(BF16).astype(F32)
    snew = jnp.sum(qrow.astype(F32) * kn, axis=-1, keepdims=True)
    s_all = s_scr[...]
    m = jnp.maximum(jnp.max(jnp.max(s_all, axis=0), axis=-1, keepdims=True), snew)
    e_new = jnp.exp(snew - m)
    z = jnp.sum(_tree_sum0(jnp.exp(s_all - m)), axis=-1, keepdims=True) + e_new
    wd_new = diff_weights(e_new / z).astype(F32)
    vn = vn_ref[0].astype(BF16).astype(F32)
    for h in range(B_HEADS):
        acc[h:h + 1, :] = wd_new[h:h + 1] * vn[:, h * B_V_DIM:(h + 1) * B_V_DIM]

    def value_chunk(c, carry):
        slot = c % 2

        @pl.when(c + 1 < n_chunks)
        def _():
            start(chunk_copies(vr_ref, vbuf, vsem, b, c + 1, 1 - slot))

        @pl.when((c + 1 == n_chunks) & (b + 1 < n_seq))
        def _():
            start(chunk_copies(kt_ref, kbuf, ksem, b + 1, 0, 0))

        wait(chunk_copies(vr_ref, vbuf, vsem, b, c, slot))
        for j in range(chunk):
            wd = diff_weights(jnp.exp(s_scr[c * chunk + j] - m) / z)
            for h in range(B_HEADS):
                vh = vbuf[slot, j, pl.ds(h, page, stride=B_HEADS), :].astype(BF16)
                acc[h:h + 1, :] += jnp.dot(wd, vh, preferred_element_type=F32)[h:h + 1]
        return carry

    lax.fori_loop(0, n_chunks, value_chunk, 0)

    for h in range(B_HEADS):
        oh = acc[h:h + 1, :]
        ms = jnp.mean(oh * oh, axis=-1, keepdims=True)
        o_ref[0, :, h * B_V_DIM:(h + 1) * B_V_DIM] = (
            oh * lax.rsqrt(ms + LN_EPS) * g_ref[...] * (1.0 - LAM_INIT)).astype(o_ref.dtype)


def _decode_attn(page_table, lam_rows, subln_g, q, k_new, v_new, cache_k, cache_v):
    n, d_b = q.shape
    n_pages = page_table.shape[1]
    page = cache_k.shape[2]
    chunk = math.gcd(n_pages, DECODE_CHUNK)
    q3, k3, v3 = (a.reshape(n, 1, d_b) for a in (q, k_new, v_new))
    per_seq = pl.BlockSpec((1, 1, d_b), lambda b, pt: (b, 0, 0))
    any_spec = pl.BlockSpec(memory_space=pl.ANY)
    out = pl.pallas_call(
        _decode_kernel,
        grid_spec=pltpu.PrefetchScalarGridSpec(
            num_scalar_prefetch=1,
            grid=(n,),
            in_specs=[pl.BlockSpec((4, B_HEAD_DIM), lambda b, pt: (0, 0)),
                      pl.BlockSpec((1, B_V_DIM), lambda b, pt: (0, 0)),
                      per_seq, per_seq, per_seq, any_spec, any_spec],
            out_specs=per_seq,
            scratch_shapes=[pltpu.VMEM((2, chunk) + cache_k.shape[1:], F32),
                            pltpu.VMEM((2, chunk) + cache_v.shape[1:], F32),
                            pltpu.SemaphoreType.DMA((2,)), pltpu.SemaphoreType.DMA((2,)),
                            pltpu.VMEM((n_pages, 2 * B_HEADS, page), F32), pltpu.VMEM((B_HEADS, B_V_DIM), F32)]),
        out_shape=jax.ShapeDtypeStruct((n, 1, d_b), BF16),
        compiler_params=pltpu.CompilerParams(dimension_semantics=("arbitrary",), vmem_limit_bytes=VMEM_LIMIT),
        name="decode_attn",
    )(page_table, lam_rows, subln_g, q3, k3, v3, cache_k, cache_v)
    return out.reshape(n, d_b)


def _tile_plan(counts, m_rows):
    padded = (counts + MOE_TILE - 1) // MOE_TILE * MOE_TILE
    pends = jnp.cumsum(padded)
    n_tiles = m_rows // MOE_TILE
    tile_expert = jnp.minimum(
        jnp.searchsorted(pends, jnp.arange(n_tiles, dtype=jnp.int32) * MOE_TILE, side='right'),
        N_EXPERTS - 1).astype(jnp.int32)
    n_used = (pends[-1:] // MOE_TILE).astype(jnp.int32)
    return (pends - padded).astype(jnp.int32), tile_expert, n_used


def kernel(x_prompt, x_sample, cache_k, cache_v, page_table, p_prompt, p_sample, w_in, ln_v_g, ln_v_b, w_spatial,
           b_spatial, lambda_q1, lambda_k1, lambda_q2, lambda_k2, subln_g, w_out, ln1_g, ln1_b, w_router,
           router_bias, w_exp_gate, w_exp_up, w_exp_down, w_sh_gate, w_sh_up, w_sh_down, ln2_g, ln2_b, w_ple,
           w_ple_gate, b_ple_gate):
    assert w_in.shape[0] == DEPTH and x_sample.shape[1] == 1
    n_seq, seq, d = x_prompt.shape
    n_dec = x_sample.shape[0]
    t = n_seq * seq
    d_a = ln_v_g.shape[1]
    d_b = B_HEADS * B_V_DIM
    row1 = lambda a: a.reshape(1, -1)

    lam_rows = jnp.concatenate([lambda_q1, lambda_k1, lambda_q2, lambda_k2], axis=0)
    lng, lnb = row1(ln_v_g[0]), row1(ln_v_b[0])
    b_sp_full = jnp.repeat(b_spatial[0].T, d_a // A_HEADS, axis=1)
    w00_row = row1(jnp.repeat(w_spatial[0, :, 0, 0], d_a // A_HEADS))
    b0_row = b_sp_full[0:1]
    w_router_t = w_router[0].T
    rbias_col = router_bias[0].reshape(N_EXPERTS, 1)

    xp = x_prompt.reshape(t, d)
    k_p, v_p, qb, kb, vb, a_p, gv_p = _proj_prompt(xp, w_in[0], lng, lnb, w_spatial[0], b_sp_full, seq)
    b_p = _attn_prompt(lam_rows, row1(subln_g[0]), qb, kb, vb, n_seq, seq)
    mix = functools.partial(_mix_route, w_out=w_out[0], g=row1(ln1_g[0]), bb=row1(ln1_b[0]),
                            w_router_t=w_router_t, rbias_col=rbias_col)
    h1_p, eidx_p, gw_p, rank_p, counts_p = mix(a_p, b_p, xp, base_counts=jnp.zeros((N_EXPERTS, 1), F32))

    xs_ = x_sample.reshape(n_dec, d)
    k_s, v_s, q_s, a_s, gv_s = _proj_sample(xs_, w_in[0], lng, lnb, w00_row, b0_row)
    n_phys, page = cache_k.shape[1], cache_k.shape[2]
    ck = cache_k[0].transpose(0, 2, 3, 1).reshape(n_phys, 2 * B_HEADS * B_HEAD_DIM, page)
    cv = cache_v[0].reshape(n_phys, page * B_HEADS, B_V_DIM)
    b_s = _decode_attn(page_table, lam_rows, row1(subln_g[0]), q_s, k_s, v_s, ck, cv)
    h1_s, eidx_s, gw_s, rank_s, counts = mix(a_s, b_s, xs_, base_counts=counts_p)

    m_rows = -(-((t + n_dec) * TOP_K + N_EXPERTS * (MOE_TILE - 1)) // MOE_TILE) * MOE_TILE
    counts = counts.reshape(N_EXPERTS).astype(jnp.int32)
    pstart, tile_expert, n_used = _tile_plan(counts, m_rows)
    xs_sorted, rows_p = _dispatch(pstart, counts, eidx_p, rank_p, h1_p, None, m_rows)
    xs_sorted, rows_s = _dispatch(pstart, counts, eidx_s, rank_s, h1_s, xs_sorted, m_rows)
    ys = _experts(tile_expert, n_used, xs_sorted, w_exp_gate[0], w_exp_up[0], w_exp_down[0])

    fin = functools.partial(_final, ys=ys, w_sg=w_sh_gate[0], w_su=w_sh_up[0], w_sd=w_sh_down[0], g=row1(ln2_g[0]),
                            bb=row1(ln2_b[0]), w_pg=w_ple_gate[0], b_pg=row1(b_ple_gate[0]), w_p=w_ple[0])
    y_p = fin(rows_p, h1_p, gw_p.T, p_prompt[0].reshape(t, -1))
    y_s = fin(rows_s, h1_s, gw_s.T, p_sample[0].reshape(n_dec, -1))

    return (y_p.reshape(n_seq, seq, d), y_s.reshape(n_dec, 1, d),
            k_p.reshape(1, n_seq, seq, 2 * B_HEADS, B_HEAD_DIM), v_p.reshape(1, n_seq, seq, B_HEADS, B_V_DIM),
            gv_p.reshape(1, n_seq, CHUNK, d_a),
            k_s.reshape(1, n_dec, 1, 2 * B_HEADS, B_HEAD_DIM), v_s.reshape(1, n_dec, 1, B_HEADS, B_V_DIM),
            gv_s.reshape(1, n_dec, 1, d_a))
```

```python
import functools
import math

import jax
import jax.numpy as jnp
from jax import lax
from jax.experimental import pallas as pl
from jax.experimental.pallas import tpu as pltpu

F32 = jnp.float32
BF16 = jnp.bfloat16

A_HEADS = 8
CHUNK = 128
B_HEADS = 4
B_HEAD_DIM = 64
B_V_DIM = 128
N_EXPERTS = 256
N_GROUPS = 8
GROUP_SIZE = N_EXPERTS // N_GROUPS
TOPK_GROUPS = 4
TOP_K = 8
ROUTED_SCALE = 2.5
LN_EPS = 1e-5
DEPTH = 1
DEEPNORM_ALPHA = (2 * DEPTH) ** 0.25
LAM_INIT = 0.8 - 0.6 * math.exp(-0.3 * 0)

LANES = 128
SUBLANES = 8
ROW_TILE = 256
ATTN_TQ = 256
ATTN_TK = 512
MOE_TILE = 256
DECODE_CHUNK = 8
VMEM_LIMIT = 56 * 1024 * 1024

_NT = (((1,), (1,)), ((), ()))


def _const_spec(shape):
    nd = len(shape)
    return pl.BlockSpec(shape, lambda *_: (0,) * nd)


def _layer_norm(x, g, b):
    mu = jnp.mean(x, axis=-1, keepdims=True)
    xc = x - mu
    var = jnp.mean(xc * xc, axis=-1, keepdims=True)
    return xc * lax.rsqrt(var + LN_EPS) * g + b


def _gelu(x):
    return 0.5 * x * (1.0 + lax.erf(x * (2.0 ** -0.5)))


_ERFC_P = (2.326819970068386e-2, -1.387039388740657e-1, 3.687424674597105e-1, -5.824733027278666e-1,
           6.210004621745983e-1, -4.944515323274145e-1, 3.404879937665872e-1, -2.741127028184656e-1,
           5.638259427386472e-1)
_ERFC_R = (-1.047766399936249e+1, 1.297719955372516e+1, -7.495518717768503e+0, 2.921019019210786e+0,
           -1.015265279202700e+0, 4.218463358204948e-1, -2.820767439740514e-1, 5.641895067754075e-1)


def _horner(y, coeffs):
    r = jnp.full(y.shape, coeffs[0], F32)
    for c in coeffs[1:]:
        r = r * y + c
    return r


def _gelu_tail_exact(x):
    t = -x * (2.0 ** -0.5)
    y = jnp.abs(t)
    q = 1.0 / jnp.maximum(y, 1.0)
    q2 = q * q
    tail = jnp.exp(-t * t) * q * jnp.where(y < 2.0, _horner(q2, _ERFC_P), _horner(q2, _ERFC_R))
    tail = jnp.where(t < 0.0, 2.0 - tail, tail)
    return 0.5 * x * jnp.where(y < 1.0, 1.0 - lax.erf(t), tail)


def _tree_sum0(x):
    n = x.shape[0]
    while n > 1 and n % 2 == 0:
        n //= 2
        x = x[:n] + x[n:]
    return jnp.sum(x, axis=0)


def _lam(lam_ref):
    lv = lam_ref[...]
    s1 = jnp.sum(lv[0:1] * lv[1:2], axis=-1, keepdims=True)
    s2 = jnp.sum(lv[2:3] * lv[3:4], axis=-1, keepdims=True)
    return jnp.exp(s1) - jnp.exp(s2) + LAM_INIT


def _proj_kernel(x_ref, w_in_ref, lng_ref, lnb_ref, wsp_ref, bsp_ref,
                 k_ref, v_ref, qb_ref, kb_ref, vb_ref, a_ref, gv_ref,
                 wbf_ref, wcat_ref, *, tiles_per_seq):
    i = pl.program_id(0)
    tm = x_ref.shape[0]
    d_a = lng_ref.shape[1]

    @pl.when(i == 0)
    def _init():
        rows = w_in_ref.shape[0]
        step = 128

        def body(r, c):
            sl = pl.ds(pl.multiple_of(r * step, step), step)
            wbf_ref[sl, :] = w_in_ref[sl, :].astype(BF16)
            return c

        lax.fori_loop(0, rows // step, body, 0)
        row = lax.broadcasted_iota(jnp.int32, (CHUNK, CHUNK), 0)
        col = lax.broadcasted_iota(jnp.int32, (CHUNK, CHUNK), 1)
        for j in range(A_HEADS // 2):
            w0 = jnp.where(col <= row, wsp_ref[2 * j], 0.0)
            w1 = jnp.where(col <= row, wsp_ref[2 * j + 1], 0.0)
            wcat_ref[j] = jnp.concatenate([w0, w1], axis=1).astype(BF16)

    xb = x_ref[...].astype(BF16)

    def seg(s):
        return jnp.dot(xb, wbf_ref[:, s * d_a:(s + 1) * d_a], preferred_element_type=F32)

    k = seg(3)
    k_ref[...] = k
    kb_ref[...] = k.astype(BF16)
    v = seg(4)
    v_ref[...] = v
    vb_ref[...] = v.astype(BF16)
    qb_ref[...] = (seg(2) * (B_HEAD_DIM ** -0.5)).astype(BF16)

    va = _layer_norm(_gelu(seg(1)), lng_ref[...], lnb_ref[...])

    @pl.when(i % tiles_per_seq == tiles_per_seq - 1)
    def _gv():
        gv_ref[0] = va[tm - CHUNK:, :]

    u = _gelu(seg(0))
    vab = va.astype(BF16)
    lane = lax.broadcasted_iota(jnp.int32, (CHUNK, LANES), 1)
    zero = jnp.zeros((CHUNK, LANES), BF16)
    for c in range(tm // CHUNK):
        r0 = c * CHUNK
        for j in range(A_HEADS // 2):
            vp = vab[r0:r0 + CHUNK, j * LANES:(j + 1) * LANES]
            stacked = jnp.concatenate([jnp.where(lane < 64, vp, zero), jnp.where(lane >= 64, vp, zero)], axis=0)
            sg = jnp.dot(wcat_ref[j], stacked, preferred_element_type=F32) + bsp_ref[:, j * LANES:(j + 1) * LANES]
            a_ref[r0:r0 + CHUNK, j * LANES:(j + 1) * LANES] = (
                u[r0:r0 + CHUNK, j * LANES:(j + 1) * LANES] * sg).astype(BF16)


def _proj_prompt(x, w_in, lng, lnb, w_sp, b_sp_full, seq):
    t, d = x.shape
    d_in = w_in.shape[1]
    d_a = lng.shape[1]
    tm = min(ROW_TILE, seq)
    tiles_per_seq = seq // tm
    n_seq = t // seq
    row = lambda w: pl.BlockSpec((tm, w), lambda i: (i, 0))
    out_shape = (
        jax.ShapeDtypeStruct((t, d_a), F32), jax.ShapeDtypeStruct((t, d_a), F32),
        jax.ShapeDtypeStruct((t, d_a), BF16), jax.ShapeDtypeStruct((t, d_a), BF16),
        jax.ShapeDtypeStruct((t, d_a), BF16), jax.ShapeDtypeStruct((t, d_a), BF16),
        jax.ShapeDtypeStruct((n_seq, CHUNK, d_a), F32),
    )
    return pl.pallas_call(
        functools.partial(_proj_kernel, tiles_per_seq=tiles_per_seq),
        grid=(t // tm,),
        in_specs=[row(d), _const_spec((d, d_in)), _const_spec((1, d_a)), _const_spec((1, d_a)),
                  _const_spec((A_HEADS, CHUNK, CHUNK)), _const_spec((CHUNK, d_a))],
        out_specs=(row(d_a), row(d_a), row(d_a), row(d_a), row(d_a), row(d_a),
                   pl.BlockSpec((1, CHUNK, d_a), lambda i: (i // tiles_per_seq, 0, 0))),
        out_shape=out_shape,
        scratch_shapes=[pltpu.VMEM((d, d_in), BF16), pltpu.VMEM((A_HEADS // 2, CHUNK, 2 * CHUNK), BF16)],
        compiler_params=pltpu.CompilerParams(dimension_semantics=("arbitrary",), vmem_limit_bytes=VMEM_LIMIT),
        name="proj_prompt",
    )(x, w_in, lng, lnb, w_sp, b_sp_full)


def _attn_kernel(lam_ref, g_ref, q_ref, k_ref, v_ref, o_ref, acc, m, l):
    i = pl.program_id(2)
    tq = q_ref.shape[0]
    seq = k_ref.shape[0]
    wide = ATTN_TK if seq % ATTN_TK == 0 and ATTN_TK % tq == 0 else tq
    per_wide = wide // tq
    qb = q_ref[...]
    lane = lax.broadcasted_iota(jnp.int32, qb.shape, 1)
    zero = jnp.zeros(qb.shape, BF16)
    q2 = jnp.concatenate([jnp.where(lane < B_HEAD_DIM, qb, zero), jnp.where(lane >= B_HEAD_DIM, qb, zero)], axis=0)

    acc[...] = jnp.zeros(acc.shape, F32)
    m[...] = jnp.full(m.shape, -jnp.inf, F32)
    l[...] = jnp.zeros(l.shape, F32)

    def block(start, width, causal):
        sl = pl.ds(pl.multiple_of(start, tq), width)
        s = lax.dot_general(q2, k_ref[sl, :], _NT, preferred_element_type=F32)
        if causal:
            row = lax.broadcasted_iota(jnp.int32, (2 * tq, width), 0) % tq
            col = lax.broadcasted_iota(jnp.int32, (2 * tq, width), 1)
            s = jnp.where(col <= row, s, -jnp.inf)
        m_old = m[...]
        m_new = jnp.maximum(m_old, jnp.max(s, axis=-1, keepdims=True))
        alpha = jnp.exp(m_old - m_new)
        p = jnp.exp(s - m_new)
        l[...] = alpha * l[...] + jnp.sum(p, axis=-1, keepdims=True)
        acc[...] = alpha * acc[...] + jnp.dot(p.astype(BF16), v_ref[sl, :], preferred_element_type=F32)
        m[...] = m_new

    n_wide = i // per_wide

    def wide_body(j, c):
        block(j * wide, wide, False)
        return c

    lax.fori_loop(0, n_wide, wide_body, 0)
    for r in range(per_wide - 1):
        @pl.when(n_wide * per_wide + r < i)
        def _():
            block((n_wide * per_wide + r) * tq, tq, False)
    block(i * tq, tq, True)

    lam = _lam(lam_ref)
    w = acc[...] / l[...]
    o = w[:tq] - lam * w[tq:]
    ms = jnp.mean(o * o, axis=-1, keepdims=True)
    o_ref[...] = (o * lax.rsqrt(ms + LN_EPS) * g_ref[...] * (1.0 - LAM_INIT)).astype(o_ref.dtype)


def _attn_prompt(lam_rows, subln_g, qb, kb, vb, n_seq, seq):
    t, d_b = qb.shape
    tq = min(ATTN_TQ, seq)
    nq = seq // tq
    return pl.pallas_call(
        _attn_kernel,
        grid=(n_seq, B_HEADS, nq),
        in_specs=[_const_spec((4, B_HEAD_DIM)), _const_spec((1, B_V_DIM)),
                  pl.BlockSpec((tq, LANES), lambda b, h, i: (b * nq + i, h)),
                  pl.BlockSpec((seq, LANES), lambda b, h, i: (b, h)),
                  pl.BlockSpec((seq, LANES), lambda b, h, i: (b, h))],
        out_specs=pl.BlockSpec((tq, LANES), lambda b, h, i: (b * nq + i, h)),
        out_shape=jax.ShapeDtypeStruct((t, d_b), BF16),
        scratch_shapes=[pltpu.VMEM((2 * tq, B_V_DIM), F32), pltpu.VMEM((2 * tq, 1), F32),
                        pltpu.VMEM((2 * tq, 1), F32)],
        compiler_params=pltpu.CompilerParams(dimension_semantics=("arbitrary", "arbitrary", "arbitrary"),
                                             vmem_limit_bytes=VMEM_LIMIT),
        name="attn_prompt",
    )(lam_rows, subln_g, qb, kb, vb)


def _route(logits_t, bias_col):
    e, n = logits_t.shape
    scores = jax.nn.sigmoid(logits_t)
    biased = scores + bias_col
    ninf = jnp.float32(-jnp.inf)
    gs = []
    sub = lax.broadcasted_iota(jnp.int32, (GROUP_SIZE, n), 0)
    for g in range(N_GROUPS):
        blk = biased[g * GROUP_SIZE:(g + 1) * GROUP_SIZE]
        t1 = jnp.max(blk, axis=0, keepdims=True)
        i1 = jnp.min(jnp.where(blk == t1, sub, GROUP_SIZE), axis=0, keepdims=True)
        t2 = jnp.max(jnp.where(sub == i1, ninf, blk), axis=0, keepdims=True)
        gs.append(t1 + t2)
    keep = []
    for g in range(N_GROUPS):
        cnt = jnp.zeros((1, n), jnp.int32)
        for o in range(N_GROUPS):
            if o == g:
                continue
            beats = (gs[o] >= gs[g]) if o < g else (gs[o] > gs[g])
            cnt = cnt + beats.astype(jnp.int32)
        keep.append(cnt < TOPK_GROUPS)
    masked = jnp.concatenate(
        [jnp.where(keep[g], biased[g * GROUP_SIZE:(g + 1) * GROUP_SIZE], ninf) for g in range(N_GROUPS)], axis=0)
    rid = lax.broadcasted_iota(jnp.int32, (e, n), 0)
    idxs, vals, hits = [], [], []
    for _ in range(TOP_K):
        top = jnp.max(masked, axis=0, keepdims=True)
        idx = jnp.min(jnp.where(masked == top, rid, e), axis=0, keepdims=True)
        hit = rid == idx
        vals.append(jnp.sum(jnp.where(hit, scores, 0.0), axis=0, keepdims=True))
        idxs.append(idx)
        hits.append(hit)
        masked = jnp.where(hit, ninf, masked)
    eidx = jnp.concatenate(idxs, axis=0)
    gw = jnp.concatenate(vals, axis=0)
    gw = gw / jnp.sum(gw, axis=0, keepdims=True) * ROUTED_SCALE
    return eidx, gw, hits


def _mix_kernel(a_ref, b_ref, x_ref, wout_ref, g_ref, bb_ref, wrt_ref, rb_ref, base_ref,
                h1_ref, eidx_ref, gw_ref, rank_ref, cnt_ref, wo_s, wr_s, before_s, seen_s):
    i = pl.program_id(0)
    d_a = a_ref.shape[1]
    tm = x_ref.shape[0]

    @pl.when(i == 0)
    def _init():
        wo_s[...] = wout_ref[...].astype(BF16)
        wr_s[...] = wrt_ref[...].astype(BF16)
        r = lax.broadcasted_iota(jnp.int32, (tm, tm), 0)
        c = lax.broadcasted_iota(jnp.int32, (tm, tm), 1)
        before_s[...] = jnp.where(r < c, 1.0, 0.0).astype(BF16)
        seen_s[...] = base_ref[...]

    dot = functools.partial(jnp.dot, preferred_element_type=F32)
    mix = dot(a_ref[...], wo_s[:d_a, :]) + dot(b_ref[...], wo_s[d_a:, :])
    h1 = _layer_norm(DEEPNORM_ALPHA * x_ref[...] + mix, g_ref[...], bb_ref[...])
    h1_ref[...] = h1
    logits_t = lax.dot_general(wr_s[...], h1.astype(BF16), _NT, preferred_element_type=F32)
    eidx, gw, hits = _route(logits_t, rb_ref[...])
    eidx_ref[...] = eidx
    gw_ref[...] = gw
    onehot = jnp.zeros(logits_t.shape, F32)
    for hit in hits:
        onehot = jnp.where(hit, 1.0, onehot)
    earlier = dot(onehot.astype(BF16), before_s[...]) + seen_s[...]
    rank_ref[...] = jnp.concatenate(
        [jnp.sum(jnp.where(hit, earlier, 0.0), axis=0, keepdims=True) for hit in hits], axis=0).astype(jnp.int32)
    seen = seen_s[...] + jnp.sum(onehot, axis=1, keepdims=True)
    seen_s[...] = seen
    cnt_ref[...] = seen


def _mix_route(a, b, x, w_out, g, bb, w_router_t, rbias_col, base_counts):
    t, d = x.shape
    d_a = a.shape[1]
    tm = min(ROW_TILE, t)
    row = lambda w: pl.BlockSpec((tm, w), lambda i: (i, 0))
    colb = pl.BlockSpec((TOP_K, tm), lambda i: (0, i))
    return pl.pallas_call(
        _mix_kernel,
        grid=(t // tm,),
        in_specs=[row(d_a), row(d_a), row(d), _const_spec((d, d)), _const_spec((1, d)), _const_spec((1, d)),
                  _const_spec((N_EXPERTS, d)), _const_spec((N_EXPERTS, 1)), _const_spec((N_EXPERTS, 1))],
        out_specs=(row(d), colb, colb, colb, _const_spec((N_EXPERTS, 1))),
        out_shape=(jax.ShapeDtypeStruct((t, d), F32),
                   jax.ShapeDtypeStruct((TOP_K, t), jnp.int32), jax.ShapeDtypeStruct((TOP_K, t), F32),
                   jax.ShapeDtypeStruct((TOP_K, t), jnp.int32), jax.ShapeDtypeStruct((N_EXPERTS, 1), F32)),
        scratch_shapes=[pltpu.VMEM((d, d), BF16), pltpu.VMEM((N_EXPERTS, d), BF16), pltpu.VMEM((tm, tm), BF16),
                        pltpu.VMEM((N_EXPERTS, 1), F32)],
        compiler_params=pltpu.CompilerParams(dimension_semantics=("arbitrary",), vmem_limit_bytes=VMEM_LIMIT),
        name="mix_route",
    )(a, b, x, w_out, g, bb, w_router_t, rbias_col, base_counts)


def _to_row_tiles(x, ref):
    n = x.shape[0]
    for c in range(SUBLANES):
        ref[pl.ds(c, n, stride=SUBLANES), :] = x[:, c * LANES:(c + 1) * LANES]


def _from_row_tiles(ref, n):
    return jnp.concatenate([ref[pl.ds(c, n, stride=SUBLANES), :] for c in range(SUBLANES)], axis=1)


def _tile_copy(src, src_row, dst, dst_row, sem):
    def tile(ref, r):
        return ref.at[pl.ds(pl.multiple_of(r * SUBLANES, SUBLANES), SUBLANES), :]
    return pltpu.make_async_copy(tile(src, src_row), tile(dst, dst_row), sem)


def _dispatch_rows(pstart_ref, eidx_ref, rank_ref, h1_ref, xs_ref, row_ref, stage, sem):
    tm = h1_ref.shape[0]
    _to_row_tiles(h1_ref[...], stage)

    def start_token(j, c):
        for k in range(TOP_K):
            row = pstart_ref[eidx_ref[k, j]] + rank_ref[k, j]
            row_ref[k, j] = row
            _tile_copy(stage, j, xs_ref, row, sem).start(priority=k % 2)
        return c

    def wait_token(j, c):
        for k in range(TOP_K):
            _tile_copy(stage, 0, xs_ref, 0, sem).wait()
        return c

    lax.fori_loop(0, tm, start_token, 0)
    lax.fori_loop(0, tm, wait_token, 0)


def _dispatch_first_kernel(pstart_ref, cnt_ref, eidx_ref, rank_ref, h1_ref, xs_ref, row_ref, stage, zero_s, sem):
    @pl.when(pl.program_id(0) == 0)
    def _pad():
        zero_s[...] = jnp.zeros(zero_s.shape, F32)

        def per_expert(action):
            def body(e, c):
                lo = pstart_ref[e] + cnt_ref[e]
                hi = pstart_ref[e] + (cnt_ref[e] + MOE_TILE - 1) // MOE_TILE * MOE_TILE

                def one(r, cc):
                    action(_tile_copy(zero_s, 0, xs_ref, r, sem))
                    return cc

                lax.fori_loop(lo, hi, one, 0)
                return c
            return body

        lax.fori_loop(0, N_EXPERTS, per_expert(lambda cp: cp.start()), 0)
        lax.fori_loop(0, N_EXPERTS, per_expert(lambda cp: cp.wait()), 0)

    _dispatch_rows(pstart_ref, eidx_ref, rank_ref, h1_ref, xs_ref, row_ref, stage, sem)


def _dispatch_more_kernel(pstart_ref, cnt_ref, eidx_ref, rank_ref, h1_ref, xs_in_ref, xs_ref, row_ref, stage, sem):
    del cnt_ref, xs_in_ref
    _dispatch_rows(pstart_ref, eidx_ref, rank_ref, h1_ref, xs_ref, row_ref, stage, sem)


def _dispatch(pstart, counts, eidx, rank, h1, xs, m_rows):
    t, d = h1.shape
    assert d == SUBLANES * LANES
    tm = min(ROW_TILE, t)
    first = xs is None
    smem_blk = pl.BlockSpec((TOP_K, tm), lambda i, *_: (0, i), memory_space=pltpu.SMEM)
    any_spec = pl.BlockSpec(memory_space=pl.ANY)
    in_specs = [smem_blk, smem_blk, pl.BlockSpec((tm, d), lambda i, *_: (i, 0))]
    stage = [pltpu.VMEM((tm * SUBLANES, LANES), F32)]
    scratch = [pltpu.SemaphoreType.DMA(())]
    return pl.pallas_call(
        _dispatch_first_kernel if first else _dispatch_more_kernel,
        grid_spec=pltpu.PrefetchScalarGridSpec(
            num_scalar_prefetch=2,
            grid=(t // tm,),
            in_specs=in_specs if first else in_specs + [any_spec],
            out_specs=(any_spec, smem_blk),
            scratch_shapes=stage + [pltpu.VMEM((SUBLANES, LANES), F32)] + scratch if first else stage + scratch),
        out_shape=(jax.ShapeDtypeStruct((m_rows * SUBLANES, LANES), F32),
                   jax.ShapeDtypeStruct((TOP_K, t), jnp.int32)),
        input_output_aliases={} if first else {5: 0},
        compiler_params=pltpu.CompilerParams(dimension_semantics=("arbitrary",), vmem_limit_bytes=VMEM_LIMIT),
        name="dispatch_first" if first else "dispatch_more",
    )(*((pstart, counts, eidx, rank, h1) if first else (pstart, counts, eidx, rank, h1, xs)))


def _experts_kernel(te_ref, nu_ref, x_ref, wg_ref, wu_ref, wd_ref, y_ref, wgu_s, wd_s):
    t = pl.program_id(0)
    prev = te_ref[jnp.maximum(t - 1, 0)]
    d_e = wg_ref.shape[2]

    @pl.when((t < nu_ref[0]) & ((t == 0) | (te_ref[t] != prev)))
    def _cast():
        wgu_s[:, :d_e] = wg_ref[0].astype(BF16)
        wgu_s[:, d_e:] = wu_ref[0].astype(BF16)
        wd_s[...] = wd_ref[0].astype(BF16)

    @pl.when(t < nu_ref[0])
    def _compute():
        x = _from_row_tiles(x_ref, MOE_TILE).astype(BF16)
        gu = jnp.dot(x, wgu_s[...], preferred_element_type=F32)
        hid = jax.nn.silu(gu[:, :d_e]) * gu[:, d_e:]
        _to_row_tiles(jnp.dot(hid.astype(BF16), wd_s[...], preferred_element_type=F32), y_ref)


def _experts(tile_expert, n_used, xs, w_g, w_u, w_d):
    d, d_e = w_g.shape[1], w_g.shape[2]
    n_tiles = xs.shape[0] // (MOE_TILE * SUBLANES)

    def rows(t, te, nu):
        return (jnp.minimum(t, nu[0] - 1), 0)

    def wmap(t, te, nu):
        return (te[jnp.minimum(t, nu[0] - 1)], 0, 0)

    return pl.pallas_call(
        _experts_kernel,
        grid_spec=pltpu.PrefetchScalarGridSpec(
            num_scalar_prefetch=2,
            grid=(n_tiles,),
            in_specs=[pl.BlockSpec((MOE_TILE * SUBLANES, LANES), rows),
                      pl.BlockSpec((1, d, d_e), wmap), pl.BlockSpec((1, d, d_e), wmap),
                      pl.BlockSpec((1, d_e, d), wmap)],
            out_specs=pl.BlockSpec((MOE_TILE * SUBLANES, LANES), rows),
            scratch_shapes=[pltpu.VMEM((d, 2 * d_e), BF16), pltpu.VMEM((d_e, d), BF16)]),
        out_shape=jax.ShapeDtypeStruct(xs.shape, F32),
        compiler_params=pltpu.CompilerParams(dimension_semantics=("arbitrary",), vmem_limit_bytes=VMEM_LIMIT),
        name="experts",
    )(tile_expert, n_used, xs, w_g, w_u, w_d)


def _final_kernel(rows_ref, rows_next_ref, h1_ref, gw_ref, p_ref, ys_ref, wsg_ref, wsu_ref, wsd_ref, g_ref, bb_ref,
                  wpg_ref, bpg_ref, wp_ref, y_ref, wgu_s, wd_s, wpg_s, wp_s, ybuf, sems):
    i = pl.program_id(0)
    n = pl.num_programs(0)
    d_s = wsg_ref.shape[1]
    tm = h1_ref.shape[0]

    def gather(idx_ref, slot, action):
        def body(j, c):
            for k in range(TOP_K):
                src_row = idx_ref[k, j] if action == "start" else 0
                cp = _tile_copy(ys_ref, src_row, ybuf.at[slot, k], j, sems.at[slot])
                cp.start(priority=k % 2) if action == "start" else cp.wait()
            return c
        lax.fori_loop(0, tm, body, 0)

    @pl.when(i == 0)
    def _init():
        gather(rows_ref, 0, "start")
        wgu_s[:, :d_s] = wsg_ref[...].astype(BF16)
        wgu_s[:, d_s:] = wsu_ref[...].astype(BF16)
        wd_s[...] = wsd_ref[...].astype(BF16)
        wpg_s[...] = wpg_ref[...].astype(BF16)
        wp_s[...] = wp_ref[...].astype(BF16)

    slot = i % 2

    @pl.when(i + 1 < n)
    def _prefetch():
        gather(rows_next_ref, 1 - slot, "start")

    dot = functools.partial(jnp.dot, preferred_element_type=F32)
    h1 = h1_ref[...]
    gu = dot(h1.astype(BF16), wgu_s[...])
    hid = jax.nn.silu(gu[:, :d_s]) * gu[:, d_s:]
    shared = dot(hid.astype(BF16), wd_s[...])

    gather(rows_ref, slot, "wait")
    gwv = gw_ref[...]
    routed = gwv[:, 0:1] * _from_row_tiles(ybuf.at[slot, 0], tm)
    for k in range(1, TOP_K):
        routed = routed + gwv[:, k:k + 1] * _from_row_tiles(ybuf.at[slot, k], tm)

    h2 = _layer_norm(DEEPNORM_ALPHA * h1 + (routed + shared), g_ref[...], bb_ref[...])
    gate = jax.nn.sigmoid(dot(h2.astype(BF16), wpg_s[...]) + bpg_ref[...])
    y_ref[...] = h2 + gate * dot(p_ref[...].astype(BF16), wp_s[...])


def _final(row_of, h1, gw, p, ys, w_sg, w_su, w_sd, g, bb, w_pg, b_pg, w_p):
    t, d = h1.shape
    d_s = w_sg.shape[1]
    d_p = p.shape[1]
    tm = min(ROW_TILE, t)
    n = t // tm
    row = lambda w: pl.BlockSpec((tm, w), lambda i: (i, 0))
    return pl.pallas_call(
        _final_kernel,
        grid=(n,),
        in_specs=[pl.BlockSpec((TOP_K, tm), lambda i: (0, i), memory_space=pltpu.SMEM),
                  pl.BlockSpec((TOP_K, tm), lambda i: (0, jnp.minimum(i + 1, n - 1)), memory_space=pltpu.SMEM),
                  row(d), row(TOP_K), row(d_p), pl.BlockSpec(memory_space=pl.ANY),
                  _const_spec((d, d_s)), _const_spec((d, d_s)), _const_spec((d_s, d)),
                  _const_spec((1, d)), _const_spec((1, d)), _const_spec((d, d)), _const_spec((1, d)),
                  _const_spec((d_p, d))],
        out_specs=row(d),
        out_shape=jax.ShapeDtypeStruct((t, d), F32),
        scratch_shapes=[pltpu.VMEM((d, 2 * d_s), BF16), pltpu.VMEM((d_s, d), BF16), pltpu.VMEM((d, d), BF16),
                        pltpu.VMEM((d_p, d), BF16), pltpu.VMEM((2, TOP_K, tm * SUBLANES, LANES), F32),
                        pltpu.SemaphoreType.DMA((2,))],
        compiler_params=pltpu.CompilerParams(dimension_semantics=("arbitrary",), vmem_limit_bytes=VMEM_LIMIT),
        name="final",
    )(row_of, row_of, h1, gw, p, ys, w_sg, w_su, w_sd, g, bb, w_pg, b_pg, w_p)


def _proj_sample_kernel(x_ref, w_in_ref, lng_ref, lnb_ref, w00_ref, b0_ref,
                        k_ref, v_ref, q_ref, a_ref, gv_ref):
    d_a = lng_ref.shape[1]
    xb = x_ref[...].astype(BF16)

    def seg(s):
        return jnp.dot(xb, w_in_ref[:, s * d_a:(s + 1) * d_a].astype(BF16), preferred_element_type=F32)

    k_ref[...] = seg(3)
    v_ref[...] = seg(4)
    q_ref[...] = seg(2) * (B_HEAD_DIM ** -0.5)
    va = _layer_norm(_gelu_tail_exact(seg(1)), lng_ref[...], lnb_ref[...])
    gv_ref[...] = va
    sg = va * w00_ref[...] + b0_ref[...]
    a_ref[...] = (_gelu_tail_exact(seg(0)) * sg).astype(a_ref.dtype)


def _proj_sample(x, w_in, lng, lnb, w00_row, b0_row):
    n, d = x.shape
    d_in = w_in.shape[1]
    d_a = lng.shape[1]
    full = lambda w: _const_spec((n, w))
    f32 = jax.ShapeDtypeStruct((n, d_a), F32)
    return pl.pallas_call(
        _proj_sample_kernel,
        grid=(1,),
        in_specs=[full(d), _const_spec((d, d_in)), _const_spec((1, d_a)), _const_spec((1, d_a)),
                  _const_spec((1, d_a)), _const_spec((1, d_a))],
        out_specs=(full(d_a),) * 5,
        out_shape=(f32, f32, f32, jax.ShapeDtypeStruct((n, d_a), BF16), f32),
        compiler_params=pltpu.CompilerParams(dimension_semantics=("arbitrary",), vmem_limit_bytes=VMEM_LIMIT),
        name="proj_sample",
    )(x, w_in, lng, lnb, w00_row, b0_row)


def _decode_kernel(pt_ref, lam_ref, g_ref, q_ref, kn_ref, vn_ref, kt_ref, vr_ref, o_ref,
                   kbuf, vbuf, ksem, vsem, s_scr, acc):
    b = pl.program_id(0)
    n_seq = pl.num_programs(0)
    n_maps = 2 * B_HEADS
    d_b = q_ref.shape[2]
    chunk = kbuf.shape[1]
    n_chunks = s_scr.shape[0] // chunk
    page = s_scr.shape[2]

    def chunk_copies(cache_ref, buf, sem, seq, c, slot):
        return [pltpu.make_async_copy(cache_ref.at[pt_ref[seq, c * chunk + j]], buf.at[slot, j], sem.at[slot])
                for j in range(chunk)]

    def start(copies):
        for cp in copies:
            cp.start()

    def wait(copies):
        for cp in copies:
            cp.wait()

    @pl.when(b == 0)
    def _first_chunk():
        start(chunk_copies(kt_ref, kbuf, ksem, 0, 0, 0))

    lane = lax.broadcasted_iota(jnp.int32, (n_maps, d_b), 1)
    mrow = lax.broadcasted_iota(jnp.int32, (n_maps, d_b), 0)
    own_map = jnp.where(mrow < B_HEADS, 2 * mrow, 2 * (mrow - B_HEADS) + 1)
    qrow = jnp.where(lane // B_HEAD_DIM == own_map, jnp.broadcast_to(q_ref[0], (n_maps, d_b)), 0.0).astype(BF16)

    def score_chunk(c, carry):
        slot = c % 2

        @pl.when(c + 1 < n_chunks)
        def _():
            start(chunk_copies(kt_ref, kbuf, ksem, b, c + 1, 1 - slot))

        @pl.when(c + 1 == n_chunks)
        def _():
            start(chunk_copies(vr_ref, vbuf, vsem, b, 0, 0))

        wait(chunk_copies(kt_ref, kbuf, ksem, b, c, slot))
        for j in range(chunk):
            s_scr[c * chunk + j] = jnp.dot(qrow, kbuf[slot, j].astype(BF16), preferred_element_type=F32)
        return carry

    lax.fori_loop(0, n_chunks, score_chunk, 0)

    lam = _lam(lam_ref)

    def diff_weights(w):
        top = w[:B_HEADS] - lam * w[B_HEADS:]
        return jnp.concatenate([top, jnp.zeros_like(top)], axis=0).astype(BF16)

    kn = kn_ref[0].astype(BF16).astype(F32)
    snew = jnp.sum(qrow.astype(F32) * kn, axis=-1, keepdims=True)
    s_all = s_scr[...]
    m = jnp.maximum(jnp.max(jnp.max(s_all, axis=0), axis=-1, keepdims=True), snew)
    e_new = jnp.exp(snew - m)
    z = jnp.sum(_tree_sum0(jnp.exp(s_all - m)), axis=-1, keepdims=True) + e_new
    wd_new = diff_weights(e_new / z).astype(F32)
    vn = vn_ref[0].astype(BF16).astype(F32)
    for h in range(B_HEADS):
        acc[h:h + 1, :] = wd_new[h:h + 1] * vn[:, h * B_V_DIM:(h + 1) * B_V_DIM]

    def value_chunk(c, carry):
        slot = c % 2

        @pl.when(c + 1 < n_chunks)
        def _():
            start(chunk_copies(vr_ref, vbuf, vsem, b, c + 1, 1 - slot))

        @pl.when((c + 1 == n_chunks) & (b + 1 < n_seq))
        def _():
            start(chunk_copies(kt_ref, kbuf, ksem, b + 1, 0, 0))

        wait(chunk_copies(vr_ref, vbuf, vsem, b, c, slot))
        for j in range(chunk):
            wd = diff_weights(jnp.exp(s_scr[c * chunk + j] - m) / z)
            for h in range(B_HEADS):
                vh = vbuf[slot, j, pl.ds(h, page, stride=B_HEADS), :].astype(BF16)
                acc[h:h + 1, :] += jnp.dot(wd, vh, preferred_element_type=F32)[h:h + 1]
        return carry

    lax.fori_loop(0, n_chunks, value_chunk, 0)

    for h in range(B_HEADS):
        oh = acc[h:h + 1, :]
        ms = jnp.mean(oh * oh, axis=-1, keepdims=True)
        o_ref[0, :, h * B_V_DIM:(h + 1) * B_V_DIM] = (
            oh * lax.rsqrt(ms + LN_EPS) * g_ref[...] * (1.0 - LAM_INIT)).astype(o_ref.dtype)


def _decode_attn(page_table, lam_rows, subln_g, q, k_new, v_new, cache_k, cache_v):
    n, d_b = q.shape
    n_pages = page_table.shape[1]
    page = cache_k.shape[2]
    chunk = math.gcd(n_pages, DECODE_CHUNK)
    q3, k3, v3 = (a.reshape(n, 1, d_b) for a in (q, k_new, v_new))
    per_seq = pl.BlockSpec((1, 1, d_b), lambda b, pt: (b, 0, 0))
    any_spec = pl.BlockSpec(memory_space=pl.ANY)
    out = pl.pallas_call(
        _decode_kernel,
        grid_spec=pltpu.PrefetchScalarGridSpec(
            num_scalar_prefetch=1,
            grid=(n,),
            in_specs=[pl.BlockSpec((4, B_HEAD_DIM), lambda b, pt: (0, 0)),
                      pl.BlockSpec((1, B_V_DIM), lambda b, pt: (0, 0)),
                      per_seq, per_seq, per_seq, any_spec, any_spec],
            out_specs=per_seq,
            scratch_shapes=[pltpu.VMEM((2, chunk) + cache_k.shape[1:], F32),
                            pltpu.VMEM((2, chunk) + cache_v.shape[1:], F32),
                            pltpu.SemaphoreType.DMA((2,)), pltpu.SemaphoreType.DMA((2,)),
                            pltpu.VMEM((n_pages, 2 * B_HEADS, page), F32), pltpu.VMEM((B_HEADS, B_V_DIM), F32)]),
        out_shape=jax.ShapeDtypeStruct((n, 1, d_b), BF16),
        compiler_params=pltpu.CompilerParams(dimension_semantics=("arbitrary",), vmem_limit_bytes=VMEM_LIMIT),
        name="decode_attn",
    )(page_table, lam_rows, subln_g, q3, k3, v3, cache_k, cache_v)
    return out.reshape(n, d_b)


def _tile_plan(counts, m_rows):
    padded = (counts + MOE_TILE - 1) // MOE_TILE * MOE_TILE
    pends = jnp.cumsum(padded)
    n_tiles = m_rows // MOE_TILE
    tile_expert = jnp.minimum(
        jnp.searchsorted(pends, jnp.arange(n_tiles, dtype=jnp.int32) * MOE_TILE, side='right'),
        N_EXPERTS - 1).astype(jnp.int32)
    n_used = (pends[-1:] // MOE_TILE).astype(jnp.int32)
    return (pends - padded).astype(jnp.int32), tile_expert, n_used


def kernel(x_prompt, x_sample, cache_k, cache_v, page_table, p_prompt, p_sample, w_in, ln_v_g, ln_v_b, w_spatial,
           b_spatial, lambda_q1, lambda_k1, lambda_q2, lambda_k2, subln_g, w_out, ln1_g, ln1_b, w_router,
           router_bias, w_exp_gate, w_exp_up, w_exp_down, w_sh_gate, w_sh_up, w_sh_down, ln2_g, ln2_b, w_ple,
           w_ple_gate, b_ple_gate):
    assert w_in.shape[0] == DEPTH and x_sample.shape[1] == 1
    n_seq, seq, d = x_prompt.shape
    n_dec = x_sample.shape[0]
    t = n_seq * seq
    d_a = ln_v_g.shape[1]
    d_b = B_HEADS * B_V_DIM
    row1 = lambda a: a.reshape(1, -1)

    lam_rows = jnp.concatenate([lambda_q1, lambda_k1, lambda_q2, lambda_k2], axis=0)
    lng, lnb = row1(ln_v_g[0]), row1(ln_v_b[0])
    b_sp_full = jnp.repeat(b_spatial[0].T, d_a // A_HEADS, axis=1)
    w00_row = row1(jnp.repeat(w_spatial[0, :, 0, 0], d_a // A_HEADS))
    b0_row = b_sp_full[0:1]
    w_router_t = w_router[0].T
    rbias_col = router_bias[0].reshape(N_EXPERTS, 1)

    xp = x_prompt.reshape(t, d)
    k_p, v_p, qb, kb, vb, a_p, gv_p = _proj_prompt(xp, w_in[0], lng, lnb, w_spatial[0], b_sp_full, seq)
    b_p = _attn_prompt(lam_rows, row1(subln_g[0]), qb, kb, vb, n_seq, seq)
    mix = functools.partial(_mix_route, w_out=w_out[0], g=row1(ln1_g[0]), bb=row1(ln1_b[0]),
                            w_router_t=w_router_t, rbias_col=rbias_col)
    h1_p, eidx_p, gw_p, rank_p, counts_p = mix(a_p, b_p, xp, base_counts=jnp.zeros((N_EXPERTS, 1), F32))

    xs_ = x_sample.reshape(n_dec, d)
    k_s, v_s, q_s, a_s, gv_s = _proj_sample(xs_, w_in[0], lng, lnb, w00_row, b0_row)
    n_phys, page = cache_k.shape[1], cache_k.shape[2]
    ck = cache_k[0].transpose(0, 2, 3, 1).reshape(n_phys, 2 * B_HEADS * B_HEAD_DIM, page)
    cv = cache_v[0].reshape(n_phys, page * B_HEADS, B_V_DIM)
    b_s = _decode_attn(page_table, lam_rows, row1(subln_g[0]), q_s, k_s, v_s, ck, cv)
    h1_s, eidx_s, gw_s, rank_s, counts = mix(a_s, b_s, xs_, base_counts=counts_p)

    m_rows = -(-((t + n_dec) * TOP_K + N_EXPERTS * (MOE_TILE - 1)) // MOE_TILE) * MOE_TILE
    counts = counts.reshape(N_EXPERTS).astype(jnp.int32)
    pstart, tile_expert, n_used = _tile_plan(counts, m_rows)
    xs_sorted, rows_p = _dispatch(pstart, counts, eidx_p, rank_p, h1_p, None, m_rows)
    xs_sorted, rows_s = _dispatch(pstart, counts, eidx_s, rank_s, h1_s, xs_sorted, m_rows)
    ys = _experts(tile_expert, n_used, xs_sorted, w_exp_gate[0], w_exp_up[0], w_exp_down[0])

    fin = functools.partial(_final, ys=ys, w_sg=w_sh_gate[0], w_su=w_sh_up[0], w_sd=w_sh_down[0], g=row1(ln2_g[0]),
                            bb=row1(ln2_b[0]), w_pg=w_ple_gate[0], b_pg=row1(b_ple_gate[0]), w_p=w_ple[0])
    y_p = fin(rows_p, h1_p, gw_p.T, p_prompt[0].reshape(t, -1))
    y_s = fin(rows_s, h1_s, gw_s.T, p_sample[0].reshape(n_dec, -1))

    return (y_p.reshape(n_seq, seq, d), y_s.reshape(n_dec, 1, d),
            k_p.reshape(1, n_seq, seq, 2 * B_HEADS, B_HEAD_DIM), v_p.reshape(1, n_seq, seq, B_HEADS, B_V_DIM),
            gv_p.reshape(1, n_seq, CHUNK, d_a),
            k_s.reshape(1, n_dec, 1, 2 * B_HEADS, B_HEAD_DIM), v_s.reshape(1, n_dec, 1, B_HEADS, B_V_DIM),
            gv_s.reshape(1, n_dec, 1, d_a))
```

```python
import functools
import math

import jax
import jax.numpy as jnp
from jax import lax
from jax.experimental import pallas as pl
from jax.experimental.pallas import tpu as pltpu

F32 = jnp.float32
BF16 = jnp.bfloat16

A_HEADS = 8
CHUNK = 128
B_HEADS = 4
B_HEAD_DIM = 64
B_V_DIM = 128
N_EXPERTS = 256
N_GROUPS = 8
GROUP_SIZE = N_EXPERTS // N_GROUPS
TOPK_GROUPS = 4
TOP_K = 8
ROUTED_SCALE = 2.5
LN_EPS = 1e-5
DEPTH = 1
DEEPNORM_ALPHA = (2 * DEPTH) ** 0.25
LAM_INIT = 0.8 - 0.6 * math.exp(-0.3 * 0)

LANES = 128
SUBLANES = 8
ROW_TILE = 256
ATTN_TQ = 256
ATTN_TK = 512
MOE_TILE = 256
DECODE_CHUNK = 16
VMEM_LIMIT = 56 * 1024 * 1024

_NT = (((1,), (1,)), ((), ()))


def _const_spec(shape):
    nd = len(shape)
    return pl.BlockSpec(shape, lambda *_: (0,) * nd)


def _layer_norm(x, g, b):
    mu = jnp.mean(x, axis=-1, keepdims=True)
    xc = x - mu
    var = jnp.mean(xc * xc, axis=-1, keepdims=True)
    return xc * lax.rsqrt(var + LN_EPS) * g + b


def _gelu(x):
    return 0.5 * x * (1.0 + lax.erf(x * (2.0 ** -0.5)))


_ERFC_P = (2.326819970068386e-2, -1.387039388740657e-1, 3.687424674597105e-1, -5.824733027278666e-1,
           6.210004621745983e-1, -4.944515323274145e-1, 3.404879937665872e-1, -2.741127028184656e-1,
           5.638259427386472e-1)
_ERFC_R = (-1.047766399936249e+1, 1.297719955372516e+1, -7.495518717768503e+0, 2.921019019210786e+0,
           -1.015265279202700e+0, 4.218463358204948e-1, -2.820767439740514e-1, 5.641895067754075e-1)


def _horner(y, coeffs):
    r = jnp.full(y.shape, coeffs[0], F32)
    for c in coeffs[1:]:
        r = r * y + c
    return r


def _gelu_tail_exact(x):
    t = -x * (2.0 ** -0.5)
    y = jnp.abs(t)
    q = 1.0 / jnp.maximum(y, 1.0)
    q2 = q * q
    tail = jnp.exp(-t * t) * q * jnp.where(y < 2.0, _horner(q2, _ERFC_P), _horner(q2, _ERFC_R))
    tail = jnp.where(t < 0.0, 2.0 - tail, tail)
    return 0.5 * x * jnp.where(y < 1.0, 1.0 - lax.erf(t), tail)


def _tree_sum0(x):
    n = x.shape[0]
    while n > 1 and n % 2 == 0:
        n //= 2
        x = x[:n] + x[n:]
    return jnp.sum(x, axis=0)


def _lam(lam_ref):
    lv = lam_ref[...]
    s1 = jnp.sum(lv[0:1] * lv[1:2], axis=-1, keepdims=True)
    s2 = jnp.sum(lv[2:3] * lv[3:4], axis=-1, keepdims=True)
    return jnp.exp(s1) - jnp.exp(s2) + LAM_INIT


def _proj_kernel(x_ref, w_in_ref, lng_ref, lnb_ref, wsp_ref, bsp_ref,
                 k_ref, v_ref, qb_ref, kb_ref, vb_ref, a_ref, gv_ref,
                 wbf_ref, wcat_ref, *, tiles_per_seq):
    i = pl.program_id(0)
    tm = x_ref.shape[0]
    d_a = lng_ref.shape[1]

    @pl.when(i == 0)
    def _init():
        rows = w_in_ref.shape[0]
        step = 128

        def body(r, c):
            sl = pl.ds(pl.multiple_of(r * step, step), step)
            wbf_ref[sl, :] = w_in_ref[sl, :].astype(BF16)
            return c

        lax.fori_loop(0, rows // step, body, 0)
        row = lax.broadcasted_iota(jnp.int32, (CHUNK, CHUNK), 0)
        col = lax.broadcasted_iota(jnp.int32, (CHUNK, CHUNK), 1)
        for j in range(A_HEADS // 2):
            w0 = jnp.where(col <= row, wsp_ref[2 * j], 0.0)
            w1 = jnp.where(col <= row, wsp_ref[2 * j + 1], 0.0)
            wcat_ref[j] = jnp.concatenate([w0, w1], axis=1).astype(BF16)

    xb = x_ref[...].astype(BF16)

    def seg(s):
        return jnp.dot(xb, wbf_ref[:, s * d_a:(s + 1) * d_a], preferred_element_type=F32)

    k = seg(3)
    k_ref[...] = k
    kb_ref[...] = k.astype(BF16)
    v = seg(4)
    v_ref[...] = v
    vb_ref[...] = v.astype(BF16)
    qb_ref[...] = (seg(2) * (B_HEAD_DIM ** -0.5)).astype(BF16)

    va = _layer_norm(_gelu(seg(1)), lng_ref[...], lnb_ref[...])

    @pl.when(i % tiles_per_seq == tiles_per_seq - 1)
    def _gv():
        gv_ref[0] = va[tm - CHUNK:, :]

    u = _gelu(seg(0))
    vab = va.astype(BF16)
    lane = lax.broadcasted_iota(jnp.int32, (CHUNK, LANES), 1)
    zero = jnp.zeros((CHUNK, LANES), BF16)
    for c in range(tm // CHUNK):
        r0 = c * CHUNK
        for j in range(A_HEADS // 2):
            vp = vab[r0:r0 + CHUNK, j * LANES:(j + 1) * LANES]
            stacked = jnp.concatenate([jnp.where(lane < 64, vp, zero), jnp.where(lane >= 64, vp, zero)], axis=0)
            sg = jnp.dot(wcat_ref[j], stacked, preferred_element_type=F32) + bsp_ref[:, j * LANES:(j + 1) * LANES]
            a_ref[r0:r0 + CHUNK, j * LANES:(j + 1) * LANES] = (
                u[r0:r0 + CHUNK, j * LANES:(j + 1) * LANES] * sg).astype(BF16)


def _proj_prompt(x, w_in, lng, lnb, w_sp, b_sp_full, seq):
    t, d = x.shape
    d_in = w_in.shape[1]
    d_a = lng.shape[1]
    tm = min(ROW_TILE, seq)
    tiles_per_seq = seq // tm
    n_seq = t // seq
    row = lambda w: pl.BlockSpec((tm, w), lambda i: (i, 0))
    out_shape = (
        jax.ShapeDtypeStruct((t, d_a), F32), jax.ShapeDtypeStruct((t, d_a), F32),
        jax.ShapeDtypeStruct((t, d_a), BF16), jax.ShapeDtypeStruct((t, d_a), BF16),
        jax.ShapeDtypeStruct((t, d_a), BF16), jax.ShapeDtypeStruct((t, d_a), BF16),
        jax.ShapeDtypeStruct((n_seq, CHUNK, d_a), F32),
    )
    return pl.pallas_call(
        functools.partial(_proj_kernel, tiles_per_seq=tiles_per_seq),
        grid=(t // tm,),
        in_specs=[row(d), _const_spec((d, d_in)), _const_spec((1, d_a)), _const_spec((1, d_a)),
                  _const_spec((A_HEADS, CHUNK, CHUNK)), _const_spec((CHUNK, d_a))],
        out_specs=(row(d_a), row(d_a), row(d_a), row(d_a), row(d_a), row(d_a),
                   pl.BlockSpec((1, CHUNK, d_a), lambda i: (i // tiles_per_seq, 0, 0))),
        out_shape=out_shape,
        scratch_shapes=[pltpu.VMEM((d, d_in), BF16), pltpu.VMEM((A_HEADS // 2, CHUNK, 2 * CHUNK), BF16)],
        compiler_params=pltpu.CompilerParams(dimension_semantics=("arbitrary",), vmem_limit_bytes=VMEM_LIMIT),
        name="proj_prompt",
    )(x, w_in, lng, lnb, w_sp, b_sp_full)


def _attn_kernel(lam_ref, g_ref, q_ref, k_ref, v_ref, o_ref, acc, m, l):
    i = pl.program_id(2)
    tq = q_ref.shape[0]
    seq = k_ref.shape[0]
    wide = ATTN_TK if seq % ATTN_TK == 0 and ATTN_TK % tq == 0 else tq
    per_wide = wide // tq
    qb = q_ref[...]
    lane = lax.broadcasted_iota(jnp.int32, qb.shape, 1)
    zero = jnp.zeros(qb.shape, BF16)
    q2 = jnp.concatenate([jnp.where(lane < B_HEAD_DIM, qb, zero), jnp.where(lane >= B_HEAD_DIM, qb, zero)], axis=0)

    acc[...] = jnp.zeros(acc.shape, F32)
    m[...] = jnp.full(m.shape, -jnp.inf, F32)
    l[...] = jnp.zeros(l.shape, F32)

    def block(start, width, causal):
        sl = pl.ds(pl.multiple_of(start, tq), width)
        s = lax.dot_general(q2, k_ref[sl, :], _NT, preferred_element_type=F32)
        if causal:
            row = lax.broadcasted_iota(jnp.int32, (2 * tq, width), 0) % tq
            col = lax.broadcasted_iota(jnp.int32, (2 * tq, width), 1)
            s = jnp.where(col <= row, s, -jnp.inf)
        m_old = m[...]
        m_new = jnp.maximum(m_old, jnp.max(s, axis=-1, keepdims=True))
        alpha = jnp.exp(m_old - m_new)
        p = jnp.exp(s - m_new)
        l[...] = alpha * l[...] + jnp.sum(p, axis=-1, keepdims=True)
        acc[...] = alpha * acc[...] + jnp.dot(p.astype(BF16), v_ref[sl, :], preferred_element_type=F32)
        m[...] = m_new

    n_wide = i // per_wide

    def wide_body(j, c):
        block(j * wide, wide, False)
        return c

    lax.fori_loop(0, n_wide, wide_body, 0)
    for r in range(per_wide - 1):
        @pl.when(n_wide * per_wide + r < i)
        def _():
            block((n_wide * per_wide + r) * tq, tq, False)
    block(i * tq, tq, True)

    lam = _lam(lam_ref)
    w = acc[...] / l[...]
    o = w[:tq] - lam * w[tq:]
    ms = jnp.mean(o * o, axis=-1, keepdims=True)
    o_ref[...] = (o * lax.rsqrt(ms + LN_EPS) * g_ref[...] * (1.0 - LAM_INIT)).astype(o_ref.dtype)


def _attn_prompt(lam_rows, subln_g, qb, kb, vb, n_seq, seq):
    t, d_b = qb.shape
    tq = min(ATTN_TQ, seq)
    nq = seq // tq
    return pl.pallas_call(
        _attn_kernel,
        grid=(n_seq, B_HEADS, nq),
        in_specs=[_const_spec((4, B_HEAD_DIM)), _const_spec((1, B_V_DIM)),
                  pl.BlockSpec((tq, LANES), lambda b, h, i: (b * nq + i, h)),
                  pl.BlockSpec((seq, LANES), lambda b, h, i: (b, h)),
                  pl.BlockSpec((seq, LANES), lambda b, h, i: (b, h))],
        out_specs=pl.BlockSpec((tq, LANES), lambda b, h, i: (b * nq + i, h)),
        out_shape=jax.ShapeDtypeStruct((t, d_b), BF16),
        scratch_shapes=[pltpu.VMEM((2 * tq, B_V_DIM), F32), pltpu.VMEM((2 * tq, 1), F32),
                        pltpu.VMEM((2 * tq, 1), F32)],
        compiler_params=pltpu.CompilerParams(dimension_semantics=("arbitrary", "arbitrary", "arbitrary"),
                                             vmem_limit_bytes=VMEM_LIMIT),
        name="attn_prompt",
    )(lam_rows, subln_g, qb, kb, vb)


def _route(logits_t, bias_col):
    e, n = logits_t.shape
    scores = jax.nn.sigmoid(logits_t)
    biased = scores + bias_col
    ninf = jnp.float32(-jnp.inf)
    gs = []
    sub = lax.broadcasted_iota(jnp.int32, (GROUP_SIZE, n), 0)
    for g in range(N_GROUPS):
        blk = biased[g * GROUP_SIZE:(g + 1) * GROUP_SIZE]
        t1 = jnp.max(blk, axis=0, keepdims=True)
        i1 = jnp.min(jnp.where(blk == t1, sub, GROUP_SIZE), axis=0, keepdims=True)
        t2 = jnp.max(jnp.where(sub == i1, ninf, blk), axis=0, keepdims=True)
        gs.append(t1 + t2)
    keep = []
    for g in range(N_GROUPS):
        cnt = jnp.zeros((1, n), jnp.int32)
        for o in range(N_GROUPS):
            if o == g:
                continue
            beats = (gs[o] >= gs[g]) if o < g else (gs[o] > gs[g])
            cnt = cnt + beats.astype(jnp.int32)
        keep.append(cnt < TOPK_GROUPS)
    masked = jnp.concatenate(
        [jnp.where(keep[g], biased[g * GROUP_SIZE:(g + 1) * GROUP_SIZE], ninf) for g in range(N_GROUPS)], axis=0)
    rid = lax.broadcasted_iota(jnp.int32, (e, n), 0)
    idxs, vals, hits = [], [], []
    for _ in range(TOP_K):
        top = jnp.max(masked, axis=0, keepdims=True)
        idx = jnp.min(jnp.where(masked == top, rid, e), axis=0, keepdims=True)
        hit = rid == idx
        vals.append(jnp.sum(jnp.where(hit, scores, 0.0), axis=0, keepdims=True))
        idxs.append(idx)
        hits.append(hit)
        masked = jnp.where(hit, ninf, masked)
    eidx = jnp.concatenate(idxs, axis=0)
    gw = jnp.concatenate(vals, axis=0)
    gw = gw / jnp.sum(gw, axis=0, keepdims=True) * ROUTED_SCALE
    return eidx, gw, hits


def _mix_kernel(a_ref, b_ref, x_ref, wout_ref, g_ref, bb_ref, wrt_ref, rb_ref, base_ref,
                h1_ref, eidx_ref, gw_ref, rank_ref, cnt_ref, wo_s, wr_s, before_s, seen_s):
    i = pl.program_id(0)
    d_a = a_ref.shape[1]
    tm = x_ref.shape[0]

    @pl.when(i == 0)
    def _init():
        wo_s[...] = wout_ref[...].astype(BF16)
        wr_s[...] = wrt_ref[...].astype(BF16)
        r = lax.broadcasted_iota(jnp.int32, (tm, tm), 0)
        c = lax.broadcasted_iota(jnp.int32, (tm, tm), 1)
        before_s[...] = jnp.where(r < c, 1.0, 0.0).astype(BF16)
        seen_s[...] = base_ref[...]

    dot = functools.partial(jnp.dot, preferred_element_type=F32)
    mix = dot(a_ref[...], wo_s[:d_a, :]) + dot(b_ref[...], wo_s[d_a:, :])
    h1 = _layer_norm(DEEPNORM_ALPHA * x_ref[...] + mix, g_ref[...], bb_ref[...])
    h1_ref[...] = h1
    logits_t = lax.dot_general(wr_s[...], h1.astype(BF16), _NT, preferred_element_type=F32)
    eidx, gw, hits = _route(logits_t, rb_ref[...])
    eidx_ref[...] = eidx
    gw_ref[...] = gw
    onehot = jnp.zeros(logits_t.shape, F32)
    for hit in hits:
        onehot = jnp.where(hit, 1.0, onehot)
    earlier = dot(onehot.astype(BF16), before_s[...]) + seen_s[...]
    rank_ref[...] = jnp.concatenate(
        [jnp.sum(jnp.where(hit, earlier, 0.0), axis=0, keepdims=True) for hit in hits], axis=0).astype(jnp.int32)
    seen = seen_s[...] + jnp.sum(onehot, axis=1, keepdims=True)
    seen_s[...] = seen
    cnt_ref[...] = seen


def _mix_route(a, b, x, w_out, g, bb, w_router_t, rbias_col, base_counts):
    t, d = x.shape
    d_a = a.shape[1]
    tm = min(ROW_TILE, t)
    row = lambda w: pl.BlockSpec((tm, w), lambda i: (i, 0))
    colb = pl.BlockSpec((TOP_K, tm), lambda i: (0, i))
    return pl.pallas_call(
        _mix_kernel,
        grid=(t // tm,),
        in_specs=[row(d_a), row(d_a), row(d), _const_spec((d, d)), _const_spec((1, d)), _const_spec((1, d)),
                  _const_spec((N_EXPERTS, d)), _const_spec((N_EXPERTS, 1)), _const_spec((N_EXPERTS, 1))],
        out_specs=(row(d), colb, colb, colb, _const_spec((N_EXPERTS, 1))),
        out_shape=(jax.ShapeDtypeStruct((t, d), F32),
                   jax.ShapeDtypeStruct((TOP_K, t), jnp.int32), jax.ShapeDtypeStruct((TOP_K, t), F32),
                   jax.ShapeDtypeStruct((TOP_K, t), jnp.int32), jax.ShapeDtypeStruct((N_EXPERTS, 1), F32)),
        scratch_shapes=[pltpu.VMEM((d, d), BF16), pltpu.VMEM((N_EXPERTS, d), BF16), pltpu.VMEM((tm, tm), BF16),
                        pltpu.VMEM((N_EXPERTS, 1), F32)],
        compiler_params=pltpu.CompilerParams(dimension_semantics=("arbitrary",), vmem_limit_bytes=VMEM_LIMIT),
        name="mix_route",
    )(a, b, x, w_out, g, bb, w_router_t, rbias_col, base_counts)


def _to_row_tiles(x, ref):
    n = x.shape[0]
    for c in range(SUBLANES):
        ref[pl.ds(c, n, stride=SUBLANES), :] = x[:, c * LANES:(c + 1) * LANES]


def _from_row_tiles(ref, n):
    return jnp.concatenate([ref[pl.ds(c, n, stride=SUBLANES), :] for c in range(SUBLANES)], axis=1)


def _tile_copy(src, src_row, dst, dst_row, sem):
    def tile(ref, r):
        return ref.at[pl.ds(pl.multiple_of(r * SUBLANES, SUBLANES), SUBLANES), :]
    return pltpu.make_async_copy(tile(src, src_row), tile(dst, dst_row), sem)


def _dispatch_rows(pstart_ref, eidx_ref, rank_ref, h1_ref, xs_ref, row_ref, stage, sems):
    i = pl.program_id(0)
    tm = h1_ref.shape[0]
    slot = i % 2
    _to_row_tiles(h1_ref[...], stage.at[slot])

    def start_token(j, c):
        for k in range(TOP_K):
            row = pstart_ref[eidx_ref[k, j]] + rank_ref[k, j]
            row_ref[k, j] = row
            _tile_copy(stage.at[slot], j, xs_ref, row, sems.at[slot]).start(priority=k % 2)
        return c

    def wait_all(s):
        def wait_token(j, c):
            for k in range(TOP_K):
                _tile_copy(stage.at[s], 0, xs_ref, 0, sems.at[s]).wait()
            return c
        lax.fori_loop(0, tm, wait_token, 0)

    lax.fori_loop(0, tm, start_token, 0)

    @pl.when(i > 0)
    def _():
        wait_all(1 - slot)

    @pl.when(i == pl.num_programs(0) - 1)
    def _():
        wait_all(slot)


def _dispatch_first_kernel(pstart_ref, cnt_ref, eidx_ref, rank_ref, h1_ref, xs_ref, row_ref, stage, zero_s, sems,
                           zsem):
    @pl.when(pl.program_id(0) == 0)
    def _pad():
        zero_s[...] = jnp.zeros(zero_s.shape, F32)

        def per_expert(action):
            def body(e, c):
                lo = pstart_ref[e] + cnt_ref[e]
                hi = pstart_ref[e] + (cnt_ref[e] + MOE_TILE - 1) // MOE_TILE * MOE_TILE

                def one(r, cc):
                    action(_tile_copy(zero_s, 0, xs_ref, r, zsem))
                    return cc

                lax.fori_loop(lo, hi, one, 0)
                return c
            return body

        lax.fori_loop(0, N_EXPERTS, per_expert(lambda cp: cp.start()), 0)
        lax.fori_loop(0, N_EXPERTS, per_expert(lambda cp: cp.wait()), 0)

    _dispatch_rows(pstart_ref, eidx_ref, rank_ref, h1_ref, xs_ref, row_ref, stage, sems)


def _dispatch_more_kernel(pstart_ref, cnt_ref, eidx_ref, rank_ref, h1_ref, xs_in_ref, xs_ref, row_ref, stage, sems):
    del cnt_ref, xs_in_ref
    _dispatch_rows(pstart_ref, eidx_ref, rank_ref, h1_ref, xs_ref, row_ref, stage, sems)


def _dispatch(pstart, counts, eidx, rank, h1, xs, m_rows):
    t, d = h1.shape
    assert d == SUBLANES * LANES
    tm = min(ROW_TILE, t)
    first = xs is None
    smem_blk = pl.BlockSpec((TOP_K, tm), lambda i, *_: (0, i), memory_space=pltpu.SMEM)
    any_spec = pl.BlockSpec(memory_space=pl.ANY)
    in_specs = [smem_blk, smem_blk, pl.BlockSpec((tm, d), lambda i, *_: (i, 0))]
    stage = [pltpu.VMEM((2, tm * SUBLANES, LANES), F32)]
    scratch = [pltpu.SemaphoreType.DMA((2,))]
    return pl.pallas_call(
        _dispatch_first_kernel if first else _dispatch_more_kernel,
        grid_spec=pltpu.PrefetchScalarGridSpec(
            num_scalar_prefetch=2,
            grid=(t // tm,),
            in_specs=in_specs if first else in_specs + [any_spec],
            out_specs=(any_spec, smem_blk),
            scratch_shapes=(stage + [pltpu.VMEM((SUBLANES, LANES), F32)] + scratch + [pltpu.SemaphoreType.DMA(())]
                            if first else stage + scratch)),
        out_shape=(jax.ShapeDtypeStruct((m_rows * SUBLANES, LANES), F32),
                   jax.ShapeDtypeStruct((TOP_K, t), jnp.int32)),
        input_output_aliases={} if first else {5: 0},
        compiler_params=pltpu.CompilerParams(dimension_semantics=("arbitrary",), vmem_limit_bytes=VMEM_LIMIT),
        name="dispatch_first" if first else "dispatch_more",
    )(*((pstart, counts, eidx, rank, h1) if first else (pstart, counts, eidx, rank, h1, xs)))


def _experts_kernel(te_ref, nu_ref, x_ref, wg_ref, wu_ref, wd_ref, y_ref, wgu_s, wd_s):
    t = pl.program_id(0)
    prev = te_ref[jnp.maximum(t - 1, 0)]
    d_e = wg_ref.shape[2]

    @pl.when((t < nu_ref[0]) & ((t == 0) | (te_ref[t] != prev)))
    def _cast():
        wgu_s[:, :d_e] = wg_ref[0].astype(BF16)
        wgu_s[:, d_e:] = wu_ref[0].astype(BF16)
        wd_s[...] = wd_ref[0].astype(BF16)

    @pl.when(t < nu_ref[0])
    def _compute():
        x = _from_row_tiles(x_ref, MOE_TILE).astype(BF16)
        gu = jnp.dot(x, wgu_s[...], preferred_element_type=F32)
        hid = jax.nn.silu(gu[:, :d_e]) * gu[:, d_e:]
        _to_row_tiles(jnp.dot(hid.astype(BF16), wd_s[...], preferred_element_type=F32), y_ref)


def _experts(tile_expert, n_used, xs, w_g, w_u, w_d):
    d, d_e = w_g.shape[1], w_g.shape[2]
    n_tiles = xs.shape[0] // (MOE_TILE * SUBLANES)

    def rows(t, te, nu):
        return (jnp.minimum(t, nu[0] - 1), 0)

    def wmap(t, te, nu):
        return (te[jnp.minimum(t, nu[0] - 1)], 0, 0)

    return pl.pallas_call(
        _experts_kernel,
        grid_spec=pltpu.PrefetchScalarGridSpec(
            num_scalar_prefetch=2,
            grid=(n_tiles,),
            in_specs=[pl.BlockSpec((MOE_TILE * SUBLANES, LANES), rows),
                      pl.BlockSpec((1, d, d_e), wmap), pl.BlockSpec((1, d, d_e), wmap),
                      pl.BlockSpec((1, d_e, d), wmap)],
            out_specs=pl.BlockSpec((MOE_TILE * SUBLANES, LANES), rows),
            scratch_shapes=[pltpu.VMEM((d, 2 * d_e), BF16), pltpu.VMEM((d_e, d), BF16)]),
        out_shape=jax.ShapeDtypeStruct(xs.shape, F32),
        compiler_params=pltpu.CompilerParams(dimension_semantics=("arbitrary",), vmem_limit_bytes=VMEM_LIMIT),
        name="experts",
    )(tile_expert, n_used, xs, w_g, w_u, w_d)


def _final_kernel(rows_ref, rows_next_ref, h1_ref, gw_ref, p_ref, ys_ref, wsg_ref, wsu_ref, wsd_ref, g_ref, bb_ref,
                  wpg_ref, bpg_ref, wp_ref, y_ref, wgu_s, wd_s, wpg_s, wp_s, ybuf, sems):
    i = pl.program_id(0)
    n = pl.num_programs(0)
    d_s = wsg_ref.shape[1]
    tm = h1_ref.shape[0]

    def gather(idx_ref, slot, action):
        def body(j, c):
            for k in range(TOP_K):
                src_row = idx_ref[k, j] if action == "start" else 0
                cp = _tile_copy(ys_ref, src_row, ybuf.at[slot, k], j, sems.at[slot])
                cp.start(priority=k % 2) if action == "start" else cp.wait()
            return c
        lax.fori_loop(0, tm, body, 0)

    @pl.when(i == 0)
    def _init():
        gather(rows_ref, 0, "start")
        wgu_s[:, :d_s] = wsg_ref[...].astype(BF16)
        wgu_s[:, d_s:] = wsu_ref[...].astype(BF16)
        wd_s[...] = wsd_ref[...].astype(BF16)
        wpg_s[...] = wpg_ref[...].astype(BF16)
        wp_s[...] = wp_ref[...].astype(BF16)

    slot = i % 2

    @pl.when(i + 1 < n)
    def _prefetch():
        gather(rows_next_ref, 1 - slot, "start")

    dot = functools.partial(jnp.dot, preferred_element_type=F32)
    h1 = h1_ref[...]
    gu = dot(h1.astype(BF16), wgu_s[...])
    hid = jax.nn.silu(gu[:, :d_s]) * gu[:, d_s:]
    shared = dot(hid.astype(BF16), wd_s[...])

    gather(rows_ref, slot, "wait")
    gwv = gw_ref[...]
    routed = gwv[:, 0:1] * _from_row_tiles(ybuf.at[slot, 0], tm)
    for k in range(1, TOP_K):
        routed = routed + gwv[:, k:k + 1] * _from_row_tiles(ybuf.at[slot, k], tm)

    h2 = _layer_norm(DEEPNORM_ALPHA * h1 + (routed + shared), g_ref[...], bb_ref[...])
    gate = jax.nn.sigmoid(dot(h2.astype(BF16), wpg_s[...]) + bpg_ref[...])
    y_ref[...] = h2 + gate * dot(p_ref[...].astype(BF16), wp_s[...])


def _final(row_of, h1, gw, p, ys, w_sg, w_su, w_sd, g, bb, w_pg, b_pg, w_p):
    t, d = h1.shape
    d_s = w_sg.shape[1]
    d_p = p.shape[1]
    tm = min(ROW_TILE, t)
    n = t // tm
    row = lambda w: pl.BlockSpec((tm, w), lambda i: (i, 0))
    return pl.pallas_call(
        _final_kernel,
        grid=(n,),
        in_specs=[pl.BlockSpec((TOP_K, tm), lambda i: (0, i), memory_space=pltpu.SMEM),
                  pl.BlockSpec((TOP_K, tm), lambda i: (0, jnp.minimum(i + 1, n - 1)), memory_space=pltpu.SMEM),
                  row(d), row(TOP_K), row(d_p), pl.BlockSpec(memory_space=pl.ANY),
                  _const_spec((d, d_s)), _const_spec((d, d_s)), _const_spec((d_s, d)),
                  _const_spec((1, d)), _const_spec((1, d)), _const_spec((d, d)), _const_spec((1, d)),
                  _const_spec((d_p, d))],
        out_specs=row(d),
        out_shape=jax.ShapeDtypeStruct((t, d), F32),
        scratch_shapes=[pltpu.VMEM((d, 2 * d_s), BF16), pltpu.VMEM((d_s, d), BF16), pltpu.VMEM((d, d), BF16),
                        pltpu.VMEM((d_p, d), BF16), pltpu.VMEM((2, TOP_K, tm * SUBLANES, LANES), F32),
                        pltpu.SemaphoreType.DMA((2,))],
        compiler_params=pltpu.CompilerParams(dimension_semantics=("arbitrary",), vmem_limit_bytes=VMEM_LIMIT),
        name="final",
    )(row_of, row_of, h1, gw, p, ys, w_sg, w_su, w_sd, g, bb, w_pg, b_pg, w_p)


def _proj_sample_kernel(x_ref, w_in_ref, lng_ref, lnb_ref, w00_ref, b0_ref,
                        k_ref, v_ref, q_ref, a_ref, gv_ref):
    d_a = lng_ref.shape[1]
    xb = x_ref[...].astype(BF16)

    def seg(s):
        return jnp.dot(xb, w_in_ref[:, s * d_a:(s + 1) * d_a].astype(BF16), preferred_element_type=F32)

    k_ref[...] = seg(3)
    v_ref[...] = seg(4)
    q_ref[...] = seg(2) * (B_HEAD_DIM ** -0.5)
    va = _layer_norm(_gelu_tail_exact(seg(1)), lng_ref[...], lnb_ref[...])
    gv_ref[...] = va
    sg = va * w00_ref[...] + b0_ref[...]
    a_ref[...] = (_gelu_tail_exact(seg(0)) * sg).astype(a_ref.dtype)


def _proj_sample(x, w_in, lng, lnb, w00_row, b0_row):
    n, d = x.shape
    d_in = w_in.shape[1]
    d_a = lng.shape[1]
    full = lambda w: _const_spec((n, w))
    f32 = jax.ShapeDtypeStruct((n, d_a), F32)
    return pl.pallas_call(
        _proj_sample_kernel,
        grid=(1,),
        in_specs=[full(d), _const_spec((d, d_in)), _const_spec((1, d_a)), _const_spec((1, d_a)),
                  _const_spec((1, d_a)), _const_spec((1, d_a))],
        out_specs=(full(d_a),) * 5,
        out_shape=(f32, f32, f32, jax.ShapeDtypeStruct((n, d_a), BF16), f32),
        compiler_params=pltpu.CompilerParams(dimension_semantics=("arbitrary",), vmem_limit_bytes=VMEM_LIMIT),
        name="proj_sample",
    )(x, w_in, lng, lnb, w00_row, b0_row)


def _decode_kernel(pt_ref, lam_ref, g_ref, q_ref, kn_ref, vn_ref, kt_ref, vr_ref, o_ref,
                   kbuf, vbuf, ksem, vsem, s_scr, acc):
    b = pl.program_id(0)
    n_seq = pl.num_programs(0)
    n_maps = 2 * B_HEADS
    d_b = q_ref.shape[2]
    chunk = kbuf.shape[1]
    n_chunks = s_scr.shape[0] // chunk
    page = s_scr.shape[2]

    def chunk_copies(cache_ref, buf, sem, seq, c, slot):
        return [pltpu.make_async_copy(cache_ref.at[pt_ref[seq, c * chunk + j]], buf.at[slot, j], sem.at[slot])
                for j in range(chunk)]

    def start(copies):
        for cp in copies:
            cp.start()

    def wait(copies):
        for cp in copies:
            cp.wait()

    @pl.when(b == 0)
    def _first_chunk():
        start(chunk_copies(kt_ref, kbuf, ksem, 0, 0, 0))

    lane = lax.broadcasted_iota(jnp.int32, (n_maps, d_b), 1)
    mrow = lax.broadcasted_iota(jnp.int32, (n_maps, d_b), 0)
    own_map = jnp.where(mrow < B_HEADS, 2 * mrow, 2 * (mrow - B_HEADS) + 1)
    qrow = jnp.where(lane // B_HEAD_DIM == own_map, jnp.broadcast_to(q_ref[0], (n_maps, d_b)), 0.0).astype(BF16)

    def score_chunk(c, carry):
        slot = c % 2

        @pl.when(c + 1 < n_chunks)
        def _():
            start(chunk_copies(kt_ref, kbuf, ksem, b, c + 1, 1 - slot))

        @pl.when(c + 1 == n_chunks)
        def _():
            start(chunk_copies(vr_ref, vbuf, vsem, b, 0, 0))

        wait(chunk_copies(kt_ref, kbuf, ksem, b, c, slot))
        for j in range(chunk):
            s_scr[c * chunk + j] = jnp.dot(qrow, kbuf[slot, j].astype(BF16), preferred_element_type=F32)
        return carry

    lax.fori_loop(0, n_chunks, score_chunk, 0)

    lam = _lam(lam_ref)

    def diff_weights(w):
        top = w[:B_HEADS] - lam * w[B_HEADS:]
        return jnp.concatenate([top, jnp.zeros_like(top)], axis=0).astype(BF16)

    kn = kn_ref[0].astype(BF16).astype(F32)
    snew = jnp.sum(qrow.astype(F32) * kn, axis=-1, keepdims=True)
    s_all = s_scr[...]
    m = jnp.maximum(jnp.max(jnp.max(s_all, axis=0), axis=-1, keepdims=True), snew)
    e_new = jnp.exp(snew - m)
    z = jnp.sum(_tree_sum0(jnp.exp(s_all - m)), axis=-1, keepdims=True) + e_new
    wd_new = diff_weights(e_new / z).astype(F32)
    vn = vn_ref[0].astype(BF16).astype(F32)
    for h in range(B_HEADS):
        acc[h:h + 1, :] = wd_new[h:h + 1] * vn[:, h * B_V_DIM:(h + 1) * B_V_DIM]

    def value_chunk(c, carry):
        slot = c % 2

        @pl.when(c + 1 < n_chunks)
        def _():
            start(chunk_copies(vr_ref, vbuf, vsem, b, c + 1, 1 - slot))

        @pl.when((c + 1 == n_chunks) & (b + 1 < n_seq))
        def _():
            start(chunk_copies(kt_ref, kbuf, ksem, b + 1, 0, 0))

        wait(chunk_copies(vr_ref, vbuf, vsem, b, c, slot))
        for j in range(chunk):
            wd = diff_weights(jnp.exp(s_scr[c * chunk + j] - m) / z)
            for h in range(B_HEADS):
                vh = vbuf[slot, j, pl.ds(h, page, stride=B_HEADS), :].astype(BF16)
                acc[h:h + 1, :] += jnp.dot(wd, vh, preferred_element_type=F32)[h:h + 1]
        return carry

    lax.fori_loop(0, n_chunks, value_chunk, 0)

    for h in range(B_HEADS):
        oh = acc[h:h + 1, :]
        ms = jnp.mean(oh * oh, axis=-1, keepdims=True)
        o_ref[0, :, h * B_V_DIM:(h + 1) * B_V_DIM] = (
            oh * lax.rsqrt(ms + LN_EPS) * g_ref[...] * (1.0 - LAM_INIT)).astype(o_ref.dtype)


def _decode_attn(page_table, lam_rows, subln_g, q, k_new, v_new, cache_k, cache_v):
    n, d_b = q.shape
    n_pages = page_table.shape[1]
    page = cache_k.shape[2]
    chunk = math.gcd(n_pages, DECODE_CHUNK)
    q3, k3, v3 = (a.reshape(n, 1, d_b) for a in (q, k_new, v_new))
    per_seq = pl.BlockSpec((1, 1, d_b), lambda b, pt: (b, 0, 0))
    any_spec = pl.BlockSpec(memory_space=pl.ANY)
    out = pl.pallas_call(
        _decode_kernel,
        grid_spec=pltpu.PrefetchScalarGridSpec(
            num_scalar_prefetch=1,
            grid=(n,),
            in_specs=[pl.BlockSpec((4, B_HEAD_DIM), lambda b, pt: (0, 0)),
                      pl.BlockSpec((1, B_V_DIM), lambda b, pt: (0, 0)),
                      per_seq, per_seq, per_seq, any_spec, any_spec],
            out_specs=per_seq,
            scratch_shapes=[pltpu.VMEM((2, chunk) + cache_k.shape[1:], F32),
                            pltpu.VMEM((2, chunk) + cache_v.shape[1:], F32),
                            pltpu.SemaphoreType.DMA((2,)), pltpu.SemaphoreType.DMA((2,)),
                            pltpu.VMEM((n_pages, 2 * B_HEADS, page), F32), pltpu.VMEM((B_HEADS, B_V_DIM), F32)]),
        out_shape=jax.ShapeDtypeStruct((n, 1, d_b), BF16),
        compiler_params=pltpu.CompilerParams(dimension_semantics=("arbitrary",), vmem_limit_bytes=VMEM_LIMIT),
        name="decode_attn",
    )(page_table, lam_rows, subln_g, q3, k3, v3, cache_k, cache_v)
    return out.reshape(n, d_b)


def _tile_plan(counts, m_rows):
    padded = (counts + MOE_TILE - 1) // MOE_TILE * MOE_TILE
    pends = jnp.cumsum(padded)
    n_tiles = m_rows // MOE_TILE
    tile_expert = jnp.minimum(
        jnp.searchsorted(pends, jnp.arange(n_tiles, dtype=jnp.int32) * MOE_TILE, side='right'),
        N_EXPERTS - 1).astype(jnp.int32)
    n_used = (pends[-1:] // MOE_TILE).astype(jnp.int32)
    return (pends - padded).astype(jnp.int32), tile_expert, n_used


def kernel(x_prompt, x_sample, cache_k, cache_v, page_table, p_prompt, p_sample, w_in, ln_v_g, ln_v_b, w_spatial,
           b_spatial, lambda_q1, lambda_k1, lambda_q2, lambda_k2, subln_g, w_out, ln1_g, ln1_b, w_router,
           router_bias, w_exp_gate, w_exp_up, w_exp_down, w_sh_gate, w_sh_up, w_sh_down, ln2_g, ln2_b, w_ple,
           w_ple_gate, b_ple_gate):
    assert w_in.shape[0] == DEPTH and x_sample.shape[1] == 1
    n_seq, seq, d = x_prompt.shape
    n_dec = x_sample.shape[0]
    t = n_seq * seq
    d_a = ln_v_g.shape[1]
    d_b = B_HEADS * B_V_DIM
    row1 = lambda a: a.reshape(1, -1)

    lam_rows = jnp.concatenate([lambda_q1, lambda_k1, lambda_q2, lambda_k2], axis=0)
    lng, lnb = row1(ln_v_g[0]), row1(ln_v_b[0])
    b_sp_full = jnp.repeat(b_spatial[0].T, d_a // A_HEADS, axis=1)
    w00_row = row1(jnp.repeat(w_spatial[0, :, 0, 0], d_a // A_HEADS))
    b0_row = b_sp_full[0:1]
    w_router_t = w_router[0].T
    rbias_col = router_bias[0].reshape(N_EXPERTS, 1)

    xp = x_prompt.reshape(t, d)
    k_p, v_p, qb, kb, vb, a_p, gv_p = _proj_prompt(xp, w_in[0], lng, lnb, w_spatial[0], b_sp_full, seq)
    b_p = _attn_prompt(lam_rows, row1(subln_g[0]), qb, kb, vb, n_seq, seq)
    mix = functools.partial(_mix_route, w_out=w_out[0], g=row1(ln1_g[0]), bb=row1(ln1_b[0]),
                            w_router_t=w_router_t, rbias_col=rbias_col)
    h1_p, eidx_p, gw_p, rank_p, counts_p = mix(a_p, b_p, xp, base_counts=jnp.zeros((N_EXPERTS, 1), F32))

    xs_ = x_sample.reshape(n_dec, d)
    k_s, v_s, q_s, a_s, gv_s = _proj_sample(xs_, w_in[0], lng, lnb, w00_row, b0_row)
    n_phys, page = cache_k.shape[1], cache_k.shape[2]
    ck = cache_k[0].transpose(0, 2, 3, 1).reshape(n_phys, 2 * B_HEADS * B_HEAD_DIM, page)
    cv = cache_v[0].reshape(n_phys, page * B_HEADS, B_V_DIM)
    b_s = _decode_attn(page_table, lam_rows, row1(subln_g[0]), q_s, k_s, v_s, ck, cv)
    h1_s, eidx_s, gw_s, rank_s, counts = mix(a_s, b_s, xs_, base_counts=counts_p)

    m_rows = -(-((t + n_dec) * TOP_K + N_EXPERTS * (MOE_TILE - 1)) // MOE_TILE) * MOE_TILE
    counts = counts.reshape(N_EXPERTS).astype(jnp.int32)
    pstart, tile_expert, n_used = _tile_plan(counts, m_rows)
    xs_sorted, rows_p = _dispatch(pstart, counts, eidx_p, rank_p, h1_p, None, m_rows)
    xs_sorted, rows_s = _dispatch(pstart, counts, eidx_s, rank_s, h1_s, xs_sorted, m_rows)
    ys = _experts(tile_expert, n_used, xs_sorted, w_exp_gate[0], w_exp_up[0], w_exp_down[0])

    fin = functools.partial(_final, ys=ys, w_sg=w_sh_gate[0], w_su=w_sh_up[0], w_sd=w_sh_down[0], g=row1(ln2_g[0]),
                            bb=row1(ln2_b[0]), w_pg=w_ple_gate[0], b_pg=row1(b_ple_gate[0]), w_p=w_ple[0])
    y_p = fin(rows_p, h1_p, gw_p.T, p_prompt[0].reshape(t, -1))
    y_s = fin(rows_s, h1_s, gw_s.T, p_sample[0].reshape(n_dec, -1))

    return (y_p.reshape(n_seq, seq, d), y_s.reshape(n_dec, 1, d),
            k_p.reshape(1, n_seq, seq, 2 * B_HEADS, B_HEAD_DIM), v_p.reshape(1, n_seq, seq, B_HEADS, B_V_DIM),
            gv_p.reshape(1, n_seq, CHUNK, d_a),
            k_s.reshape(1, n_dec, 1, 2 * B_HEADS, B_HEAD_DIM), v_s.reshape(1, n_dec, 1, B_HEADS, B_V_DIM),
            gv_s.reshape(1, n_dec, 1, d_a))
```

```python
import functools
import math

import jax
import jax.numpy as jnp
from jax import lax
from jax.experimental import pallas as pl
from jax.experimental.pallas import tpu as pltpu

F32 = jnp.float32
BF16 = jnp.bfloat16

A_HEADS = 8
CHUNK = 128
B_HEADS = 4
B_HEAD_DIM = 64
B_V_DIM = 128
N_EXPERTS = 256
N_GROUPS = 8
GROUP_SIZE = N_EXPERTS // N_GROUPS
TOPK_GROUPS = 4
TOP_K = 8
ROUTED_SCALE = 2.5
LN_EPS = 1e-5
DEPTH = 1
DEEPNORM_ALPHA = (2 * DEPTH) ** 0.25
LAM_INIT = 0.8 - 0.6 * math.exp(-0.3 * 0)

LANES = 128
SUBLANES = 8
ROW_TILE = 256
ATTN_TK = 512
MOE_TILE = 256
DECODE_CHUNK = 16
VMEM_LIMIT = 56 * 1024 * 1024

_NT = (((1,), (1,)), ((), ()))


def _const_spec(shape):
    nd = len(shape)
    return pl.BlockSpec(shape, lambda *_: (0,) * nd)


def _layer_norm(x, g, b):
    mu = jnp.mean(x, axis=-1, keepdims=True)
    xc = x - mu
    var = jnp.mean(xc * xc, axis=-1, keepdims=True)
    return xc * lax.rsqrt(var + LN_EPS) * g + b


def _gelu(x):
    return 0.5 * x * (1.0 + lax.erf(x * (2.0 ** -0.5)))


_ERFC_P = (2.326819970068386e-2, -1.387039388740657e-1, 3.687424674597105e-1, -5.824733027278666e-1,
           6.210004621745983e-1, -4.944515323274145e-1, 3.404879937665872e-1, -2.741127028184656e-1,
           5.638259427386472e-1)
_ERFC_R = (-1.047766399936249e+1, 1.297719955372516e+1, -7.495518717768503e+0, 2.921019019210786e+0,
           -1.015265279202700e+0, 4.218463358204948e-1, -2.820767439740514e-1, 5.641895067754075e-1)


def _horner(y, coeffs):
    r = jnp.full(y.shape, coeffs[0], F32)
    for c in coeffs[1:]:
        r = r * y + c
    return r


def _gelu_tail_exact(x):
    t = -x * (2.0 ** -0.5)
    y = jnp.abs(t)
    q = 1.0 / jnp.maximum(y, 1.0)
    q2 = q * q
    tail = jnp.exp(-t * t) * q * jnp.where(y < 2.0, _horner(q2, _ERFC_P), _horner(q2, _ERFC_R))
    tail = jnp.where(t < 0.0, 2.0 - tail, tail)
    return 0.5 * x * jnp.where(y < 1.0, 1.0 - lax.erf(t), tail)


def _tree_sum0(x):
    n = x.shape[0]
    while n > 1 and n % 2 == 0:
        n //= 2
        x = x[:n] + x[n:]
    return jnp.sum(x, axis=0)


def _lam(lam_ref):
    lv = lam_ref[...]
    s1 = jnp.sum(lv[0:1] * lv[1:2], axis=-1, keepdims=True)
    s2 = jnp.sum(lv[2:3] * lv[3:4], axis=-1, keepdims=True)
    return jnp.exp(s1) - jnp.exp(s2) + LAM_INIT


def _proj_kernel(x_ref, w_in_ref, lng_ref, lnb_ref, wsp_ref, bsp_ref,
                 k_ref, v_ref, qb_ref, kb_ref, vt_ref, a_ref, gv_ref,
                 wbf_ref, wcat_ref, *, tiles_per_seq):
    i = pl.program_id(0)
    tm = x_ref.shape[0]
    d_a = lng_ref.shape[1]

    @pl.when(i == 0)
    def _init():
        rows = w_in_ref.shape[0]
        step = 128

        def body(r, c):
            sl = pl.ds(pl.multiple_of(r * step, step), step)
            wbf_ref[sl, :] = w_in_ref[sl, :].astype(BF16)
            return c

        lax.fori_loop(0, rows // step, body, 0)
        row = lax.broadcasted_iota(jnp.int32, (CHUNK, CHUNK), 0)
        col = lax.broadcasted_iota(jnp.int32, (CHUNK, CHUNK), 1)
        for j in range(A_HEADS // 2):
            w0 = jnp.where(col <= row, wsp_ref[2 * j], 0.0)
            w1 = jnp.where(col <= row, wsp_ref[2 * j + 1], 0.0)
            wcat_ref[j] = jnp.concatenate([w0, w1], axis=1).astype(BF16)

    xb = x_ref[...].astype(BF16)

    def seg(s):
        return jnp.dot(xb, wbf_ref[:, s * d_a:(s + 1) * d_a], preferred_element_type=F32)

    k = seg(3)
    k_ref[...] = k
    kb_ref[...] = k.astype(BF16)
    v = seg(4)
    v_ref[...] = v
    vt_ref[0] = v.T.astype(BF16)
    qb_ref[...] = (seg(2) * (B_HEAD_DIM ** -0.5)).astype(BF16)

    va = _layer_norm(_gelu(seg(1)), lng_ref[...], lnb_ref[...])

    @pl.when(i % tiles_per_seq == tiles_per_seq - 1)
    def _gv():
        gv_ref[0] = va[tm - CHUNK:, :]

    u = _gelu(seg(0))
    vab = va.astype(BF16)
    lane = lax.broadcasted_iota(jnp.int32, (CHUNK, LANES), 1)
    zero = jnp.zeros((CHUNK, LANES), BF16)
    for c in range(tm // CHUNK):
        r0 = c * CHUNK
        for j in range(A_HEADS // 2):
            vp = vab[r0:r0 + CHUNK, j * LANES:(j + 1) * LANES]
            stacked = jnp.concatenate([jnp.where(lane < 64, vp, zero), jnp.where(lane >= 64, vp, zero)], axis=0)
            sg = jnp.dot(wcat_ref[j], stacked, preferred_element_type=F32) + bsp_ref[:, j * LANES:(j + 1) * LANES]
            a_ref[r0:r0 + CHUNK, j * LANES:(j + 1) * LANES] = (
                u[r0:r0 + CHUNK, j * LANES:(j + 1) * LANES] * sg).astype(BF16)


def _proj_prompt(x, w_in, lng, lnb, w_sp, b_sp_full, seq):
    t, d = x.shape
    d_in = w_in.shape[1]
    d_a = lng.shape[1]
    tm = min(ROW_TILE, seq)
    tiles_per_seq = seq // tm
    n_seq = t // seq
    row = lambda w: pl.BlockSpec((tm, w), lambda i: (i, 0))
    out_shape = (
        jax.ShapeDtypeStruct((t, d_a), F32), jax.ShapeDtypeStruct((t, d_a), F32),
        jax.ShapeDtypeStruct((t, d_a), BF16), jax.ShapeDtypeStruct((t, d_a), BF16),
        jax.ShapeDtypeStruct((t // tm, d_a, tm), BF16), jax.ShapeDtypeStruct((t, d_a), BF16),
        jax.ShapeDtypeStruct((n_seq, CHUNK, d_a), F32),
    )
    return pl.pallas_call(
        functools.partial(_proj_kernel, tiles_per_seq=tiles_per_seq),
        grid=(t // tm,),
        in_specs=[row(d), _const_spec((d, d_in)), _const_spec((1, d_a)), _const_spec((1, d_a)),
                  _const_spec((A_HEADS, CHUNK, CHUNK)), _const_spec((CHUNK, d_a))],
        out_specs=(row(d_a), row(d_a), row(d_a), row(d_a), pl.BlockSpec((1, d_a, tm), lambda i: (i, 0, 0)), row(d_a),
                   pl.BlockSpec((1, CHUNK, d_a), lambda i: (i // tiles_per_seq, 0, 0))),
        out_shape=out_shape,
        scratch_shapes=[pltpu.VMEM((d, d_in), BF16), pltpu.VMEM((A_HEADS // 2, CHUNK, 2 * CHUNK), BF16)],
        compiler_params=pltpu.CompilerParams(dimension_semantics=("arbitrary",), vmem_limit_bytes=VMEM_LIMIT),
        name="proj_prompt",
    )(x, w_in, lng, lnb, w_sp, b_sp_full)


def _attn_kernel(lam_ref, gcol_ref, q_ref, k_ref, vt_ref, o_ref, acc, m, l):
    i = pl.program_id(2)
    tq = q_ref.shape[0]
    per_wide = max(1, min(ATTN_TK // tq, vt_ref.shape[1]))
    qb = q_ref[...]
    lane = lax.broadcasted_iota(jnp.int32, qb.shape, 1)
    zero = jnp.zeros(qb.shape, BF16)
    q2 = jnp.concatenate([jnp.where(lane < B_HEAD_DIM, qb, zero), jnp.where(lane >= B_HEAD_DIM, qb, zero)], axis=0)

    acc[...] = jnp.zeros(acc.shape, F32)
    m[...] = jnp.full(m.shape, -jnp.inf, F32)
    l[...] = jnp.zeros(l.shape, F32)

    def block(c0, n_ch, causal):
        width = n_ch * tq
        kb = k_ref[pl.ds(pl.multiple_of(c0 * tq, tq), width), :]
        s = lax.dot_general(kb, q2, _NT, preferred_element_type=F32)
        if causal:
            key = lax.broadcasted_iota(jnp.int32, (width, 2 * tq), 0)
            qry = lax.broadcasted_iota(jnp.int32, (width, 2 * tq), 1) % tq
            s = jnp.where(key <= qry, s, -jnp.inf)
        m_old = m[...]
        m_new = jnp.maximum(m_old, jnp.max(s, axis=0, keepdims=True))
        alpha = jnp.exp(m_old - m_new)
        p = jnp.exp(s - m_new)
        l[...] = alpha * l[...] + jnp.sum(p, axis=0, keepdims=True)
        pb = p.astype(BF16)
        pv = jnp.dot(vt_ref[0, c0], pb[:tq], preferred_element_type=F32)
        for ch in range(1, n_ch):
            pv = pv + jnp.dot(vt_ref[0, c0 + ch], pb[ch * tq:(ch + 1) * tq], preferred_element_type=F32)
        acc[...] = alpha * acc[...] + pv
        m[...] = m_new

    n_wide = i // per_wide

    def wide_body(j, c):
        block(j * per_wide, per_wide, False)
        return c

    lax.fori_loop(0, n_wide, wide_body, 0)
    for r in range(per_wide - 1):
        @pl.when(n_wide * per_wide + r < i)
        def _():
            block(n_wide * per_wide + r, 1, False)
    block(i, 1, True)

    lam = _lam(lam_ref)
    w = acc[...] / l[...]
    o = w[:, :tq] - lam * w[:, tq:]
    ms = jnp.mean(o * o, axis=0, keepdims=True)
    o = o * lax.rsqrt(ms + LN_EPS) * gcol_ref[...] * (1.0 - LAM_INIT)
    o_ref[...] = o.T.astype(o_ref.dtype)


def _attn_prompt(lam_rows, subln_g_col, qb, kb, vt, n_seq, seq):
    t, d_b = qb.shape
    tq = vt.shape[2]
    nq = seq // tq
    vt4 = vt.reshape(n_seq, nq, d_b, tq)
    return pl.pallas_call(
        _attn_kernel,
        grid=(n_seq, B_HEADS, nq),
        in_specs=[_const_spec((4, B_HEAD_DIM)), _const_spec((B_V_DIM, 1)),
                  pl.BlockSpec((tq, LANES), lambda b, h, i: (b * nq + i, h)),
                  pl.BlockSpec((seq, LANES), lambda b, h, i: (b, h)),
                  pl.BlockSpec((1, nq, B_V_DIM, tq), lambda b, h, i: (b, 0, h, 0))],
        out_specs=pl.BlockSpec((tq, LANES), lambda b, h, i: (b * nq + i, h)),
        out_shape=jax.ShapeDtypeStruct((t, d_b), BF16),
        scratch_shapes=[pltpu.VMEM((B_V_DIM, 2 * tq), F32), pltpu.VMEM((1, 2 * tq), F32),
                        pltpu.VMEM((1, 2 * tq), F32)],
        compiler_params=pltpu.CompilerParams(dimension_semantics=("arbitrary", "arbitrary", "arbitrary"),
                                             vmem_limit_bytes=VMEM_LIMIT),
        name="attn_prompt",
    )(lam_rows, subln_g_col, qb, kb, vt4)


def _route(logits_t, bias_col):
    e, n = logits_t.shape
    scores = jax.nn.sigmoid(logits_t)
    biased = scores + bias_col
    ninf = jnp.float32(-jnp.inf)
    gs = []
    sub = lax.broadcasted_iota(jnp.int32, (GROUP_SIZE, n), 0)
    for g in range(N_GROUPS):
        blk = biased[g * GROUP_SIZE:(g + 1) * GROUP_SIZE]
        t1 = jnp.max(blk, axis=0, keepdims=True)
        i1 = jnp.min(jnp.where(blk == t1, sub, GROUP_SIZE), axis=0, keepdims=True)
        t2 = jnp.max(jnp.where(sub == i1, ninf, blk), axis=0, keepdims=True)
        gs.append(t1 + t2)
    keep = []
    for g in range(N_GROUPS):
        cnt = jnp.zeros((1, n), jnp.int32)
        for o in range(N_GROUPS):
            if o == g:
                continue
            beats = (gs[o] >= gs[g]) if o < g else (gs[o] > gs[g])
            cnt = cnt + beats.astype(jnp.int32)
        keep.append(cnt < TOPK_GROUPS)
    masked = jnp.concatenate(
        [jnp.where(keep[g], biased[g * GROUP_SIZE:(g + 1) * GROUP_SIZE], ninf) for g in range(N_GROUPS)], axis=0)
    rid = lax.broadcasted_iota(jnp.int32, (e, n), 0)
    idxs, vals, hits = [], [], []
    for _ in range(TOP_K):
        top = jnp.max(masked, axis=0, keepdims=True)
        idx = jnp.min(jnp.where(masked == top, rid, e), axis=0, keepdims=True)
        hit = rid == idx
        vals.append(jnp.sum(jnp.where(hit, scores, 0.0), axis=0, keepdims=True))
        idxs.append(idx)
        hits.append(hit)
        masked = jnp.where(hit, ninf, masked)
    eidx = jnp.concatenate(idxs, axis=0)
    gw = jnp.concatenate(vals, axis=0)
    gw = gw / jnp.sum(gw, axis=0, keepdims=True) * ROUTED_SCALE
    return eidx, gw, hits


def _mix_kernel(a_ref, b_ref, x_ref, wout_ref, g_ref, bb_ref, wrt_ref, rb_ref, base_ref,
                h1_ref, eidx_ref, gw_ref, rank_ref, cnt_ref, wo_s, wr_s, before_s, seen_s):
    i = pl.program_id(0)
    d_a = a_ref.shape[1]
    tm = x_ref.shape[0]

    @pl.when(i == 0)
    def _init():
        wo_s[...] = wout_ref[...].astype(BF16)
        wr_s[...] = wrt_ref[...].astype(BF16)
        r = lax.broadcasted_iota(jnp.int32, (tm, tm), 0)
        c = lax.broadcasted_iota(jnp.int32, (tm, tm), 1)
        before_s[...] = jnp.where(r < c, 1.0, 0.0).astype(BF16)
        seen_s[...] = base_ref[...]

    dot = functools.partial(jnp.dot, preferred_element_type=F32)
    mix = dot(a_ref[...], wo_s[:d_a, :]) + dot(b_ref[...], wo_s[d_a:, :])
    h1 = _layer_norm(DEEPNORM_ALPHA * x_ref[...] + mix, g_ref[...], bb_ref[...])
    h1_ref[...] = h1
    logits_t = lax.dot_general(wr_s[...], h1.astype(BF16), _NT, preferred_element_type=F32)
    eidx, gw, hits = _route(logits_t, rb_ref[...])
    eidx_ref[...] = eidx
    gw_ref[...] = gw
    onehot = jnp.zeros(logits_t.shape, F32)
    for hit in hits:
        onehot = jnp.where(hit, 1.0, onehot)
    earlier = dot(onehot.astype(BF16), before_s[...]) + seen_s[...]
    rank_ref[...] = jnp.concatenate(
        [jnp.sum(jnp.where(hit, earlier, 0.0), axis=0, keepdims=True) for hit in hits], axis=0).astype(jnp.int32)
    seen = seen_s[...] + jnp.sum(onehot, axis=1, keepdims=True)
    seen_s[...] = seen
    cnt_ref[...] = seen


def _mix_route(a, b, x, w_out, g, bb, w_router_t, rbias_col, base_counts):
    t, d = x.shape
    d_a = a.shape[1]
    tm = min(ROW_TILE, t)
    row = lambda w: pl.BlockSpec((tm, w), lambda i: (i, 0))
    colb = pl.BlockSpec((TOP_K, tm), lambda i: (0, i))
    return pl.pallas_call(
        _mix_kernel,
        grid=(t // tm,),
        in_specs=[row(d_a), row(d_a), row(d), _const_spec((d, d)), _const_spec((1, d)), _const_spec((1, d)),
                  _const_spec((N_EXPERTS, d)), _const_spec((N_EXPERTS, 1)), _const_spec((N_EXPERTS, 1))],
        out_specs=(row(d), colb, colb, colb, _const_spec((N_EXPERTS, 1))),
        out_shape=(jax.ShapeDtypeStruct((t, d), F32),
                   jax.ShapeDtypeStruct((TOP_K, t), jnp.int32), jax.ShapeDtypeStruct((TOP_K, t), F32),
                   jax.ShapeDtypeStruct((TOP_K, t), jnp.int32), jax.ShapeDtypeStruct((N_EXPERTS, 1), F32)),
        scratch_shapes=[pltpu.VMEM((d, d), BF16), pltpu.VMEM((N_EXPERTS, d), BF16), pltpu.VMEM((tm, tm), BF16),
                        pltpu.VMEM((N_EXPERTS, 1), F32)],
        compiler_params=pltpu.CompilerParams(dimension_semantics=("arbitrary",), vmem_limit_bytes=VMEM_LIMIT),
        name="mix_route",
    )(a, b, x, w_out, g, bb, w_router_t, rbias_col, base_counts)


def _to_row_tiles(x, ref):
    n = x.shape[0]
    for c in range(SUBLANES):
        ref[pl.ds(c, n, stride=SUBLANES), :] = x[:, c * LANES:(c + 1) * LANES]


def _from_row_tiles(ref, n):
    return jnp.concatenate([ref[pl.ds(c, n, stride=SUBLANES), :] for c in range(SUBLANES)], axis=1)


def _tile_copy(src, src_row, dst, dst_row, sem):
    def tile(ref, r):
        return ref.at[pl.ds(pl.multiple_of(r * SUBLANES, SUBLANES), SUBLANES), :]
    return pltpu.make_async_copy(tile(src, src_row), tile(dst, dst_row), sem)


def _dispatch_rows(pstart_ref, eidx_ref, rank_ref, h1_ref, xs_ref, row_ref, stage, sems):
    i = pl.program_id(0)
    tm = h1_ref.shape[0]
    slot = i % 2
    _to_row_tiles(h1_ref[...], stage.at[slot])

    def start_token(j, c):
        for k in range(TOP_K):
            row = pstart_ref[eidx_ref[k, j]] + rank_ref[k, j]
            row_ref[k, j] = row
            _tile_copy(stage.at[slot], j, xs_ref, row, sems.at[slot]).start(priority=k % 2)
        return c

    def wait_all(s):
        def wait_token(j, c):
            for k in range(TOP_K):
                _tile_copy(stage.at[s], 0, xs_ref, 0, sems.at[s]).wait()
            return c
        lax.fori_loop(0, tm, wait_token, 0)

    lax.fori_loop(0, tm, start_token, 0)

    @pl.when(i > 0)
    def _():
        wait_all(1 - slot)

    @pl.when(i == pl.num_programs(0) - 1)
    def _():
        wait_all(slot)


def _dispatch_first_kernel(pstart_ref, cnt_ref, eidx_ref, rank_ref, h1_ref, xs_ref, row_ref, stage, zero_s, sems,
                           zsem):
    @pl.when(pl.program_id(0) == 0)
    def _pad():
        zero_s[...] = jnp.zeros(zero_s.shape, F32)

        def per_expert(action):
            def body(e, c):
                lo = pstart_ref[e] + cnt_ref[e]
                hi = pstart_ref[e] + (cnt_ref[e] + MOE_TILE - 1) // MOE_TILE * MOE_TILE

                def one(r, cc):
                    action(_tile_copy(zero_s, 0, xs_ref, r, zsem))
                    return cc

                lax.fori_loop(lo, hi, one, 0)
                return c
            return body

        lax.fori_loop(0, N_EXPERTS, per_expert(lambda cp: cp.start()), 0)
        lax.fori_loop(0, N_EXPERTS, per_expert(lambda cp: cp.wait()), 0)

    _dispatch_rows(pstart_ref, eidx_ref, rank_ref, h1_ref, xs_ref, row_ref, stage, sems)


def _dispatch_more_kernel(pstart_ref, cnt_ref, eidx_ref, rank_ref, h1_ref, xs_in_ref, xs_ref, row_ref, stage, sems):
    del cnt_ref, xs_in_ref
    _dispatch_rows(pstart_ref, eidx_ref, rank_ref, h1_ref, xs_ref, row_ref, stage, sems)


def _dispatch(pstart, counts, eidx, rank, h1, xs, m_rows):
    t, d = h1.shape
    assert d == SUBLANES * LANES
    tm = min(ROW_TILE, t)
    first = xs is None
    smem_blk = pl.BlockSpec((TOP_K, tm), lambda i, *_: (0, i), memory_space=pltpu.SMEM)
    any_spec = pl.BlockSpec(memory_space=pl.ANY)
    in_specs = [smem_blk, smem_blk, pl.BlockSpec((tm, d), lambda i, *_: (i, 0))]
    stage = [pltpu.VMEM((2, tm * SUBLANES, LANES), F32)]
    scratch = [pltpu.SemaphoreType.DMA((2,))]
    return pl.pallas_call(
        _dispatch_first_kernel if first else _dispatch_more_kernel,
        grid_spec=pltpu.PrefetchScalarGridSpec(
            num_scalar_prefetch=2,
            grid=(t // tm,),
            in_specs=in_specs if first else in_specs + [any_spec],
            out_specs=(any_spec, smem_blk),
            scratch_shapes=(stage + [pltpu.VMEM((SUBLANES, LANES), F32)] + scratch + [pltpu.SemaphoreType.DMA(())]
                            if first else stage + scratch)),
        out_shape=(jax.ShapeDtypeStruct((m_rows * SUBLANES, LANES), F32),
                   jax.ShapeDtypeStruct((TOP_K, t), jnp.int32)),
        input_output_aliases={} if first else {5: 0},
        compiler_params=pltpu.CompilerParams(dimension_semantics=("arbitrary",), vmem_limit_bytes=VMEM_LIMIT),
        name="dispatch_first" if first else "dispatch_more",
    )(*((pstart, counts, eidx, rank, h1) if first else (pstart, counts, eidx, rank, h1, xs)))


def _experts_kernel(te_ref, nu_ref, x_ref, wg_ref, wu_ref, wd_ref, y_ref, wgu_s, wd_s):
    t = pl.program_id(0)
    prev = te_ref[jnp.maximum(t - 1, 0)]
    d_e = wg_ref.shape[2]

    @pl.when((t < nu_ref[0]) & ((t == 0) | (te_ref[t] != prev)))
    def _cast():
        wgu_s[:, :d_e] = wg_ref[0].astype(BF16)
        wgu_s[:, d_e:] = wu_ref[0].astype(BF16)
        wd_s[...] = wd_ref[0].astype(BF16)

    @pl.when(t < nu_ref[0])
    def _compute():
        x = _from_row_tiles(x_ref, MOE_TILE).astype(BF16)
        gu = jnp.dot(x, wgu_s[...], preferred_element_type=F32)
        hid = jax.nn.silu(gu[:, :d_e]) * gu[:, d_e:]
        _to_row_tiles(jnp.dot(hid.astype(BF16), wd_s[...], preferred_element_type=F32), y_ref)


def _experts(tile_expert, n_used, xs, w_g, w_u, w_d):
    d, d_e = w_g.shape[1], w_g.shape[2]
    n_tiles = xs.shape[0] // (MOE_TILE * SUBLANES)

    def rows(t, te, nu):
        return (jnp.minimum(t, nu[0] - 1), 0)

    def wmap(t, te, nu):
        return (te[jnp.minimum(t, nu[0] - 1)], 0, 0)

    return pl.pallas_call(
        _experts_kernel,
        grid_spec=pltpu.PrefetchScalarGridSpec(
            num_scalar_prefetch=2,
            grid=(n_tiles,),
            in_specs=[pl.BlockSpec((MOE_TILE * SUBLANES, LANES), rows),
                      pl.BlockSpec((1, d, d_e), wmap), pl.BlockSpec((1, d, d_e), wmap),
                      pl.BlockSpec((1, d_e, d), wmap)],
            out_specs=pl.BlockSpec((MOE_TILE * SUBLANES, LANES), rows),
            scratch_shapes=[pltpu.VMEM((d, 2 * d_e), BF16), pltpu.VMEM((d_e, d), BF16)]),
        out_shape=jax.ShapeDtypeStruct(xs.shape, F32),
        compiler_params=pltpu.CompilerParams(dimension_semantics=("arbitrary",), vmem_limit_bytes=VMEM_LIMIT),
        name="experts",
    )(tile_expert, n_used, xs, w_g, w_u, w_d)


def _final_kernel(rows_ref, rows_next_ref, h1_ref, gw_ref, p_ref, ys_ref, wsg_ref, wsu_ref, wsd_ref, g_ref, bb_ref,
                  wpg_ref, bpg_ref, wp_ref, y_ref, wgu_s, wd_s, wpg_s, wp_s, ybuf, sems):
    i = pl.program_id(0)
    n = pl.num_programs(0)
    d_s = wsg_ref.shape[1]
    tm = h1_ref.shape[0]

    def gather(idx_ref, slot, action):
        def body(j, c):
            for k in range(TOP_K):
                src_row = idx_ref[k, j] if action == "start" else 0
                cp = _tile_copy(ys_ref, src_row, ybuf.at[slot, k], j, sems.at[slot])
                cp.start(priority=k % 2) if action == "start" else cp.wait()
            return c
        lax.fori_loop(0, tm, body, 0)

    @pl.when(i == 0)
    def _init():
        gather(rows_ref, 0, "start")
        wgu_s[:, :d_s] = wsg_ref[...].astype(BF16)
        wgu_s[:, d_s:] = wsu_ref[...].astype(BF16)
        wd_s[...] = wsd_ref[...].astype(BF16)
        wpg_s[...] = wpg_ref[...].astype(BF16)
        wp_s[...] = wp_ref[...].astype(BF16)

    slot = i % 2

    @pl.when(i + 1 < n)
    def _prefetch():
        gather(rows_next_ref, 1 - slot, "start")

    dot = functools.partial(jnp.dot, preferred_element_type=F32)
    h1 = h1_ref[...]
    gu = dot(h1.astype(BF16), wgu_s[...])
    hid = jax.nn.silu(gu[:, :d_s]) * gu[:, d_s:]
    shared = dot(hid.astype(BF16), wd_s[...])

    gather(rows_ref, slot, "wait")
    gwv = gw_ref[...]
    routed = gwv[:, 0:1] * _from_row_tiles(ybuf.at[slot, 0], tm)
    for k in range(1, TOP_K):
        routed = routed + gwv[:, k:k + 1] * _from_row_tiles(ybuf.at[slot, k], tm)

    h2 = _layer_norm(DEEPNORM_ALPHA * h1 + (routed + shared), g_ref[...], bb_ref[...])
    gate = jax.nn.sigmoid(dot(h2.astype(BF16), wpg_s[...]) + bpg_ref[...])
    y_ref[...] = h2 + gate * dot(p_ref[...].astype(BF16), wp_s[...])


def _final(row_of, h1, gw, p, ys, w_sg, w_su, w_sd, g, bb, w_pg, b_pg, w_p):
    t, d = h1.shape
    d_s = w_sg.shape[1]
    d_p = p.shape[1]
    tm = min(ROW_TILE, t)
    n = t // tm
    row = lambda w: pl.BlockSpec((tm, w), lambda i: (i, 0))
    return pl.pallas_call(
        _final_kernel,
        grid=(n,),
        in_specs=[pl.BlockSpec((TOP_K, tm), lambda i: (0, i), memory_space=pltpu.SMEM),
                  pl.BlockSpec((TOP_K, tm), lambda i: (0, jnp.minimum(i + 1, n - 1)), memory_space=pltpu.SMEM),
                  row(d), row(TOP_K), row(d_p), pl.BlockSpec(memory_space=pl.ANY),
                  _const_spec((d, d_s)), _const_spec((d, d_s)), _const_spec((d_s, d)),
                  _const_spec((1, d)), _const_spec((1, d)), _const_spec((d, d)), _const_spec((1, d)),
                  _const_spec((d_p, d))],
        out_specs=row(d),
        out_shape=jax.ShapeDtypeStruct((t, d), F32),
        scratch_shapes=[pltpu.VMEM((d, 2 * d_s), BF16), pltpu.VMEM((d_s, d), BF16), pltpu.VMEM((d, d), BF16),
                        pltpu.VMEM((d_p, d), BF16), pltpu.VMEM((2, TOP_K, tm * SUBLANES, LANES), F32),
                        pltpu.SemaphoreType.DMA((2,))],
        compiler_params=pltpu.CompilerParams(dimension_semantics=("arbitrary",), vmem_limit_bytes=VMEM_LIMIT),
        name="final",
    )(row_of, row_of, h1, gw, p, ys, w_sg, w_su, w_sd, g, bb, w_pg, b_pg, w_p)


def _proj_sample_kernel(x_ref, w_in_ref, lng_ref, lnb_ref, w00_ref, b0_ref,
                        k_ref, v_ref, q_ref, a_ref, gv_ref):
    d_a = lng_ref.shape[1]
    xb = x_ref[...].astype(BF16)

    def seg(s):
        return jnp.dot(xb, w_in_ref[:, s * d_a:(s + 1) * d_a].astype(BF16), preferred_element_type=F32)

    k_ref[...] = seg(3)
    v_ref[...] = seg(4)
    q_ref[...] = seg(2) * (B_HEAD_DIM ** -0.5)
    va = _layer_norm(_gelu_tail_exact(seg(1)), lng_ref[...], lnb_ref[...])
    gv_ref[...] = va
    sg = va * w00_ref[...] + b0_ref[...]
    a_ref[...] = (_gelu_tail_exact(seg(0)) * sg).astype(a_ref.dtype)


def _proj_sample(x, w_in, lng, lnb, w00_row, b0_row):
    n, d = x.shape
    d_in = w_in.shape[1]
    d_a = lng.shape[1]
    full = lambda w: _const_spec((n, w))
    f32 = jax.ShapeDtypeStruct((n, d_a), F32)
    return pl.pallas_call(
        _proj_sample_kernel,
        grid=(1,),
        in_specs=[full(d), _const_spec((d, d_in)), _const_spec((1, d_a)), _const_spec((1, d_a)),
                  _const_spec((1, d_a)), _const_spec((1, d_a))],
        out_specs=(full(d_a),) * 5,
        out_shape=(f32, f32, f32, jax.ShapeDtypeStruct((n, d_a), BF16), f32),
        compiler_params=pltpu.CompilerParams(dimension_semantics=("arbitrary",), vmem_limit_bytes=VMEM_LIMIT),
        name="proj_sample",
    )(x, w_in, lng, lnb, w00_row, b0_row)


def _decode_kernel(pt_ref, lam_ref, g_ref, q_ref, kn_ref, vn_ref, kt_ref, vr_ref, o_ref,
                   kbuf, vbuf, ksem, vsem, s_scr, acc):
    b = pl.program_id(0)
    n_seq = pl.num_programs(0)
    n_maps = 2 * B_HEADS
    d_b = q_ref.shape[2]
    chunk = kbuf.shape[1]
    n_chunks = s_scr.shape[0] // chunk
    page = s_scr.shape[2]

    def chunk_copies(cache_ref, buf, sem, seq, c, slot):
        return [pltpu.make_async_copy(cache_ref.at[pt_ref[seq, c * chunk + j]], buf.at[slot, j], sem.at[slot])
                for j in range(chunk)]

    def start(copies):
        for cp in copies:
            cp.start()

    def wait(copies):
        for cp in copies:
            cp.wait()

    @pl.when(b == 0)
    def _first_chunk():
        start(chunk_copies(kt_ref, kbuf, ksem, 0, 0, 0))

    lane = lax.broadcasted_iota(jnp.int32, (n_maps, d_b), 1)
    mrow = lax.broadcasted_iota(jnp.int32, (n_maps, d_b), 0)
    own_map = jnp.where(mrow < B_HEADS, 2 * mrow, 2 * (mrow - B_HEADS) + 1)
    qrow = jnp.where(lane // B_HEAD_DIM == own_map, jnp.broadcast_to(q_ref[0], (n_maps, d_b)), 0.0).astype(BF16)

    def score_chunk(c, carry):
        slot = c % 2

        @pl.when(c + 1 < n_chunks)
        def _():
            start(chunk_copies(kt_ref, kbuf, ksem, b, c + 1, 1 - slot))

        @pl.when(c + 1 == n_chunks)
        def _():
            start(chunk_copies(vr_ref, vbuf, vsem, b, 0, 0))

        wait(chunk_copies(kt_ref, kbuf, ksem, b, c, slot))
        for j in range(chunk):
            s_scr[c * chunk + j] = jnp.dot(qrow, kbuf[slot, j].astype(BF16), preferred_element_type=F32)
        return carry

    lax.fori_loop(0, n_chunks, score_chunk, 0)

    lam = _lam(lam_ref)

    def diff_weights(w):
        top = w[:B_HEADS] - lam * w[B_HEADS:]
        return jnp.concatenate([top, jnp.zeros_like(top)], axis=0).astype(BF16)

    kn = kn_ref[0].astype(BF16).astype(F32)
    snew = jnp.sum(qrow.astype(F32) * kn, axis=-1, keepdims=True)
    s_all = s_scr[...]
    m = jnp.maximum(jnp.max(jnp.max(s_all, axis=0), axis=-1, keepdims=True), snew)
    e_new = jnp.exp(snew - m)
    z = jnp.sum(_tree_sum0(jnp.exp(s_all - m)), axis=-1, keepdims=True) + e_new
    wd_new = diff_weights(e_new / z).astype(F32)
    vn = vn_ref[0].astype(BF16).astype(F32)
    for h in range(B_HEADS):
        acc[h:h + 1, :] = wd_new[h:h + 1] * vn[:, h * B_V_DIM:(h + 1) * B_V_DIM]

    def value_chunk(c, carry):
        slot = c % 2

        @pl.when(c + 1 < n_chunks)
        def _():
            start(chunk_copies(vr_ref, vbuf, vsem, b, c + 1, 1 - slot))

        @pl.when((c + 1 == n_chunks) & (b + 1 < n_seq))
        def _():
            start(chunk_copies(kt_ref, kbuf, ksem, b + 1, 0, 0))

        wait(chunk_copies(vr_ref, vbuf, vsem, b, c, slot))
        for j in range(chunk):
            wd = diff_weights(jnp.exp(s_scr[c * chunk + j] - m) / z)
            for h in range(B_HEADS):
                vh = vbuf[slot, j, pl.ds(h, page, stride=B_HEADS), :].astype(BF16)
                acc[h:h + 1, :] += jnp.dot(wd, vh, preferred_element_type=F32)[h:h + 1]
        return carry

    lax.fori_loop(0, n_chunks, value_chunk, 0)

    for h in range(B_HEADS):
        oh = acc[h:h + 1, :]
        ms = jnp.mean(oh * oh, axis=-1, keepdims=True)
        o_ref[0, :, h * B_V_DIM:(h + 1) * B_V_DIM] = (
            oh * lax.rsqrt(ms + LN_EPS) * g_ref[...] * (1.0 - LAM_INIT)).astype(o_ref.dtype)


def _decode_attn(page_table, lam_rows, subln_g, q, k_new, v_new, cache_k, cache_v):
    n, d_b = q.shape
    n_pages = page_table.shape[1]
    page = cache_k.shape[2]
    chunk = math.gcd(n_pages, DECODE_CHUNK)
    q3, k3, v3 = (a.reshape(n, 1, d_b) for a in (q, k_new, v_new))
    per_seq = pl.BlockSpec((1, 1, d_b), lambda b, pt: (b, 0, 0))
    any_spec = pl.BlockSpec(memory_space=pl.ANY)
    out = pl.pallas_call(
        _decode_kernel,
        grid_spec=pltpu.PrefetchScalarGridSpec(
            num_scalar_prefetch=1,
            grid=(n,),
            in_specs=[pl.BlockSpec((4, B_HEAD_DIM), lambda b, pt: (0, 0)),
                      pl.BlockSpec((1, B_V_DIM), lambda b, pt: (0, 0)),
                      per_seq, per_seq, per_seq, any_spec, any_spec],
            out_specs=per_seq,
            scratch_shapes=[pltpu.VMEM((2, chunk) + cache_k.shape[1:], F32),
                            pltpu.VMEM((2, chunk) + cache_v.shape[1:], F32),
                            pltpu.SemaphoreType.DMA((2,)), pltpu.SemaphoreType.DMA((2,)),
                            pltpu.VMEM((n_pages, 2 * B_HEADS, page), F32), pltpu.VMEM((B_HEADS, B_V_DIM), F32)]),
        out_shape=jax.ShapeDtypeStruct((n, 1, d_b), BF16),
        compiler_params=pltpu.CompilerParams(dimension_semantics=("arbitrary",), vmem_limit_bytes=VMEM_LIMIT),
        name="decode_attn",
    )(page_table, lam_rows, subln_g, q3, k3, v3, cache_k, cache_v)
    return out.reshape(n, d_b)


def _tile_plan(counts, m_rows):
    padded = (counts + MOE_TILE - 1) // MOE_TILE * MOE_TILE
    pends = jnp.cumsum(padded)
    n_tiles = m_rows // MOE_TILE
    tile_expert = jnp.minimum(
        jnp.searchsorted(pends, jnp.arange(n_tiles, dtype=jnp.int32) * MOE_TILE, side='right'),
        N_EXPERTS - 1).astype(jnp.int32)
    n_used = (pends[-1:] // MOE_TILE).astype(jnp.int32)
    return (pends - padded).astype(jnp.int32), tile_expert, n_used


def kernel(x_prompt, x_sample, cache_k, cache_v, page_table, p_prompt, p_sample, w_in, ln_v_g, ln_v_b, w_spatial,
           b_spatial, lambda_q1, lambda_k1, lambda_q2, lambda_k2, subln_g, w_out, ln1_g, ln1_b, w_router,
           router_bias, w_exp_gate, w_exp_up, w_exp_down, w_sh_gate, w_sh_up, w_sh_down, ln2_g, ln2_b, w_ple,
           w_ple_gate, b_ple_gate):
    assert w_in.shape[0] == DEPTH and x_sample.shape[1] == 1
    n_seq, seq, d = x_prompt.shape
    n_dec = x_sample.shape[0]
    t = n_seq * seq
    d_a = ln_v_g.shape[1]
    d_b = B_HEADS * B_V_DIM
    row1 = lambda a: a.reshape(1, -1)

    lam_rows = jnp.concatenate([lambda_q1, lambda_k1, lambda_q2, lambda_k2], axis=0)
    lng, lnb = row1(ln_v_g[0]), row1(ln_v_b[0])
    b_sp_full = jnp.repeat(b_spatial[0].T, d_a // A_HEADS, axis=1)
    w00_row = row1(jnp.repeat(w_spatial[0, :, 0, 0], d_a // A_HEADS))
    b0_row = b_sp_full[0:1]
    w_router_t = w_router[0].T
    rbias_col = router_bias[0].reshape(N_EXPERTS, 1)

    xp = x_prompt.reshape(t, d)
    k_p, v_p, qb, kb, vt, a_p, gv_p = _proj_prompt(xp, w_in[0], lng, lnb, w_spatial[0], b_sp_full, seq)
    b_p = _attn_prompt(lam_rows, subln_g[0].reshape(B_V_DIM, 1), qb, kb, vt, n_seq, seq)
    mix = functools.partial(_mix_route, w_out=w_out[0], g=row1(ln1_g[0]), bb=row1(ln1_b[0]),
                            w_router_t=w_router_t, rbias_col=rbias_col)
    h1_p, eidx_p, gw_p, rank_p, counts_p = mix(a_p, b_p, xp, base_counts=jnp.zeros((N_EXPERTS, 1), F32))

    xs_ = x_sample.reshape(n_dec, d)
    k_s, v_s, q_s, a_s, gv_s = _proj_sample(xs_, w_in[0], lng, lnb, w00_row, b0_row)
    n_phys, page = cache_k.shape[1], cache_k.shape[2]
    ck = cache_k[0].transpose(0, 2, 3, 1).reshape(n_phys, 2 * B_HEADS * B_HEAD_DIM, page)
    cv = cache_v[0].reshape(n_phys, page * B_HEADS, B_V_DIM)
    b_s = _decode_attn(page_table, lam_rows, row1(subln_g[0]), q_s, k_s, v_s, ck, cv)
    h1_s, eidx_s, gw_s, rank_s, counts = mix(a_s, b_s, xs_, base_counts=counts_p)

    m_rows = -(-((t + n_dec) * TOP_K + N_EXPERTS * (MOE_TILE - 1)) // MOE_TILE) * MOE_TILE
    counts = counts.reshape(N_EXPERTS).astype(jnp.int32)
    pstart, tile_expert, n_used = _tile_plan(counts, m_rows)
    xs_sorted, rows_p = _dispatch(pstart, counts, eidx_p, rank_p, h1_p, None, m_rows)
    xs_sorted, rows_s = _dispatch(pstart, counts, eidx_s, rank_s, h1_s, xs_sorted, m_rows)
    ys = _experts(tile_expert, n_used, xs_sorted, w_exp_gate[0], w_exp_up[0], w_exp_down[0])

    fin = functools.partial(_final, ys=ys, w_sg=w_sh_gate[0], w_su=w_sh_up[0], w_sd=w_sh_down[0], g=row1(ln2_g[0]),
                            bb=row1(ln2_b[0]), w_pg=w_ple_gate[0], b_pg=row1(b_ple_gate[0]), w_p=w_ple[0])
    y_p = fin(rows_p, h1_p, gw_p.T, p_prompt[0].reshape(t, -1))
    y_s = fin(rows_s, h1_s, gw_s.T, p_sample[0].reshape(n_dec, -1))

    return (y_p.reshape(n_seq, seq, d), y_s.reshape(n_dec, 1, d),
            k_p.reshape(1, n_seq, seq, 2 * B_HEADS, B_HEAD_DIM), v_p.reshape(1, n_seq, seq, B_HEADS, B_V_DIM),
            gv_p.reshape(1, n_seq, CHUNK, d_a),
            k_s.reshape(1, n_dec, 1, 2 * B_HEADS, B_HEAD_DIM), v_s.reshape(1, n_dec, 1, B_HEADS, B_V_DIM),
            gv_s.reshape(1, n_dec, 1, d_a))
```

```python
import functools
import math

import jax
import jax.numpy as jnp
from jax import lax
from jax.experimental import pallas as pl
from jax.experimental.pallas import tpu as pltpu

F32 = jnp.float32
BF16 = jnp.bfloat16

A_HEADS = 8
CHUNK = 128
B_HEADS = 4
B_HEAD_DIM = 64
B_V_DIM = 128
N_EXPERTS = 256
N_GROUPS = 8
GROUP_SIZE = N_EXPERTS // N_GROUPS
TOPK_GROUPS = 4
TOP_K = 8
ROUTED_SCALE = 2.5
LN_EPS = 1e-5
DEPTH = 1
DEEPNORM_ALPHA = (2 * DEPTH) ** 0.25
LAM_INIT = 0.8 - 0.6 * math.exp(-0.3 * 0)

LANES = 128
SUBLANES = 8
ROW_WORDS = 4
ROW_TILE = 256
ATTN_TK = 512
MOE_TILE = 256
DECODE_CHUNK = 16
VMEM_LIMIT = 56 * 1024 * 1024

_NT = (((1,), (1,)), ((), ()))


def _const_spec(shape):
    nd = len(shape)
    return pl.BlockSpec(shape, lambda *_: (0,) * nd)


def _layer_norm(x, g, b):
    mu = jnp.mean(x, axis=-1, keepdims=True)
    xc = x - mu
    var = jnp.mean(xc * xc, axis=-1, keepdims=True)
    return xc * lax.rsqrt(var + LN_EPS) * g + b


def _gelu(x):
    return 0.5 * x * (1.0 + lax.erf(x * (2.0 ** -0.5)))


_ERFC_P = (2.326819970068386e-2, -1.387039388740657e-1, 3.687424674597105e-1, -5.824733027278666e-1,
           6.210004621745983e-1, -4.944515323274145e-1, 3.404879937665872e-1, -2.741127028184656e-1,
           5.638259427386472e-1)
_ERFC_R = (-1.047766399936249e+1, 1.297719955372516e+1, -7.495518717768503e+0, 2.921019019210786e+0,
           -1.015265279202700e+0, 4.218463358204948e-1, -2.820767439740514e-1, 5.641895067754075e-1)


def _horner(y, coeffs):
    r = jnp.full(y.shape, coeffs[0], F32)
    for c in coeffs[1:]:
        r = r * y + c
    return r


def _gelu_tail_exact(x):
    t = -x * (2.0 ** -0.5)
    y = jnp.abs(t)
    q = 1.0 / jnp.maximum(y, 1.0)
    q2 = q * q
    tail = jnp.exp(-t * t) * q * jnp.where(y < 2.0, _horner(q2, _ERFC_P), _horner(q2, _ERFC_R))
    tail = jnp.where(t < 0.0, 2.0 - tail, tail)
    return 0.5 * x * jnp.where(y < 1.0, 1.0 - lax.erf(t), tail)


def _tree_sum0(x):
    n = x.shape[0]
    while n > 1 and n % 2 == 0:
        n //= 2
        x = x[:n] + x[n:]
    return jnp.sum(x, axis=0)


def _lam(lam_ref):
    lv = lam_ref[...]
    s1 = jnp.sum(lv[0:1] * lv[1:2], axis=-1, keepdims=True)
    s2 = jnp.sum(lv[2:3] * lv[3:4], axis=-1, keepdims=True)
    return jnp.exp(s1) - jnp.exp(s2) + LAM_INIT


def _proj_kernel(x_ref, w_in_ref, lng_ref, lnb_ref, wsp_ref, bsp_ref,
                 k_ref, v_ref, qb_ref, kb_ref, vt_ref, a_ref, gv_ref,
                 wbf_ref, wcat_ref, *, tiles_per_seq):
    i = pl.program_id(0)
    tm = x_ref.shape[0]
    d_a = lng_ref.shape[1]

    @pl.when(i == 0)
    def _init():
        rows = w_in_ref.shape[0]
        step = 128

        def body(r, c):
            sl = pl.ds(pl.multiple_of(r * step, step), step)
            wbf_ref[sl, :] = w_in_ref[sl, :].astype(BF16)
            return c

        lax.fori_loop(0, rows // step, body, 0)
        row = lax.broadcasted_iota(jnp.int32, (CHUNK, CHUNK), 0)
        col = lax.broadcasted_iota(jnp.int32, (CHUNK, CHUNK), 1)
        for j in range(A_HEADS // 2):
            w0 = jnp.where(col <= row, wsp_ref[2 * j], 0.0)
            w1 = jnp.where(col <= row, wsp_ref[2 * j + 1], 0.0)
            wcat_ref[j] = jnp.concatenate([w0, w1], axis=1).astype(BF16)

    xb = x_ref[...].astype(BF16)

    def seg(s):
        return jnp.dot(xb, wbf_ref[:, s * d_a:(s + 1) * d_a], preferred_element_type=F32)

    k = seg(3)
    k_ref[...] = k
    kb_ref[...] = k.astype(BF16)
    v = seg(4)
    v_ref[...] = v
    vt_ref[0] = v.T.astype(BF16)
    qb_ref[...] = (seg(2) * (B_HEAD_DIM ** -0.5)).astype(BF16)

    va = _layer_norm(_gelu(seg(1)), lng_ref[...], lnb_ref[...])

    @pl.when(i % tiles_per_seq == tiles_per_seq - 1)
    def _gv():
        gv_ref[0] = va[tm - CHUNK:, :]

    u = _gelu(seg(0))
    vab = va.astype(BF16)
    lane = lax.broadcasted_iota(jnp.int32, (CHUNK, LANES), 1)
    zero = jnp.zeros((CHUNK, LANES), BF16)
    for c in range(tm // CHUNK):
        r0 = c * CHUNK
        for j in range(A_HEADS // 2):
            vp = vab[r0:r0 + CHUNK, j * LANES:(j + 1) * LANES]
            stacked = jnp.concatenate([jnp.where(lane < 64, vp, zero), jnp.where(lane >= 64, vp, zero)], axis=0)
            sg = jnp.dot(wcat_ref[j], stacked, preferred_element_type=F32) + bsp_ref[:, j * LANES:(j + 1) * LANES]
            a_ref[r0:r0 + CHUNK, j * LANES:(j + 1) * LANES] = (
                u[r0:r0 + CHUNK, j * LANES:(j + 1) * LANES] * sg).astype(BF16)


def _proj_prompt(x, w_in, lng, lnb, w_sp, b_sp_full, seq):
    t, d = x.shape
    d_in = w_in.shape[1]
    d_a = lng.shape[1]
    tm = min(ROW_TILE, seq)
    tiles_per_seq = seq // tm
    n_seq = t // seq
    row = lambda w: pl.BlockSpec((tm, w), lambda i: (i, 0))
    out_shape = (
        jax.ShapeDtypeStruct((t, d_a), F32), jax.ShapeDtypeStruct((t, d_a), F32),
        jax.ShapeDtypeStruct((t, d_a), BF16), jax.ShapeDtypeStruct((t, d_a), BF16),
        jax.ShapeDtypeStruct((t // tm, d_a, tm), BF16), jax.ShapeDtypeStruct((t, d_a), BF16),
        jax.ShapeDtypeStruct((n_seq, CHUNK, d_a), F32),
    )
    return pl.pallas_call(
        functools.partial(_proj_kernel, tiles_per_seq=tiles_per_seq),
        grid=(t // tm,),
        in_specs=[row(d), _const_spec((d, d_in)), _const_spec((1, d_a)), _const_spec((1, d_a)),
                  _const_spec((A_HEADS, CHUNK, CHUNK)), _const_spec((CHUNK, d_a))],
        out_specs=(row(d_a), row(d_a), row(d_a), row(d_a), pl.BlockSpec((1, d_a, tm), lambda i: (i, 0, 0)), row(d_a),
                   pl.BlockSpec((1, CHUNK, d_a), lambda i: (i // tiles_per_seq, 0, 0))),
        out_shape=out_shape,
        scratch_shapes=[pltpu.VMEM((d, d_in), BF16), pltpu.VMEM((A_HEADS // 2, CHUNK, 2 * CHUNK), BF16)],
        compiler_params=pltpu.CompilerParams(dimension_semantics=("arbitrary",), vmem_limit_bytes=VMEM_LIMIT),
        name="proj_prompt",
    )(x, w_in, lng, lnb, w_sp, b_sp_full)


def _attn_kernel(lam_ref, gcol_ref, q_ref, k_ref, vt_ref, o_ref, acc, m, l):
    i = pl.program_id(2)
    tq = q_ref.shape[0]
    per_wide = max(1, min(ATTN_TK // tq, vt_ref.shape[1]))
    qb = q_ref[...]
    lane = lax.broadcasted_iota(jnp.int32, qb.shape, 1)
    zero = jnp.zeros(qb.shape, BF16)
    q2 = jnp.concatenate([jnp.where(lane < B_HEAD_DIM, qb, zero), jnp.where(lane >= B_HEAD_DIM, qb, zero)], axis=0)

    acc[...] = jnp.zeros(acc.shape, F32)
    m[...] = jnp.full(m.shape, -jnp.inf, F32)
    l[...] = jnp.zeros(l.shape, F32)

    def block(c0, n_ch, causal):
        width = n_ch * tq
        kb = k_ref[pl.ds(pl.multiple_of(c0 * tq, tq), width), :]
        s = lax.dot_general(kb, q2, _NT, preferred_element_type=F32)
        if causal:
            key = lax.broadcasted_iota(jnp.int32, (width, 2 * tq), 0)
            qry = lax.broadcasted_iota(jnp.int32, (width, 2 * tq), 1) % tq
            s = jnp.where(key <= qry, s, -jnp.inf)
        m_old = m[...]
        m_new = jnp.maximum(m_old, jnp.max(s, axis=0, keepdims=True))
        alpha = jnp.exp(m_old - m_new)
        p = jnp.exp(s - m_new)
        l[...] = alpha * l[...] + jnp.sum(p, axis=0, keepdims=True)
        pb = p.astype(BF16)
        pv = jnp.dot(vt_ref[0, c0], pb[:tq], preferred_element_type=F32)
        for ch in range(1, n_ch):
            pv = pv + jnp.dot(vt_ref[0, c0 + ch], pb[ch * tq:(ch + 1) * tq], preferred_element_type=F32)
        acc[...] = alpha * acc[...] + pv
        m[...] = m_new

    n_wide = i // per_wide

    def wide_body(j, c):
        block(j * per_wide, per_wide, False)
        return c

    lax.fori_loop(0, n_wide, wide_body, 0)
    for r in range(per_wide - 1):
        @pl.when(n_wide * per_wide + r < i)
        def _():
            block(n_wide * per_wide + r, 1, False)
    block(i, 1, True)

    lam = _lam(lam_ref)
    w = acc[...] / l[...]
    o = w[:, :tq] - lam * w[:, tq:]
    ms = jnp.mean(o * o, axis=0, keepdims=True)
    o = o * lax.rsqrt(ms + LN_EPS) * gcol_ref[...] * (1.0 - LAM_INIT)
    o_ref[...] = o.T.astype(o_ref.dtype)


def _attn_prompt(lam_rows, subln_g_col, qb, kb, vt, n_seq, seq):
    t, d_b = qb.shape
    tq = vt.shape[2]
    nq = seq // tq
    vt4 = vt.reshape(n_seq, nq, d_b, tq)
    return pl.pallas_call(
        _attn_kernel,
        grid=(n_seq, B_HEADS, nq),
        in_specs=[_const_spec((4, B_HEAD_DIM)), _const_spec((B_V_DIM, 1)),
                  pl.BlockSpec((tq, LANES), lambda b, h, i: (b * nq + i, h)),
                  pl.BlockSpec((seq, LANES), lambda b, h, i: (b, h)),
                  pl.BlockSpec((1, nq, B_V_DIM, tq), lambda b, h, i: (b, 0, h, 0))],
        out_specs=pl.BlockSpec((tq, LANES), lambda b, h, i: (b * nq + i, h)),
        out_shape=jax.ShapeDtypeStruct((t, d_b), BF16),
        scratch_shapes=[pltpu.VMEM((B_V_DIM, 2 * tq), F32), pltpu.VMEM((1, 2 * tq), F32),
                        pltpu.VMEM((1, 2 * tq), F32)],
        compiler_params=pltpu.CompilerParams(dimension_semantics=("arbitrary", "arbitrary", "arbitrary"),
                                             vmem_limit_bytes=VMEM_LIMIT),
        name="attn_prompt",
    )(lam_rows, subln_g_col, qb, kb, vt4)


def _route(logits_t, bias_col):
    e, n = logits_t.shape
    scores = jax.nn.sigmoid(logits_t)
    biased = scores + bias_col
    ninf = jnp.float32(-jnp.inf)
    gs = []
    sub = lax.broadcasted_iota(jnp.int32, (GROUP_SIZE, n), 0)
    for g in range(N_GROUPS):
        blk = biased[g * GROUP_SIZE:(g + 1) * GROUP_SIZE]
        t1 = jnp.max(blk, axis=0, keepdims=True)
        i1 = jnp.min(jnp.where(blk == t1, sub, GROUP_SIZE), axis=0, keepdims=True)
        t2 = jnp.max(jnp.where(sub == i1, ninf, blk), axis=0, keepdims=True)
        gs.append(t1 + t2)
    keep = []
    for g in range(N_GROUPS):
        cnt = jnp.zeros((1, n), jnp.int32)
        for o in range(N_GROUPS):
            if o == g:
                continue
            beats = (gs[o] >= gs[g]) if o < g else (gs[o] > gs[g])
            cnt = cnt + beats.astype(jnp.int32)
        keep.append(cnt < TOPK_GROUPS)
    masked = jnp.concatenate(
        [jnp.where(keep[g], biased[g * GROUP_SIZE:(g + 1) * GROUP_SIZE], ninf) for g in range(N_GROUPS)], axis=0)
    rid = lax.broadcasted_iota(jnp.int32, (e, n), 0)
    idxs, vals, hits = [], [], []
    for _ in range(TOP_K):
        top = jnp.max(masked, axis=0, keepdims=True)
        idx = jnp.min(jnp.where(masked == top, rid, e), axis=0, keepdims=True)
        hit = rid == idx
        vals.append(jnp.sum(jnp.where(hit, scores, 0.0), axis=0, keepdims=True))
        idxs.append(idx)
        hits.append(hit)
        masked = jnp.where(hit, ninf, masked)
    eidx = jnp.concatenate(idxs, axis=0)
    gw = jnp.concatenate(vals, axis=0)
    gw = gw / jnp.sum(gw, axis=0, keepdims=True) * ROUTED_SCALE
    return eidx, gw, hits


def _mix_kernel(a_ref, b_ref, x_ref, wout_ref, g_ref, bb_ref, wrt_ref, rb_ref, base_ref,
                h1_ref, eidx_ref, gw_ref, rank_ref, cnt_ref, wo_s, wr_s, before_s, seen_s):
    i = pl.program_id(0)
    d_a = a_ref.shape[1]
    tm = x_ref.shape[0]

    @pl.when(i == 0)
    def _init():
        wo_s[...] = wout_ref[...].astype(BF16)
        wr_s[...] = wrt_ref[...].astype(BF16)
        r = lax.broadcasted_iota(jnp.int32, (tm, tm), 0)
        c = lax.broadcasted_iota(jnp.int32, (tm, tm), 1)
        before_s[...] = jnp.where(r < c, 1.0, 0.0).astype(BF16)
        seen_s[...] = base_ref[...]

    dot = functools.partial(jnp.dot, preferred_element_type=F32)
    mix = dot(a_ref[...], wo_s[:d_a, :]) + dot(b_ref[...], wo_s[d_a:, :])
    h1 = _layer_norm(DEEPNORM_ALPHA * x_ref[...] + mix, g_ref[...], bb_ref[...])
    h1_ref[...] = h1
    logits_t = lax.dot_general(wr_s[...], h1.astype(BF16), _NT, preferred_element_type=F32)
    eidx, gw, hits = _route(logits_t, rb_ref[...])
    eidx_ref[...] = eidx
    gw_ref[...] = gw
    onehot = jnp.zeros(logits_t.shape, F32)
    for hit in hits:
        onehot = jnp.where(hit, 1.0, onehot)
    earlier = dot(onehot.astype(BF16), before_s[...]) + seen_s[...]
    rank_ref[...] = jnp.concatenate(
        [jnp.sum(jnp.where(hit, earlier, 0.0), axis=0, keepdims=True) for hit in hits], axis=0).astype(jnp.int32)
    seen = seen_s[...] + jnp.sum(onehot, axis=1, keepdims=True)
    seen_s[...] = seen
    cnt_ref[...] = seen


def _mix_route(a, b, x, w_out, g, bb, w_router_t, rbias_col, base_counts):
    t, d = x.shape
    d_a = a.shape[1]
    tm = min(ROW_TILE, t)
    row = lambda w: pl.BlockSpec((tm, w), lambda i: (i, 0))
    colb = pl.BlockSpec((TOP_K, tm), lambda i: (0, i))
    return pl.pallas_call(
        _mix_kernel,
        grid=(t // tm,),
        in_specs=[row(d_a), row(d_a), row(d), _const_spec((d, d)), _const_spec((1, d)), _const_spec((1, d)),
                  _const_spec((N_EXPERTS, d)), _const_spec((N_EXPERTS, 1)), _const_spec((N_EXPERTS, 1))],
        out_specs=(row(d), colb, colb, colb, _const_spec((N_EXPERTS, 1))),
        out_shape=(jax.ShapeDtypeStruct((t, d), F32),
                   jax.ShapeDtypeStruct((TOP_K, t), jnp.int32), jax.ShapeDtypeStruct((TOP_K, t), F32),
                   jax.ShapeDtypeStruct((TOP_K, t), jnp.int32), jax.ShapeDtypeStruct((N_EXPERTS, 1), F32)),
        scratch_shapes=[pltpu.VMEM((d, d), BF16), pltpu.VMEM((N_EXPERTS, d), BF16), pltpu.VMEM((tm, tm), BF16),
                        pltpu.VMEM((N_EXPERTS, 1), F32)],
        compiler_params=pltpu.CompilerParams(dimension_semantics=("arbitrary",), vmem_limit_bytes=VMEM_LIMIT),
        name="mix_route",
    )(a, b, x, w_out, g, bb, w_router_t, rbias_col, base_counts)


def _pack_bf16(x):
    w = x.shape[1] // 2
    u = pltpu.bitcast(x.astype(BF16).astype(F32), jnp.uint32)
    return (u[:, :w] >> 16) | u[:, w:]


def _unpack_bf16(u):
    lo = pltpu.bitcast(u << 16, F32)
    hi = pltpu.bitcast(u & jnp.uint32(0xFFFF0000), F32)
    return jnp.concatenate([lo, hi], axis=1)


def _to_row_tiles(x, ref):
    n, pieces = x.shape[0], x.shape[1] // LANES
    for c in range(pieces):
        ref[pl.ds(c, n, stride=pieces), :] = x[:, c * LANES:(c + 1) * LANES]


def _from_row_tiles(ref, n):
    pieces = ref.shape[0] // n
    return jnp.concatenate([ref[pl.ds(c, n, stride=pieces), :] for c in range(pieces)], axis=1)


def _tile_copy(src, src_row, dst, dst_row, sem):
    def tile(ref, r):
        return ref.at[pl.ds(pl.multiple_of(r * ROW_WORDS, ROW_WORDS), ROW_WORDS), :]
    return pltpu.make_async_copy(tile(src, src_row), tile(dst, dst_row), sem)


def _dispatch_rows(pstart_ref, eidx_ref, rank_ref, h1_ref, xs_ref, row_ref, stage, sems):
    i = pl.program_id(0)
    tm = h1_ref.shape[0]
    slot = i % 2
    _to_row_tiles(_pack_bf16(h1_ref[...]), stage.at[slot])

    def start_token(j, c):
        for k in range(TOP_K):
            row = pstart_ref[eidx_ref[k, j]] + rank_ref[k, j]
            row_ref[k, j] = row
            _tile_copy(stage.at[slot], j, xs_ref, row, sems.at[slot]).start(priority=k % 2)
        return c

    def wait_all(s):
        def wait_token(j, c):
            for k in range(TOP_K):
                _tile_copy(stage.at[s], 0, xs_ref, 0, sems.at[s]).wait()
            return c
        lax.fori_loop(0, tm, wait_token, 0)

    lax.fori_loop(0, tm, start_token, 0)

    @pl.when(i > 0)
    def _():
        wait_all(1 - slot)

    @pl.when(i == pl.num_programs(0) - 1)
    def _():
        wait_all(slot)


def _dispatch_first_kernel(pstart_ref, cnt_ref, eidx_ref, rank_ref, h1_ref, xs_ref, row_ref, stage, zero_s, sems,
                           zsem):
    @pl.when(pl.program_id(0) == 0)
    def _pad():
        zero_s[...] = jnp.zeros(zero_s.shape, zero_s.dtype)

        def per_expert(action):
            def body(e, c):
                lo = pstart_ref[e] + cnt_ref[e]
                hi = pstart_ref[e] + (cnt_ref[e] + MOE_TILE - 1) // MOE_TILE * MOE_TILE

                def one(r, cc):
                    action(_tile_copy(zero_s, 0, xs_ref, r, zsem))
                    return cc

                lax.fori_loop(lo, hi, one, 0)
                return c
            return body

        lax.fori_loop(0, N_EXPERTS, per_expert(lambda cp: cp.start()), 0)
        lax.fori_loop(0, N_EXPERTS, per_expert(lambda cp: cp.wait()), 0)

    _dispatch_rows(pstart_ref, eidx_ref, rank_ref, h1_ref, xs_ref, row_ref, stage, sems)


def _dispatch_more_kernel(pstart_ref, cnt_ref, eidx_ref, rank_ref, h1_ref, xs_in_ref, xs_ref, row_ref, stage, sems):
    del cnt_ref, xs_in_ref
    _dispatch_rows(pstart_ref, eidx_ref, rank_ref, h1_ref, xs_ref, row_ref, stage, sems)


def _dispatch(pstart, counts, eidx, rank, h1, xs, m_rows):
    t, d = h1.shape
    assert d == 2 * ROW_WORDS * LANES
    tm = min(ROW_TILE, t)
    first = xs is None
    smem_blk = pl.BlockSpec((TOP_K, tm), lambda i, *_: (0, i), memory_space=pltpu.SMEM)
    any_spec = pl.BlockSpec(memory_space=pl.ANY)
    in_specs = [smem_blk, smem_blk, pl.BlockSpec((tm, d), lambda i, *_: (i, 0))]
    stage = [pltpu.VMEM((2, tm * ROW_WORDS, LANES), jnp.uint32)]
    scratch = [pltpu.SemaphoreType.DMA((2,))]
    return pl.pallas_call(
        _dispatch_first_kernel if first else _dispatch_more_kernel,
        grid_spec=pltpu.PrefetchScalarGridSpec(
            num_scalar_prefetch=2,
            grid=(t // tm,),
            in_specs=in_specs if first else in_specs + [any_spec],
            out_specs=(any_spec, smem_blk),
            scratch_shapes=(stage + [pltpu.VMEM((SUBLANES, LANES), jnp.uint32)] + scratch
                            + [pltpu.SemaphoreType.DMA(())] if first else stage + scratch)),
        out_shape=(jax.ShapeDtypeStruct((m_rows * ROW_WORDS, LANES), jnp.uint32),
                   jax.ShapeDtypeStruct((TOP_K, t), jnp.int32)),
        input_output_aliases={} if first else {5: 0},
        compiler_params=pltpu.CompilerParams(dimension_semantics=("arbitrary",), vmem_limit_bytes=VMEM_LIMIT),
        name="dispatch_first" if first else "dispatch_more",
    )(*((pstart, counts, eidx, rank, h1) if first else (pstart, counts, eidx, rank, h1, xs)))


def _experts_kernel(te_ref, nu_ref, x_ref, wg_ref, wu_ref, wd_ref, y_ref, wgu_s, wd_s):
    t = pl.program_id(0)
    prev = te_ref[jnp.maximum(t - 1, 0)]
    d_e = wg_ref.shape[2]

    @pl.when((t < nu_ref[0]) & ((t == 0) | (te_ref[t] != prev)))
    def _cast():
        wgu_s[:, :d_e] = wg_ref[0].astype(BF16)
        wgu_s[:, d_e:] = wu_ref[0].astype(BF16)
        wd_s[...] = wd_ref[0].astype(BF16)

    @pl.when(t < nu_ref[0])
    def _compute():
        x = _unpack_bf16(_from_row_tiles(x_ref, MOE_TILE)).astype(BF16)
        gu = jnp.dot(x, wgu_s[...], preferred_element_type=F32)
        hid = jax.nn.silu(gu[:, :d_e]) * gu[:, d_e:]
        _to_row_tiles(_pack_bf16(jnp.dot(hid.astype(BF16), wd_s[...], preferred_element_type=F32)), y_ref)


def _experts(tile_expert, n_used, xs, w_g, w_u, w_d):
    d, d_e = w_g.shape[1], w_g.shape[2]
    n_tiles = xs.shape[0] // (MOE_TILE * ROW_WORDS)

    def rows(t, te, nu):
        return (jnp.minimum(t, nu[0] - 1), 0)

    def wmap(t, te, nu):
        return (te[jnp.minimum(t, nu[0] - 1)], 0, 0)

    return pl.pallas_call(
        _experts_kernel,
        grid_spec=pltpu.PrefetchScalarGridSpec(
            num_scalar_prefetch=2,
            grid=(n_tiles,),
            in_specs=[pl.BlockSpec((MOE_TILE * ROW_WORDS, LANES), rows),
                      pl.BlockSpec((1, d, d_e), wmap), pl.BlockSpec((1, d, d_e), wmap),
                      pl.BlockSpec((1, d_e, d), wmap)],
            out_specs=pl.BlockSpec((MOE_TILE * ROW_WORDS, LANES), rows),
            scratch_shapes=[pltpu.VMEM((d, 2 * d_e), BF16), pltpu.VMEM((d_e, d), BF16)]),
        out_shape=jax.ShapeDtypeStruct(xs.shape, jnp.uint32),
        compiler_params=pltpu.CompilerParams(dimension_semantics=("arbitrary",), vmem_limit_bytes=VMEM_LIMIT),
        name="experts",
    )(tile_expert, n_used, xs, w_g, w_u, w_d)


def _final_kernel(rows_ref, rows_next_ref, h1_ref, gw_ref, p_ref, ys_ref, wsg_ref, wsu_ref, wsd_ref, g_ref, bb_ref,
                  wpg_ref, bpg_ref, wp_ref, y_ref, wgu_s, wd_s, wpg_s, wp_s, ybuf, sems):
    i = pl.program_id(0)
    n = pl.num_programs(0)
    d_s = wsg_ref.shape[1]
    tm = h1_ref.shape[0]

    def gather(idx_ref, slot, action):
        def body(j, c):
            for k in range(TOP_K):
                src_row = idx_ref[k, j] if action == "start" else 0
                cp = _tile_copy(ys_ref, src_row, ybuf.at[slot, k], j, sems.at[slot])
                cp.start(priority=k % 2) if action == "start" else cp.wait()
            return c
        lax.fori_loop(0, tm, body, 0)

    @pl.when(i == 0)
    def _init():
        gather(rows_ref, 0, "start")
        wgu_s[:, :d_s] = wsg_ref[...].astype(BF16)
        wgu_s[:, d_s:] = wsu_ref[...].astype(BF16)
        wd_s[...] = wsd_ref[...].astype(BF16)
        wpg_s[...] = wpg_ref[...].astype(BF16)
        wp_s[...] = wp_ref[...].astype(BF16)

    slot = i % 2

    @pl.when(i + 1 < n)
    def _prefetch():
        gather(rows_next_ref, 1 - slot, "start")

    dot = functools.partial(jnp.dot, preferred_element_type=F32)
    h1 = h1_ref[...]
    gu = dot(h1.astype(BF16), wgu_s[...])
    hid = jax.nn.silu(gu[:, :d_s]) * gu[:, d_s:]
    shared = dot(hid.astype(BF16), wd_s[...])

    gather(rows_ref, slot, "wait")
    gwv = gw_ref[...]
    routed = gwv[:, 0:1] * _unpack_bf16(_from_row_tiles(ybuf.at[slot, 0], tm))
    for k in range(1, TOP_K):
        routed = routed + gwv[:, k:k + 1] * _unpack_bf16(_from_row_tiles(ybuf.at[slot, k], tm))

    h2 = _layer_norm(DEEPNORM_ALPHA * h1 + (routed + shared), g_ref[...], bb_ref[...])
    gate = jax.nn.sigmoid(dot(h2.astype(BF16), wpg_s[...]) + bpg_ref[...])
    y_ref[...] = h2 + gate * dot(p_ref[...].astype(BF16), wp_s[...])


def _final(row_of, h1, gw, p, ys, w_sg, w_su, w_sd, g, bb, w_pg, b_pg, w_p):
    t, d = h1.shape
    d_s = w_sg.shape[1]
    d_p = p.shape[1]
    tm = min(ROW_TILE, t)
    n = t // tm
    row = lambda w: pl.BlockSpec((tm, w), lambda i: (i, 0))
    return pl.pallas_call(
        _final_kernel,
        grid=(n,),
        in_specs=[pl.BlockSpec((TOP_K, tm), lambda i: (0, i), memory_space=pltpu.SMEM),
                  pl.BlockSpec((TOP_K, tm), lambda i: (0, jnp.minimum(i + 1, n - 1)), memory_space=pltpu.SMEM),
                  row(d), row(TOP_K), row(d_p), pl.BlockSpec(memory_space=pl.ANY),
                  _const_spec((d, d_s)), _const_spec((d, d_s)), _const_spec((d_s, d)),
                  _const_spec((1, d)), _const_spec((1, d)), _const_spec((d, d)), _const_spec((1, d)),
                  _const_spec((d_p, d))],
        out_specs=row(d),
        out_shape=jax.ShapeDtypeStruct((t, d), F32),
        scratch_shapes=[pltpu.VMEM((d, 2 * d_s), BF16), pltpu.VMEM((d_s, d), BF16), pltpu.VMEM((d, d), BF16),
                        pltpu.VMEM((d_p, d), BF16), pltpu.VMEM((2, TOP_K, tm * ROW_WORDS, LANES), jnp.uint32),
                        pltpu.SemaphoreType.DMA((2,))],
        compiler_params=pltpu.CompilerParams(dimension_semantics=("arbitrary",), vmem_limit_bytes=VMEM_LIMIT),
        name="final",
    )(row_of, row_of, h1, gw, p, ys, w_sg, w_su, w_sd, g, bb, w_pg, b_pg, w_p)


def _proj_sample_kernel(x_ref, w_in_ref, lng_ref, lnb_ref, w00_ref, b0_ref,
                        k_ref, v_ref, q_ref, a_ref, gv_ref):
    d_a = lng_ref.shape[1]
    xb = x_ref[...].astype(BF16)

    def seg(s):
        return jnp.dot(xb, w_in_ref[:, s * d_a:(s + 1) * d_a].astype(BF16), preferred_element_type=F32)

    k_ref[...] = seg(3)
    v_ref[...] = seg(4)
    q_ref[...] = seg(2) * (B_HEAD_DIM ** -0.5)
    va = _layer_norm(_gelu_tail_exact(seg(1)), lng_ref[...], lnb_ref[...])
    gv_ref[...] = va
    sg = va * w00_ref[...] + b0_ref[...]
    a_ref[...] = (_gelu_tail_exact(seg(0)) * sg).astype(a_ref.dtype)


def _proj_sample(x, w_in, lng, lnb, w00_row, b0_row):
    n, d = x.shape
    d_in = w_in.shape[1]
    d_a = lng.shape[1]
    full = lambda w: _const_spec((n, w))
    f32 = jax.ShapeDtypeStruct((n, d_a), F32)
    return pl.pallas_call(
        _proj_sample_kernel,
        grid=(1,),
        in_specs=[full(d), _const_spec((d, d_in)), _const_spec((1, d_a)), _const_spec((1, d_a)),
                  _const_spec((1, d_a)), _const_spec((1, d_a))],
        out_specs=(full(d_a),) * 5,
        out_shape=(f32, f32, f32, jax.ShapeDtypeStruct((n, d_a), BF16), f32),
        compiler_params=pltpu.CompilerParams(dimension_semantics=("arbitrary",), vmem_limit_bytes=VMEM_LIMIT),
        name="proj_sample",
    )(x, w_in, lng, lnb, w00_row, b0_row)


def _decode_kernel(pt_ref, lam_ref, g_ref, q_ref, kn_ref, vn_ref, kt_ref, vr_ref, o_ref,
                   kbuf, vbuf, ksem, vsem, s_scr, acc):
    b = pl.program_id(0)
    n_seq = pl.num_programs(0)
    n_maps = 2 * B_HEADS
    d_b = q_ref.shape[2]
    chunk = kbuf.shape[1]
    n_chunks = s_scr.shape[0] // chunk
    page = s_scr.shape[2]

    def chunk_copies(cache_ref, buf, sem, seq, c, slot):
        return [pltpu.make_async_copy(cache_ref.at[pt_ref[seq, c * chunk + j]], buf.at[slot, j], sem.at[slot])
                for j in range(chunk)]

    def start(copies):
        for cp in copies:
            cp.start()

    def wait(copies):
        for cp in copies:
            cp.wait()

    @pl.when(b == 0)
    def _first_chunk():
        start(chunk_copies(kt_ref, kbuf, ksem, 0, 0, 0))

    lane = lax.broadcasted_iota(jnp.int32, (n_maps, d_b), 1)
    mrow = lax.broadcasted_iota(jnp.int32, (n_maps, d_b), 0)
    own_map = jnp.where(mrow < B_HEADS, 2 * mrow, 2 * (mrow - B_HEADS) + 1)
    qrow = jnp.where(lane // B_HEAD_DIM == own_map, jnp.broadcast_to(q_ref[0], (n_maps, d_b)), 0.0).astype(BF16)

    def score_chunk(c, carry):
        slot = c % 2

        @pl.when(c + 1 < n_chunks)
        def _():
            start(chunk_copies(kt_ref, kbuf, ksem, b, c + 1, 1 - slot))

        @pl.when(c + 1 == n_chunks)
        def _():
            start(chunk_copies(vr_ref, vbuf, vsem, b, 0, 0))

        wait(chunk_copies(kt_ref, kbuf, ksem, b, c, slot))
        for j in range(chunk):
            s_scr[c * chunk + j] = jnp.dot(qrow, kbuf[slot, j].astype(BF16), preferred_element_type=F32)
        return carry

    lax.fori_loop(0, n_chunks, score_chunk, 0)

    lam = _lam(lam_ref)

    def diff_weights(w):
        top = w[:B_HEADS] - lam * w[B_HEADS:]
        return jnp.concatenate([top, jnp.zeros_like(top)], axis=0).astype(BF16)

    kn = kn_ref[0].astype(BF16).astype(F32)
    snew = jnp.sum(qrow.astype(F32) * kn, axis=-1, keepdims=True)
    s_all = s_scr[...]
    m = jnp.maximum(jnp.max(jnp.max(s_all, axis=0), axis=-1, keepdims=True), snew)
    e_new = jnp.exp(snew - m)
    z = jnp.sum(_tree_sum0(jnp.exp(s_all - m)), axis=-1, keepdims=True) + e_new
    wd_new = diff_weights(e_new / z).astype(F32)
    vn = vn_ref[0].astype(BF16).astype(F32)
    for h in range(B_HEADS):
        acc[h:h + 1, :] = wd_new[h:h + 1] * vn[:, h * B_V_DIM:(h + 1) * B_V_DIM]

    def value_chunk(c, carry):
        slot = c % 2

        @pl.when(c + 1 < n_chunks)
        def _():
            start(chunk_copies(vr_ref, vbuf, vsem, b, c + 1, 1 - slot))

        @pl.when((c + 1 == n_chunks) & (b + 1 < n_seq))
        def _():
            start(chunk_copies(kt_ref, kbuf, ksem, b + 1, 0, 0))

        wait(chunk_copies(vr_ref, vbuf, vsem, b, c, slot))
        for j in range(chunk):
            wd = diff_weights(jnp.exp(s_scr[c * chunk + j] - m) / z)
            for h in range(B_HEADS):
                vh = vbuf[slot, j, pl.ds(h, page, stride=B_HEADS), :].astype(BF16)
                acc[h:h + 1, :] += jnp.dot(wd, vh, preferred_element_type=F32)[h:h + 1]
        return carry

    lax.fori_loop(0, n_chunks, value_chunk, 0)

    for h in range(B_HEADS):
        oh = acc[h:h + 1, :]
        ms = jnp.mean(oh * oh, axis=-1, keepdims=True)
        o_ref[0, :, h * B_V_DIM:(h + 1) * B_V_DIM] = (
            oh * lax.rsqrt(ms + LN_EPS) * g_ref[...] * (1.0 - LAM_INIT)).astype(o_ref.dtype)


def _decode_attn(page_table, lam_rows, subln_g, q, k_new, v_new, cache_k, cache_v):
    n, d_b = q.shape
    n_pages = page_table.shape[1]
    page = cache_k.shape[2]
    chunk = math.gcd(n_pages, DECODE_CHUNK)
    q3, k3, v3 = (a.reshape(n, 1, d_b) for a in (q, k_new, v_new))
    per_seq = pl.BlockSpec((1, 1, d_b), lambda b, pt: (b, 0, 0))
    any_spec = pl.BlockSpec(memory_space=pl.ANY)
    out = pl.pallas_call(
        _decode_kernel,
        grid_spec=pltpu.PrefetchScalarGridSpec(
            num_scalar_prefetch=1,
            grid=(n,),
            in_specs=[pl.BlockSpec((4, B_HEAD_DIM), lambda b, pt: (0, 0)),
                      pl.BlockSpec((1, B_V_DIM), lambda b, pt: (0, 0)),
                      per_seq, per_seq, per_seq, any_spec, any_spec],
            out_specs=per_seq,
            scratch_shapes=[pltpu.VMEM((2, chunk) + cache_k.shape[1:], F32),
                            pltpu.VMEM((2, chunk) + cache_v.shape[1:], F32),
                            pltpu.SemaphoreType.DMA((2,)), pltpu.SemaphoreType.DMA((2,)),
                            pltpu.VMEM((n_pages, 2 * B_HEADS, page), F32), pltpu.VMEM((B_HEADS, B_V_DIM), F32)]),
        out_shape=jax.ShapeDtypeStruct((n, 1, d_b), BF16),
        compiler_params=pltpu.CompilerParams(dimension_semantics=("arbitrary",), vmem_limit_bytes=VMEM_LIMIT),
        name="decode_attn",
    )(page_table, lam_rows, subln_g, q3, k3, v3, cache_k, cache_v)
    return out.reshape(n, d_b)


def _tile_plan(counts, m_rows):
    padded = (counts + MOE_TILE - 1) // MOE_TILE * MOE_TILE
    pends = jnp.cumsum(padded)
    n_tiles = m_rows // MOE_TILE
    tile_expert = jnp.minimum(
        jnp.searchsorted(pends, jnp.arange(n_tiles, dtype=jnp.int32) * MOE_TILE, side='right'),
        N_EXPERTS - 1).astype(jnp.int32)
    n_used = (pends[-1:] // MOE_TILE).astype(jnp.int32)
    return (pends - padded).astype(jnp.int32), tile_expert, n_used


def kernel(x_prompt, x_sample, cache_k, cache_v, page_table, p_prompt, p_sample, w_in, ln_v_g, ln_v_b, w_spatial,
           b_spatial, lambda_q1, lambda_k1, lambda_q2, lambda_k2, subln_g, w_out, ln1_g, ln1_b, w_router,
           router_bias, w_exp_gate, w_exp_up, w_exp_down, w_sh_gate, w_sh_up, w_sh_down, ln2_g, ln2_b, w_ple,
           w_ple_gate, b_ple_gate):
    assert w_in.shape[0] == DEPTH and x_sample.shape[1] == 1
    n_seq, seq, d = x_prompt.shape
    n_dec = x_sample.shape[0]
    t = n_seq * seq
    d_a = ln_v_g.shape[1]
    d_b = B_HEADS * B_V_DIM
    row1 = lambda a: a.reshape(1, -1)

    lam_rows = jnp.concatenate([lambda_q1, lambda_k1, lambda_q2, lambda_k2], axis=0)
    lng, lnb = row1(ln_v_g[0]), row1(ln_v_b[0])
    b_sp_full = jnp.repeat(b_spatial[0].T, d_a // A_HEADS, axis=1)
    w00_row = row1(jnp.repeat(w_spatial[0, :, 0, 0], d_a // A_HEADS))
    b0_row = b_sp_full[0:1]
    w_router_t = w_router[0].T
    rbias_col = router_bias[0].reshape(N_EXPERTS, 1)

    xp = x_prompt.reshape(t, d)
    k_p, v_p, qb, kb, vt, a_p, gv_p = _proj_prompt(xp, w_in[0], lng, lnb, w_spatial[0], b_sp_full, seq)
    b_p = _attn_prompt(lam_rows, subln_g[0].reshape(B_V_DIM, 1), qb, kb, vt, n_seq, seq)
    mix = functools.partial(_mix_route, w_out=w_out[0], g=row1(ln1_g[0]), bb=row1(ln1_b[0]),
                            w_router_t=w_router_t, rbias_col=rbias_col)
    h1_p, eidx_p, gw_p, rank_p, counts_p = mix(a_p, b_p, xp, base_counts=jnp.zeros((N_EXPERTS, 1), F32))

    xs_ = x_sample.reshape(n_dec, d)
    k_s, v_s, q_s, a_s, gv_s = _proj_sample(xs_, w_in[0], lng, lnb, w00_row, b0_row)
    n_phys, page = cache_k.shape[1], cache_k.shape[2]
    ck = cache_k[0].transpose(0, 2, 3, 1).reshape(n_phys, 2 * B_HEADS * B_HEAD_DIM, page)
    cv = cache_v[0].reshape(n_phys, page * B_HEADS, B_V_DIM)
    b_s = _decode_attn(page_table, lam_rows, row1(subln_g[0]), q_s, k_s, v_s, ck, cv)
    h1_s, eidx_s, gw_s, rank_s, counts = mix(a_s, b_s, xs_, base_counts=counts_p)

    m_rows = -(-((t + n_dec) * TOP_K + N_EXPERTS * (MOE_TILE - 1)) // MOE_TILE) * MOE_TILE
    counts = counts.reshape(N_EXPERTS).astype(jnp.int32)
    pstart, tile_expert, n_used = _tile_plan(counts, m_rows)
    xs_sorted, rows_p = _dispatch(pstart, counts, eidx_p, rank_p, h1_p, None, m_rows)
    xs_sorted, rows_s = _dispatch(pstart, counts, eidx_s, rank_s, h1_s, xs_sorted, m_rows)
    ys = _experts(tile_expert, n_used, xs_sorted, w_exp_gate[0], w_exp_up[0], w_exp_down[0])

    fin = functools.partial(_final, ys=ys, w_sg=w_sh_gate[0], w_su=w_sh_up[0], w_sd=w_sh_down[0], g=row1(ln2_g[0]),
                            bb=row1(ln2_b[0]), w_pg=w_ple_gate[0], b_pg=row1(b_ple_gate[0]), w_p=w_ple[0])
    y_p = fin(rows_p, h1_p, gw_p.T, p_prompt[0].reshape(t, -1))
    y_s = fin(rows_s, h1_s, gw_s.T, p_sample[0].reshape(n_dec, -1))

    return (y_p.reshape(n_seq, seq, d), y_s.reshape(n_dec, 1, d),
            k_p.reshape(1, n_seq, seq, 2 * B_HEADS, B_HEAD_DIM), v_p.reshape(1, n_seq, seq, B_HEADS, B_V_DIM),
            gv_p.reshape(1, n_seq, CHUNK, d_a),
            k_s.reshape(1, n_dec, 1, 2 * B_HEADS, B_HEAD_DIM), v_s.reshape(1, n_dec, 1, B_HEADS, B_V_DIM),
            gv_s.reshape(1, n_dec, 1, d_a))
```

```python
import functools
import math

import jax
import jax.numpy as jnp
from jax import lax
from jax.experimental import pallas as pl
from jax.experimental.pallas import tpu as pltpu

F32 = jnp.float32
BF16 = jnp.bfloat16

A_HEADS = 8
CHUNK = 128
B_HEADS = 4
B_HEAD_DIM = 64
B_V_DIM = 128
N_EXPERTS = 256
N_GROUPS = 8
GROUP_SIZE = N_EXPERTS // N_GROUPS
TOPK_GROUPS = 4
TOP_K = 8
ROUTED_SCALE = 2.5
LN_EPS = 1e-5
DEPTH = 1
DEEPNORM_ALPHA = (2 * DEPTH) ** 0.25
LAM_INIT = 0.8 - 0.6 * math.exp(-0.3 * 0)

LANES = 128
SUBLANES = 8
ROW_WORDS = 4
ROW_TILE = 256
ATTN_TK = 512
MOE_TILE = 512
DECODE_CHUNK = 16
VMEM_LIMIT = 56 * 1024 * 1024

_NT = (((1,), (1,)), ((), ()))


def _const_spec(shape):
    nd = len(shape)
    return pl.BlockSpec(shape, lambda *_: (0,) * nd)


def _layer_norm(x, g, b):
    mu = jnp.mean(x, axis=-1, keepdims=True)
    xc = x - mu
    var = jnp.mean(xc * xc, axis=-1, keepdims=True)
    return xc * lax.rsqrt(var + LN_EPS) * g + b


def _gelu(x):
    return 0.5 * x * (1.0 + lax.erf(x * (2.0 ** -0.5)))


_ERFC_P = (2.326819970068386e-2, -1.387039388740657e-1, 3.687424674597105e-1, -5.824733027278666e-1,
           6.210004621745983e-1, -4.944515323274145e-1, 3.404879937665872e-1, -2.741127028184656e-1,
           5.638259427386472e-1)
_ERFC_R = (-1.047766399936249e+1, 1.297719955372516e+1, -7.495518717768503e+0, 2.921019019210786e+0,
           -1.015265279202700e+0, 4.218463358204948e-1, -2.820767439740514e-1, 5.641895067754075e-1)


def _horner(y, coeffs):
    r = jnp.full(y.shape, coeffs[0], F32)
    for c in coeffs[1:]:
        r = r * y + c
    return r


def _gelu_tail_exact(x):
    t = -x * (2.0 ** -0.5)
    y = jnp.abs(t)
    q = 1.0 / jnp.maximum(y, 1.0)
    q2 = q * q
    tail = jnp.exp(-t * t) * q * jnp.where(y < 2.0, _horner(q2, _ERFC_P), _horner(q2, _ERFC_R))
    tail = jnp.where(t < 0.0, 2.0 - tail, tail)
    return 0.5 * x * jnp.where(y < 1.0, 1.0 - lax.erf(t), tail)


def _tree_sum0(x):
    n = x.shape[0]
    while n > 1 and n % 2 == 0:
        n //= 2
        x = x[:n] + x[n:]
    return jnp.sum(x, axis=0)


def _lam(lam_ref):
    lv = lam_ref[...]
    s1 = jnp.sum(lv[0:1] * lv[1:2], axis=-1, keepdims=True)
    s2 = jnp.sum(lv[2:3] * lv[3:4], axis=-1, keepdims=True)
    return jnp.exp(s1) - jnp.exp(s2) + LAM_INIT


def _proj_kernel(x_ref, w_in_ref, lng_ref, lnb_ref, wsp_ref, bsp_ref,
                 k_ref, v_ref, qb_ref, kb_ref, vt_ref, a_ref, gv_ref,
                 wbf_ref, wcat_ref, *, tiles_per_seq):
    i = pl.program_id(0)
    tm = x_ref.shape[0]
    d_a = lng_ref.shape[1]

    @pl.when(i == 0)
    def _init():
        rows = w_in_ref.shape[0]
        step = 128

        def body(r, c):
            sl = pl.ds(pl.multiple_of(r * step, step), step)
            wbf_ref[sl, :] = w_in_ref[sl, :].astype(BF16)
            return c

        lax.fori_loop(0, rows // step, body, 0)
        row = lax.broadcasted_iota(jnp.int32, (CHUNK, CHUNK), 0)
        col = lax.broadcasted_iota(jnp.int32, (CHUNK, CHUNK), 1)
        for j in range(A_HEADS // 2):
            w0 = jnp.where(col <= row, wsp_ref[2 * j], 0.0)
            w1 = jnp.where(col <= row, wsp_ref[2 * j + 1], 0.0)
            wcat_ref[j] = jnp.concatenate([w0, w1], axis=1).astype(BF16)

    xb = x_ref[...].astype(BF16)

    def seg(s):
        return jnp.dot(xb, wbf_ref[:, s * d_a:(s + 1) * d_a], preferred_element_type=F32)

    k = seg(3)
    k_ref[...] = k
    kb_ref[...] = k.astype(BF16)
    v = seg(4)
    v_ref[...] = v
    vt_ref[0] = v.T.astype(BF16)
    qb_ref[...] = (seg(2) * (B_HEAD_DIM ** -0.5)).astype(BF16)

    va = _layer_norm(_gelu(seg(1)), lng_ref[...], lnb_ref[...])

    @pl.when(i % tiles_per_seq == tiles_per_seq - 1)
    def _gv():
        gv_ref[0] = va[tm - CHUNK:, :]

    u = _gelu(seg(0))
    vab = va.astype(BF16)
    lane = lax.broadcasted_iota(jnp.int32, (CHUNK, LANES), 1)
    zero = jnp.zeros((CHUNK, LANES), BF16)
    for c in range(tm // CHUNK):
        r0 = c * CHUNK
        for j in range(A_HEADS // 2):
            vp = vab[r0:r0 + CHUNK, j * LANES:(j + 1) * LANES]
            stacked = jnp.concatenate([jnp.where(lane < 64, vp, zero), jnp.where(lane >= 64, vp, zero)], axis=0)
            sg = jnp.dot(wcat_ref[j], stacked, preferred_element_type=F32) + bsp_ref[:, j * LANES:(j + 1) * LANES]
            a_ref[r0:r0 + CHUNK, j * LANES:(j + 1) * LANES] = (
                u[r0:r0 + CHUNK, j * LANES:(j + 1) * LANES] * sg).astype(BF16)


def _proj_prompt(x, w_in, lng, lnb, w_sp, b_sp_full, seq):
    t, d = x.shape
    d_in = w_in.shape[1]
    d_a = lng.shape[1]
    tm = min(ROW_TILE, seq)
    tiles_per_seq = seq // tm
    n_seq = t // seq
    row = lambda w: pl.BlockSpec((tm, w), lambda i: (i, 0))
    out_shape = (
        jax.ShapeDtypeStruct((t, d_a), F32), jax.ShapeDtypeStruct((t, d_a), F32),
        jax.ShapeDtypeStruct((t, d_a), BF16), jax.ShapeDtypeStruct((t, d_a), BF16),
        jax.ShapeDtypeStruct((t // tm, d_a, tm), BF16), jax.ShapeDtypeStruct((t, d_a), BF16),
        jax.ShapeDtypeStruct((n_seq, CHUNK, d_a), F32),
    )
    return pl.pallas_call(
        functools.partial(_proj_kernel, tiles_per_seq=tiles_per_seq),
        grid=(t // tm,),
        in_specs=[row(d), _const_spec((d, d_in)), _const_spec((1, d_a)), _const_spec((1, d_a)),
                  _const_spec((A_HEADS, CHUNK, CHUNK)), _const_spec((CHUNK, d_a))],
        out_specs=(row(d_a), row(d_a), row(d_a), row(d_a), pl.BlockSpec((1, d_a, tm), lambda i: (i, 0, 0)), row(d_a),
                   pl.BlockSpec((1, CHUNK, d_a), lambda i: (i // tiles_per_seq, 0, 0))),
        out_shape=out_shape,
        scratch_shapes=[pltpu.VMEM((d, d_in), BF16), pltpu.VMEM((A_HEADS // 2, CHUNK, 2 * CHUNK), BF16)],
        compiler_params=pltpu.CompilerParams(dimension_semantics=("arbitrary",), vmem_limit_bytes=VMEM_LIMIT),
        name="proj_prompt",
    )(x, w_in, lng, lnb, w_sp, b_sp_full)


def _attn_kernel(lam_ref, gcol_ref, q_ref, k_ref, vt_ref, o_ref, acc, m, l):
    i = pl.program_id(2)
    tq = q_ref.shape[0]
    per_wide = max(1, min(ATTN_TK // tq, vt_ref.shape[1]))
    qb = q_ref[...]
    lane = lax.broadcasted_iota(jnp.int32, qb.shape, 1)
    zero = jnp.zeros(qb.shape, BF16)
    q2 = jnp.concatenate([jnp.where(lane < B_HEAD_DIM, qb, zero), jnp.where(lane >= B_HEAD_DIM, qb, zero)], axis=0)

    acc[...] = jnp.zeros(acc.shape, F32)
    m[...] = jnp.full(m.shape, -jnp.inf, F32)
    l[...] = jnp.zeros(l.shape, F32)

    def block(c0, n_ch, causal):
        width = n_ch * tq
        kb = k_ref[pl.ds(pl.multiple_of(c0 * tq, tq), width), :]
        s = lax.dot_general(kb, q2, _NT, preferred_element_type=F32)
        if causal:
            key = lax.broadcasted_iota(jnp.int32, (width, 2 * tq), 0)
            qry = lax.broadcasted_iota(jnp.int32, (width, 2 * tq), 1) % tq
            s = jnp.where(key <= qry, s, -jnp.inf)
        m_old = m[...]
        m_new = jnp.maximum(m_old, jnp.max(s, axis=0, keepdims=True))
        alpha = jnp.exp(m_old - m_new)
        p = jnp.exp(s - m_new)
        l[...] = alpha * l[...] + jnp.sum(p, axis=0, keepdims=True)
        pb = p.astype(BF16)
        pv = jnp.dot(vt_ref[0, c0], pb[:tq], preferred_element_type=F32)
        for ch in range(1, n_ch):
            pv = pv + jnp.dot(vt_ref[0, c0 + ch], pb[ch * tq:(ch + 1) * tq], preferred_element_type=F32)
        acc[...] = alpha * acc[...] + pv
        m[...] = m_new

    n_wide = i // per_wide

    def wide_body(j, c):
        block(j * per_wide, per_wide, False)
        return c

    lax.fori_loop(0, n_wide, wide_body, 0)
    for r in range(per_wide - 1):
        @pl.when(n_wide * per_wide + r < i)
        def _():
            block(n_wide * per_wide + r, 1, False)
    block(i, 1, True)

    lam = _lam(lam_ref)
    w = acc[...] / l[...]
    o = w[:, :tq] - lam * w[:, tq:]
    ms = jnp.mean(o * o, axis=0, keepdims=True)
    o = o * lax.rsqrt(ms + LN_EPS) * gcol_ref[...] * (1.0 - LAM_INIT)
    o_ref[...] = o.T.astype(o_ref.dtype)


def _attn_prompt(lam_rows, subln_g_col, qb, kb, vt, n_seq, seq):
    t, d_b = qb.shape
    tq = vt.shape[2]
    nq = seq // tq
    vt4 = vt.reshape(n_seq, nq, d_b, tq)
    return pl.pallas_call(
        _attn_kernel,
        grid=(n_seq, B_HEADS, nq),
        in_specs=[_const_spec((4, B_HEAD_DIM)), _const_spec((B_V_DIM, 1)),
                  pl.BlockSpec((tq, LANES), lambda b, h, i: (b * nq + i, h)),
                  pl.BlockSpec((seq, LANES), lambda b, h, i: (b, h)),
                  pl.BlockSpec((1, nq, B_V_DIM, tq), lambda b, h, i: (b, 0, h, 0))],
        out_specs=pl.BlockSpec((tq, LANES), lambda b, h, i: (b * nq + i, h)),
        out_shape=jax.ShapeDtypeStruct((t, d_b), BF16),
        scratch_shapes=[pltpu.VMEM((B_V_DIM, 2 * tq), F32), pltpu.VMEM((1, 2 * tq), F32),
                        pltpu.VMEM((1, 2 * tq), F32)],
        compiler_params=pltpu.CompilerParams(dimension_semantics=("arbitrary", "arbitrary", "arbitrary"),
                                             vmem_limit_bytes=VMEM_LIMIT),
        name="attn_prompt",
    )(lam_rows, subln_g_col, qb, kb, vt4)


def _route(logits_t, bias_col):
    e, n = logits_t.shape
    scores = jax.nn.sigmoid(logits_t)
    biased = scores + bias_col
    ninf = jnp.float32(-jnp.inf)
    gs = []
    sub = lax.broadcasted_iota(jnp.int32, (GROUP_SIZE, n), 0)
    for g in range(N_GROUPS):
        blk = biased[g * GROUP_SIZE:(g + 1) * GROUP_SIZE]
        t1 = jnp.max(blk, axis=0, keepdims=True)
        i1 = jnp.min(jnp.where(blk == t1, sub, GROUP_SIZE), axis=0, keepdims=True)
        t2 = jnp.max(jnp.where(sub == i1, ninf, blk), axis=0, keepdims=True)
        gs.append(t1 + t2)
    keep = []
    for g in range(N_GROUPS):
        cnt = jnp.zeros((1, n), jnp.int32)
        for o in range(N_GROUPS):
            if o == g:
                continue
            beats = (gs[o] >= gs[g]) if o < g else (gs[o] > gs[g])
            cnt = cnt + beats.astype(jnp.int32)
        keep.append(cnt < TOPK_GROUPS)
    masked = jnp.concatenate(
        [jnp.where(keep[g], biased[g * GROUP_SIZE:(g + 1) * GROUP_SIZE], ninf) for g in range(N_GROUPS)], axis=0)
    rid = lax.broadcasted_iota(jnp.int32, (e, n), 0)
    idxs, vals, hits = [], [], []
    for _ in range(TOP_K):
        top = jnp.max(masked, axis=0, keepdims=True)
        idx = jnp.min(jnp.where(masked == top, rid, e), axis=0, keepdims=True)
        hit = rid == idx
        vals.append(jnp.sum(jnp.where(hit, scores, 0.0), axis=0, keepdims=True))
        idxs.append(idx)
        hits.append(hit)
        masked = jnp.where(hit, ninf, masked)
    eidx = jnp.concatenate(idxs, axis=0)
    gw = jnp.concatenate(vals, axis=0)
    gw = gw / jnp.sum(gw, axis=0, keepdims=True) * ROUTED_SCALE
    return eidx, gw, hits


def _mix_kernel(a_ref, b_ref, x_ref, wout_ref, g_ref, bb_ref, wrt_ref, rb_ref, base_ref,
                h1_ref, eidx_ref, gw_ref, rank_ref, cnt_ref, wo_s, wr_s, before_s, seen_s):
    i = pl.program_id(0)
    d_a = a_ref.shape[1]
    tm = x_ref.shape[0]

    @pl.when(i == 0)
    def _init():
        wo_s[...] = wout_ref[...].astype(BF16)
        wr_s[...] = wrt_ref[...].astype(BF16)
        r = lax.broadcasted_iota(jnp.int32, (tm, tm), 0)
        c = lax.broadcasted_iota(jnp.int32, (tm, tm), 1)
        before_s[...] = jnp.where(r < c, 1.0, 0.0).astype(BF16)
        seen_s[...] = base_ref[...]

    dot = functools.partial(jnp.dot, preferred_element_type=F32)
    mix = dot(a_ref[...], wo_s[:d_a, :]) + dot(b_ref[...], wo_s[d_a:, :])
    h1 = _layer_norm(DEEPNORM_ALPHA * x_ref[...] + mix, g_ref[...], bb_ref[...])
    h1_ref[...] = h1
    logits_t = lax.dot_general(wr_s[...], h1.astype(BF16), _NT, preferred_element_type=F32)
    eidx, gw, hits = _route(logits_t, rb_ref[...])
    eidx_ref[...] = eidx
    gw_ref[...] = gw
    onehot = jnp.zeros(logits_t.shape, F32)
    for hit in hits:
        onehot = jnp.where(hit, 1.0, onehot)
    earlier = dot(onehot.astype(BF16), before_s[...]) + seen_s[...]
    rank_ref[...] = jnp.concatenate(
        [jnp.sum(jnp.where(hit, earlier, 0.0), axis=0, keepdims=True) for hit in hits], axis=0).astype(jnp.int32)
    seen = seen_s[...] + jnp.sum(onehot, axis=1, keepdims=True)
    seen_s[...] = seen
    cnt_ref[...] = seen


def _mix_route(a, b, x, w_out, g, bb, w_router_t, rbias_col, base_counts):
    t, d = x.shape
    d_a = a.shape[1]
    tm = min(ROW_TILE, t)
    row = lambda w: pl.BlockSpec((tm, w), lambda i: (i, 0))
    colb = pl.BlockSpec((TOP_K, tm), lambda i: (0, i))
    return pl.pallas_call(
        _mix_kernel,
        grid=(t // tm,),
        in_specs=[row(d_a), row(d_a), row(d), _const_spec((d, d)), _const_spec((1, d)), _const_spec((1, d)),
                  _const_spec((N_EXPERTS, d)), _const_spec((N_EXPERTS, 1)), _const_spec((N_EXPERTS, 1))],
        out_specs=(row(d), colb, colb, colb, _const_spec((N_EXPERTS, 1))),
        out_shape=(jax.ShapeDtypeStruct((t, d), F32),
                   jax.ShapeDtypeStruct((TOP_K, t), jnp.int32), jax.ShapeDtypeStruct((TOP_K, t), F32),
                   jax.ShapeDtypeStruct((TOP_K, t), jnp.int32), jax.ShapeDtypeStruct((N_EXPERTS, 1), F32)),
        scratch_shapes=[pltpu.VMEM((d, d), BF16), pltpu.VMEM((N_EXPERTS, d), BF16), pltpu.VMEM((tm, tm), BF16),
                        pltpu.VMEM((N_EXPERTS, 1), F32)],
        compiler_params=pltpu.CompilerParams(dimension_semantics=("arbitrary",), vmem_limit_bytes=VMEM_LIMIT),
        name="mix_route",
    )(a, b, x, w_out, g, bb, w_router_t, rbias_col, base_counts)


def _pack_bf16(x):
    w = x.shape[1] // 2
    u = pltpu.bitcast(x.astype(BF16).astype(F32), jnp.uint32)
    return (u[:, :w] >> 16) | u[:, w:]


def _unpack_bf16(u):
    lo = pltpu.bitcast(u << 16, F32)
    hi = pltpu.bitcast(u & jnp.uint32(0xFFFF0000), F32)
    return jnp.concatenate([lo, hi], axis=1)


def _to_row_tiles(x, ref):
    n, pieces = x.shape[0], x.shape[1] // LANES
    for c in range(pieces):
        ref[pl.ds(c, n, stride=pieces), :] = x[:, c * LANES:(c + 1) * LANES]


def _from_row_tiles(ref, n):
    pieces = ref.shape[0] // n
    return jnp.concatenate([ref[pl.ds(c, n, stride=pieces), :] for c in range(pieces)], axis=1)


def _tile_copy(src, src_row, dst, dst_row, sem):
    def tile(ref, r):
        return ref.at[pl.ds(pl.multiple_of(r * ROW_WORDS, ROW_WORDS), ROW_WORDS), :]
    return pltpu.make_async_copy(tile(src, src_row), tile(dst, dst_row), sem)


def _dispatch_rows(pstart_ref, eidx_ref, rank_ref, h1_ref, xs_ref, row_ref, stage, sems):
    i = pl.program_id(0)
    tm = h1_ref.shape[0]
    slot = i % 2
    _to_row_tiles(_pack_bf16(h1_ref[...]), stage.at[slot])

    def start_token(j, c):
        for k in range(TOP_K):
            row = pstart_ref[eidx_ref[k, j]] + rank_ref[k, j]
            row_ref[k, j] = row
            _tile_copy(stage.at[slot], j, xs_ref, row, sems.at[slot]).start(priority=k % 2)
        return c

    def wait_all(s):
        def wait_token(j, c):
            for k in range(TOP_K):
                _tile_copy(stage.at[s], 0, xs_ref, 0, sems.at[s]).wait()
            return c
        lax.fori_loop(0, tm, wait_token, 0)

    lax.fori_loop(0, tm, start_token, 0)

    @pl.when(i > 0)
    def _():
        wait_all(1 - slot)

    @pl.when(i == pl.num_programs(0) - 1)
    def _():
        wait_all(slot)


def _dispatch_first_kernel(pstart_ref, cnt_ref, eidx_ref, rank_ref, h1_ref, xs_ref, row_ref, stage, zero_s, sems,
                           zsem):
    @pl.when(pl.program_id(0) == 0)
    def _pad():
        zero_s[...] = jnp.zeros(zero_s.shape, zero_s.dtype)

        def per_expert(action):
            def body(e, c):
                lo = pstart_ref[e] + cnt_ref[e]
                hi = pstart_ref[e] + (cnt_ref[e] + MOE_TILE - 1) // MOE_TILE * MOE_TILE

                def one(r, cc):
                    action(_tile_copy(zero_s, 0, xs_ref, r, zsem))
                    return cc

                lax.fori_loop(lo, hi, one, 0)
                return c
            return body

        lax.fori_loop(0, N_EXPERTS, per_expert(lambda cp: cp.start()), 0)
        lax.fori_loop(0, N_EXPERTS, per_expert(lambda cp: cp.wait()), 0)

    _dispatch_rows(pstart_ref, eidx_ref, rank_ref, h1_ref, xs_ref, row_ref, stage, sems)


def _dispatch_more_kernel(pstart_ref, cnt_ref, eidx_ref, rank_ref, h1_ref, xs_in_ref, xs_ref, row_ref, stage, sems):
    del cnt_ref, xs_in_ref
    _dispatch_rows(pstart_ref, eidx_ref, rank_ref, h1_ref, xs_ref, row_ref, stage, sems)


def _dispatch(pstart, counts, eidx, rank, h1, xs, m_rows):
    t, d = h1.shape
    assert d == 2 * ROW_WORDS * LANES
    tm = min(ROW_TILE, t)
    first = xs is None
    smem_blk = pl.BlockSpec((TOP_K, tm), lambda i, *_: (0, i), memory_space=pltpu.SMEM)
    any_spec = pl.BlockSpec(memory_space=pl.ANY)
    in_specs = [smem_blk, smem_blk, pl.BlockSpec((tm, d), lambda i, *_: (i, 0))]
    stage = [pltpu.VMEM((2, tm * ROW_WORDS, LANES), jnp.uint32)]
    scratch = [pltpu.SemaphoreType.DMA((2,))]
    return pl.pallas_call(
        _dispatch_first_kernel if first else _dispatch_more_kernel,
        grid_spec=pltpu.PrefetchScalarGridSpec(
            num_scalar_prefetch=2,
            grid=(t // tm,),
            in_specs=in_specs if first else in_specs + [any_spec],
            out_specs=(any_spec, smem_blk),
            scratch_shapes=(stage + [pltpu.VMEM((SUBLANES, LANES), jnp.uint32)] + scratch
                            + [pltpu.SemaphoreType.DMA(())] if first else stage + scratch)),
        out_shape=(jax.ShapeDtypeStruct((m_rows * ROW_WORDS, LANES), jnp.uint32),
                   jax.ShapeDtypeStruct((TOP_K, t), jnp.int32)),
        input_output_aliases={} if first else {5: 0},
        compiler_params=pltpu.CompilerParams(dimension_semantics=("arbitrary",), vmem_limit_bytes=VMEM_LIMIT),
        name="dispatch_first" if first else "dispatch_more",
    )(*((pstart, counts, eidx, rank, h1) if first else (pstart, counts, eidx, rank, h1, xs)))


def _experts_kernel(te_ref, nu_ref, x_ref, wg_ref, wu_ref, wd_ref, y_ref, wgu_s, wd_s):
    t = pl.program_id(0)
    prev = te_ref[jnp.maximum(t - 1, 0)]
    d_e = wg_ref.shape[2]

    @pl.when((t < nu_ref[0]) & ((t == 0) | (te_ref[t] != prev)))
    def _cast():
        wgu_s[:, :d_e] = wg_ref[0].astype(BF16)
        wgu_s[:, d_e:] = wu_ref[0].astype(BF16)
        wd_s[...] = wd_ref[0].astype(BF16)

    @pl.when(t < nu_ref[0])
    def _compute():
        x = _unpack_bf16(_from_row_tiles(x_ref, MOE_TILE)).astype(BF16)
        gu = jnp.dot(x, wgu_s[...], preferred_element_type=F32)
        hid = jax.nn.silu(gu[:, :d_e]) * gu[:, d_e:]
        _to_row_tiles(_pack_bf16(jnp.dot(hid.astype(BF16), wd_s[...], preferred_element_type=F32)), y_ref)


def _experts(tile_expert, n_used, xs, w_g, w_u, w_d):
    d, d_e = w_g.shape[1], w_g.shape[2]
    n_tiles = xs.shape[0] // (MOE_TILE * ROW_WORDS)

    def rows(t, te, nu):
        return (jnp.minimum(t, nu[0] - 1), 0)

    def wmap(t, te, nu):
        return (te[jnp.minimum(t, nu[0] - 1)], 0, 0)

    return pl.pallas_call(
        _experts_kernel,
        grid_spec=pltpu.PrefetchScalarGridSpec(
            num_scalar_prefetch=2,
            grid=(n_tiles,),
            in_specs=[pl.BlockSpec((MOE_TILE * ROW_WORDS, LANES), rows),
                      pl.BlockSpec((1, d, d_e), wmap), pl.BlockSpec((1, d, d_e), wmap),
                      pl.BlockSpec((1, d_e, d), wmap)],
            out_specs=pl.BlockSpec((MOE_TILE * ROW_WORDS, LANES), rows),
            scratch_shapes=[pltpu.VMEM((d, 2 * d_e), BF16), pltpu.VMEM((d_e, d), BF16)]),
        out_shape=jax.ShapeDtypeStruct(xs.shape, jnp.uint32),
        compiler_params=pltpu.CompilerParams(dimension_semantics=("arbitrary",), vmem_limit_bytes=VMEM_LIMIT),
        name="experts",
    )(tile_expert, n_used, xs, w_g, w_u, w_d)


def _final_kernel(rows_ref, rows_next_ref, h1_ref, gw_ref, p_ref, ys_ref, wsg_ref, wsu_ref, wsd_ref, g_ref, bb_ref,
                  wpg_ref, bpg_ref, wp_ref, y_ref, wgu_s, wd_s, wpg_s, wp_s, ybuf, sems):
    i = pl.program_id(0)
    n = pl.num_programs(0)
    d_s = wsg_ref.shape[1]
    tm = h1_ref.shape[0]

    def gather(idx_ref, slot, action):
        def body(j, c):
            for k in range(TOP_K):
                src_row = idx_ref[k, j] if action == "start" else 0
                cp = _tile_copy(ys_ref, src_row, ybuf.at[slot, k], j, sems.at[slot])
                cp.start(priority=k % 2) if action == "start" else cp.wait()
            return c
        lax.fori_loop(0, tm, body, 0)

    @pl.when(i == 0)
    def _init():
        gather(rows_ref, 0, "start")
        wgu_s[:, :d_s] = wsg_ref[...].astype(BF16)
        wgu_s[:, d_s:] = wsu_ref[...].astype(BF16)
        wd_s[...] = wsd_ref[...].astype(BF16)
        wpg_s[...] = wpg_ref[...].astype(BF16)
        wp_s[...] = wp_ref[...].astype(BF16)

    slot = i % 2

    @pl.when(i + 1 < n)
    def _prefetch():
        gather(rows_next_ref, 1 - slot, "start")

    dot = functools.partial(jnp.dot, preferred_element_type=F32)
    h1 = h1_ref[...]
    gu = dot(h1.astype(BF16), wgu_s[...])
    hid = jax.nn.silu(gu[:, :d_s]) * gu[:, d_s:]
    shared = dot(hid.astype(BF16), wd_s[...])

    gather(rows_ref, slot, "wait")
    gwv = gw_ref[...]
    routed = gwv[:, 0:1] * _unpack_bf16(_from_row_tiles(ybuf.at[slot, 0], tm))
    for k in range(1, TOP_K):
        routed = routed + gwv[:, k:k + 1] * _unpack_bf16(_from_row_tiles(ybuf.at[slot, k], tm))

    h2 = _layer_norm(DEEPNORM_ALPHA * h1 + (routed + shared), g_ref[...], bb_ref[...])
    gate = jax.nn.sigmoid(dot(h2.astype(BF16), wpg_s[...]) + bpg_ref[...])
    y_ref[...] = h2 + gate * dot(p_ref[...].astype(BF16), wp_s[...])


def _final(row_of, h1, gw, p, ys, w_sg, w_su, w_sd, g, bb, w_pg, b_pg, w_p):
    t, d = h1.shape
    d_s = w_sg.shape[1]
    d_p = p.shape[1]
    tm = min(ROW_TILE, t)
    n = t // tm
    row = lambda w: pl.BlockSpec((tm, w), lambda i: (i, 0))
    return pl.pallas_call(
        _final_kernel,
        grid=(n,),
        in_specs=[pl.BlockSpec((TOP_K, tm), lambda i: (0, i), memory_space=pltpu.SMEM),
                  pl.BlockSpec((TOP_K, tm), lambda i: (0, jnp.minimum(i + 1, n - 1)), memory_space=pltpu.SMEM),
                  row(d), row(TOP_K), row(d_p), pl.BlockSpec(memory_space=pl.ANY),
                  _const_spec((d, d_s)), _const_spec((d, d_s)), _const_spec((d_s, d)),
                  _const_spec((1, d)), _const_spec((1, d)), _const_spec((d, d)), _const_spec((1, d)),
                  _const_spec((d_p, d))],
        out_specs=row(d),
        out_shape=jax.ShapeDtypeStruct((t, d), F32),
        scratch_shapes=[pltpu.VMEM((d, 2 * d_s), BF16), pltpu.VMEM((d_s, d), BF16), pltpu.VMEM((d, d), BF16),
                        pltpu.VMEM((d_p, d), BF16), pltpu.VMEM((2, TOP_K, tm * ROW_WORDS, LANES), jnp.uint32),
                        pltpu.SemaphoreType.DMA((2,))],
        compiler_params=pltpu.CompilerParams(dimension_semantics=("arbitrary",), vmem_limit_bytes=VMEM_LIMIT),
        name="final",
    )(row_of, row_of, h1, gw, p, ys, w_sg, w_su, w_sd, g, bb, w_pg, b_pg, w_p)


def _proj_sample_kernel(x_ref, w_in_ref, lng_ref, lnb_ref, w00_ref, b0_ref,
                        k_ref, v_ref, q_ref, a_ref, gv_ref):
    d_a = lng_ref.shape[1]
    xb = x_ref[...].astype(BF16)

    def seg(s):
        return jnp.dot(xb, w_in_ref[:, s * d_a:(s + 1) * d_a].astype(BF16), preferred_element_type=F32)

    k_ref[...] = seg(3)
    v_ref[...] = seg(4)
    q_ref[...] = seg(2) * (B_HEAD_DIM ** -0.5)
    va = _layer_norm(_gelu_tail_exact(seg(1)), lng_ref[...], lnb_ref[...])
    gv_ref[...] = va
    sg = va * w00_ref[...] + b0_ref[...]
    a_ref[...] = (_gelu_tail_exact(seg(0)) * sg).astype(a_ref.dtype)


def _proj_sample(x, w_in, lng, lnb, w00_row, b0_row):
    n, d = x.shape
    d_in = w_in.shape[1]
    d_a = lng.shape[1]
    full = lambda w: _const_spec((n, w))
    f32 = jax.ShapeDtypeStruct((n, d_a), F32)
    return pl.pallas_call(
        _proj_sample_kernel,
        grid=(1,),
        in_specs=[full(d), _const_spec((d, d_in)), _const_spec((1, d_a)), _const_spec((1, d_a)),
                  _const_spec((1, d_a)), _const_spec((1, d_a))],
        out_specs=(full(d_a),) * 5,
        out_shape=(f32, f32, f32, jax.ShapeDtypeStruct((n, d_a), BF16), f32),
        compiler_params=pltpu.CompilerParams(dimension_semantics=("arbitrary",), vmem_limit_bytes=VMEM_LIMIT),
        name="proj_sample",
    )(x, w_in, lng, lnb, w00_row, b0_row)


def _decode_kernel(pt_ref, lam_ref, g_ref, q_ref, kn_ref, vn_ref, kt_ref, vr_ref, o_ref,
                   kbuf, vbuf, ksem, vsem, s_scr, acc):
    b = pl.program_id(0)
    n_seq = pl.num_programs(0)
    n_maps = 2 * B_HEADS
    d_b = q_ref.shape[2]
    chunk = kbuf.shape[1]
    n_chunks = s_scr.shape[0] // chunk
    page = s_scr.shape[2]

    def chunk_copies(cache_ref, buf, sem, seq, c, slot):
        return [pltpu.make_async_copy(cache_ref.at[pt_ref[seq, c * chunk + j]], buf.at[slot, j], sem.at[slot])
                for j in range(chunk)]

    def start(copies):
        for cp in copies:
            cp.start()

    def wait(copies):
        for cp in copies:
            cp.wait()

    @pl.when(b == 0)
    def _first_chunk():
        start(chunk_copies(kt_ref, kbuf, ksem, 0, 0, 0))

    lane = lax.broadcasted_iota(jnp.int32, (n_maps, d_b), 1)
    mrow = lax.broadcasted_iota(jnp.int32, (n_maps, d_b), 0)
    own_map = jnp.where(mrow < B_HEADS, 2 * mrow, 2 * (mrow - B_HEADS) + 1)
    qrow = jnp.where(lane // B_HEAD_DIM == own_map, jnp.broadcast_to(q_ref[0], (n_maps, d_b)), 0.0).astype(BF16)

    def score_chunk(c, carry):
        slot = c % 2

        @pl.when(c + 1 < n_chunks)
        def _():
            start(chunk_copies(kt_ref, kbuf, ksem, b, c + 1, 1 - slot))

        @pl.when(c + 1 == n_chunks)
        def _():
            start(chunk_copies(vr_ref, vbuf, vsem, b, 0, 0))

        wait(chunk_copies(kt_ref, kbuf, ksem, b, c, slot))
        for j in range(chunk):
            s_scr[c * chunk + j] = jnp.dot(qrow, kbuf[slot, j].astype(BF16), preferred_element_type=F32)
        return carry

    lax.fori_loop(0, n_chunks, score_chunk, 0)

    lam = _lam(lam_ref)

    def diff_weights(w):
        top = w[:B_HEADS] - lam * w[B_HEADS:]
        return jnp.concatenate([top, jnp.zeros_like(top)], axis=0).astype(BF16)

    kn = kn_ref[0].astype(BF16).astype(F32)
    snew = jnp.sum(qrow.astype(F32) * kn, axis=-1, keepdims=True)
    s_all = s_scr[...]
    m = jnp.maximum(jnp.max(jnp.max(s_all, axis=0), axis=-1, keepdims=True), snew)
    e_new = jnp.exp(snew - m)
    z = jnp.sum(_tree_sum0(jnp.exp(s_all - m)), axis=-1, keepdims=True) + e_new
    wd_new = diff_weights(e_new / z).astype(F32)
    vn = vn_ref[0].astype(BF16).astype(F32)
    for h in range(B_HEADS):
        acc[h:h + 1, :] = wd_new[h:h + 1] * vn[:, h * B_V_DIM:(h + 1) * B_V_DIM]

    def value_chunk(c, carry):
        slot = c % 2

        @pl.when(c + 1 < n_chunks)
        def _():
            start(chunk_copies(vr_ref, vbuf, vsem, b, c + 1, 1 - slot))

        @pl.when((c + 1 == n_chunks) & (b + 1 < n_seq))
        def _():
            start(chunk_copies(kt_ref, kbuf, ksem, b + 1, 0, 0))

        wait(chunk_copies(vr_ref, vbuf, vsem, b, c, slot))
        for j in range(chunk):
            wd = diff_weights(jnp.exp(s_scr[c * chunk + j] - m) / z)
            for h in range(B_HEADS):
                vh = vbuf[slot, j, pl.ds(h, page, stride=B_HEADS), :].astype(BF16)
                acc[h:h + 1, :] += jnp.dot(wd, vh, preferred_element_type=F32)[h:h + 1]
        return carry

    lax.fori_loop(0, n_chunks, value_chunk, 0)

    for h in range(B_HEADS):
        oh = acc[h:h + 1, :]
        ms = jnp.mean(oh * oh, axis=-1, keepdims=True)
        o_ref[0, :, h * B_V_DIM:(h + 1) * B_V_DIM] = (
            oh * lax.rsqrt(ms + LN_EPS) * g_ref[...] * (1.0 - LAM_INIT)).astype(o_ref.dtype)


def _decode_attn(page_table, lam_rows, subln_g, q, k_new, v_new, cache_k, cache_v):
    n, d_b = q.shape
    n_pages = page_table.shape[1]
    page = cache_k.shape[2]
    chunk = math.gcd(n_pages, DECODE_CHUNK)
    q3, k3, v3 = (a.reshape(n, 1, d_b) for a in (q, k_new, v_new))
    per_seq = pl.BlockSpec((1, 1, d_b), lambda b, pt: (b, 0, 0))
    any_spec = pl.BlockSpec(memory_space=pl.ANY)
    out = pl.pallas_call(
        _decode_kernel,
        grid_spec=pltpu.PrefetchScalarGridSpec(
            num_scalar_prefetch=1,
            grid=(n,),
            in_specs=[pl.BlockSpec((4, B_HEAD_DIM), lambda b, pt: (0, 0)),
                      pl.BlockSpec((1, B_V_DIM), lambda b, pt: (0, 0)),
                      per_seq, per_seq, per_seq, any_spec, any_spec],
            out_specs=per_seq,
            scratch_shapes=[pltpu.VMEM((2, chunk) + cache_k.shape[1:], F32),
                            pltpu.VMEM((2, chunk) + cache_v.shape[1:], F32),
                            pltpu.SemaphoreType.DMA((2,)), pltpu.SemaphoreType.DMA((2,)),
                            pltpu.VMEM((n_pages, 2 * B_HEADS, page), F32), pltpu.VMEM((B_HEADS, B_V_DIM), F32)]),
        out_shape=jax.ShapeDtypeStruct((n, 1, d_b), BF16),
        compiler_params=pltpu.CompilerParams(dimension_semantics=("arbitrary",), vmem_limit_bytes=VMEM_LIMIT),
        name="decode_attn",
    )(page_table, lam_rows, subln_g, q3, k3, v3, cache_k, cache_v)
    return out.reshape(n, d_b)


def _tile_plan(counts, m_rows):
    padded = (counts + MOE_TILE - 1) // MOE_TILE * MOE_TILE
    pends = jnp.cumsum(padded)
    n_tiles = m_rows // MOE_TILE
    tile_expert = jnp.minimum(
        jnp.searchsorted(pends, jnp.arange(n_tiles, dtype=jnp.int32) * MOE_TILE, side='right'),
        N_EXPERTS - 1).astype(jnp.int32)
    n_used = (pends[-1:] // MOE_TILE).astype(jnp.int32)
    return (pends - padded).astype(jnp.int32), tile_expert, n_used


def kernel(x_prompt, x_sample, cache_k, cache_v, page_table, p_prompt, p_sample, w_in, ln_v_g, ln_v_b, w_spatial,
           b_spatial, lambda_q1, lambda_k1, lambda_q2, lambda_k2, subln_g, w_out, ln1_g, ln1_b, w_router,
           router_bias, w_exp_gate, w_exp_up, w_exp_down, w_sh_gate, w_sh_up, w_sh_down, ln2_g, ln2_b, w_ple,
           w_ple_gate, b_ple_gate):
    assert w_in.shape[0] == DEPTH and x_sample.shape[1] == 1
    n_seq, seq, d = x_prompt.shape
    n_dec = x_sample.shape[0]
    t = n_seq * seq
    d_a = ln_v_g.shape[1]
    d_b = B_HEADS * B_V_DIM
    row1 = lambda a: a.reshape(1, -1)

    lam_rows = jnp.concatenate([lambda_q1, lambda_k1, lambda_q2, lambda_k2], axis=0)
    lng, lnb = row1(ln_v_g[0]), row1(ln_v_b[0])
    b_sp_full = jnp.repeat(b_spatial[0].T, d_a // A_HEADS, axis=1)
    w00_row = row1(jnp.repeat(w_spatial[0, :, 0, 0], d_a // A_HEADS))
    b0_row = b_sp_full[0:1]
    w_router_t = w_router[0].T
    rbias_col = router_bias[0].reshape(N_EXPERTS, 1)

    xp = x_prompt.reshape(t, d)
    k_p, v_p, qb, kb, vt, a_p, gv_p = _proj_prompt(xp, w_in[0], lng, lnb, w_spatial[0], b_sp_full, seq)
    b_p = _attn_prompt(lam_rows, subln_g[0].reshape(B_V_DIM, 1), qb, kb, vt, n_seq, seq)
    mix = functools.partial(_mix_route, w_out=w_out[0], g=row1(ln1_g[0]), bb=row1(ln1_b[0]),
                            w_router_t=w_router_t, rbias_col=rbias_col)
    h1_p, eidx_p, gw_p, rank_p, counts_p = mix(a_p, b_p, xp, base_counts=jnp.zeros((N_EXPERTS, 1), F32))

    xs_ = x_sample.reshape(n_dec, d)
    k_s, v_s, q_s, a_s, gv_s = _proj_sample(xs_, w_in[0], lng, lnb, w00_row, b0_row)
    n_phys, page = cache_k.shape[1], cache_k.shape[2]
    ck = cache_k[0].transpose(0, 2, 3, 1).reshape(n_phys, 2 * B_HEADS * B_HEAD_DIM, page)
    cv = cache_v[0].reshape(n_phys, page * B_HEADS, B_V_DIM)
    b_s = _decode_attn(page_table, lam_rows, row1(subln_g[0]), q_s, k_s, v_s, ck, cv)
    h1_s, eidx_s, gw_s, rank_s, counts = mix(a_s, b_s, xs_, base_counts=counts_p)

    m_rows = -(-((t + n_dec) * TOP_K + N_EXPERTS * (MOE_TILE - 1)) // MOE_TILE) * MOE_TILE
    counts = counts.reshape(N_EXPERTS).astype(jnp.int32)
    pstart, tile_expert, n_used = _tile_plan(counts, m_rows)
    xs_sorted, rows_p = _dispatch(pstart, counts, eidx_p, rank_p, h1_p, None, m_rows)
    xs_sorted, rows_s = _dispatch(pstart, counts, eidx_s, rank_s, h1_s, xs_sorted, m_rows)
    ys = _experts(tile_expert, n_used, xs_sorted, w_exp_gate[0], w_exp_up[0], w_exp_down[0])

    fin = functools.partial(_final, ys=ys, w_sg=w_sh_gate[0], w_su=w_sh_up[0], w_sd=w_sh_down[0], g=row1(ln2_g[0]),
                            bb=row1(ln2_b[0]), w_pg=w_ple_gate[0], b_pg=row1(b_ple_gate[0]), w_p=w_ple[0])
    y_p = fin(rows_p, h1_p, gw_p.T, p_prompt[0].reshape(t, -1))
    y_s = fin(rows_s, h1_s, gw_s.T, p_sample[0].reshape(n_dec, -1))

    return (y_p.reshape(n_seq, seq, d), y_s.reshape(n_dec, 1, d),
            k_p.reshape(1, n_seq, seq, 2 * B_HEADS, B_HEAD_DIM), v_p.reshape(1, n_seq, seq, B_HEADS, B_V_DIM),
            gv_p.reshape(1, n_seq, CHUNK, d_a),
            k_s.reshape(1, n_dec, 1, 2 * B_HEADS, B_HEAD_DIM), v_s.reshape(1, n_dec, 1, B_HEADS, B_V_DIM),
            gv_s.reshape(1, n_dec, 1, d_a))
```

```python
import functools
import math

import jax
import jax.numpy as jnp
from jax import lax
from jax.experimental import pallas as pl
from jax.experimental.pallas import tpu as pltpu

F32 = jnp.float32
BF16 = jnp.bfloat16

A_HEADS = 8
CHUNK = 128
B_HEADS = 4
B_HEAD_DIM = 64
B_V_DIM = 128
N_EXPERTS = 256
N_GROUPS = 8
GROUP_SIZE = N_EXPERTS // N_GROUPS
TOPK_GROUPS = 4
TOP_K = 8
ROUTED_SCALE = 2.5
LN_EPS = 1e-5
DEPTH = 1
DEEPNORM_ALPHA = (2 * DEPTH) ** 0.25
LAM_INIT = 0.8 - 0.6 * math.exp(-0.3 * 0)

LANES = 128
SUBLANES = 8
ROW_WORDS = 4
ROW_TILE = 256
ATTN_TK = 512
MOE_TILE = 512
DECODE_CHUNK = 16
VMEM_LIMIT = 56 * 1024 * 1024

_NT = (((1,), (1,)), ((), ()))


def _const_spec(shape):
    nd = len(shape)
    return pl.BlockSpec(shape, lambda *_: (0,) * nd)


def _layer_norm(x, g, b):
    mu = jnp.mean(x, axis=-1, keepdims=True)
    xc = x - mu
    var = jnp.mean(xc * xc, axis=-1, keepdims=True)
    return xc * lax.rsqrt(var + LN_EPS) * g + b


def _gelu(x):
    return 0.5 * x * (1.0 + lax.erf(x * (2.0 ** -0.5)))


_ERFC_P = (2.326819970068386e-2, -1.387039388740657e-1, 3.687424674597105e-1, -5.824733027278666e-1,
           6.210004621745983e-1, -4.944515323274145e-1, 3.404879937665872e-1, -2.741127028184656e-1,
           5.638259427386472e-1)
_ERFC_R = (-1.047766399936249e+1, 1.297719955372516e+1, -7.495518717768503e+0, 2.921019019210786e+0,
           -1.015265279202700e+0, 4.218463358204948e-1, -2.820767439740514e-1, 5.641895067754075e-1)


def _horner(y, coeffs):
    r = jnp.full(y.shape, coeffs[0], F32)
    for c in coeffs[1:]:
        r = r * y + c
    return r


def _gelu_tail_exact(x):
    t = -x * (2.0 ** -0.5)
    y = jnp.abs(t)
    q = 1.0 / jnp.maximum(y, 1.0)
    q2 = q * q
    tail = jnp.exp(-t * t) * q * jnp.where(y < 2.0, _horner(q2, _ERFC_P), _horner(q2, _ERFC_R))
    tail = jnp.where(t < 0.0, 2.0 - tail, tail)
    return 0.5 * x * jnp.where(y < 1.0, 1.0 - lax.erf(t), tail)


def _tree_sum0(x):
    n = x.shape[0]
    while n > 1 and n % 2 == 0:
        n //= 2
        x = x[:n] + x[n:]
    return jnp.sum(x, axis=0)


def _lam(lam_ref):
    lv = lam_ref[...]
    s1 = jnp.sum(lv[0:1] * lv[1:2], axis=-1, keepdims=True)
    s2 = jnp.sum(lv[2:3] * lv[3:4], axis=-1, keepdims=True)
    return jnp.exp(s1) - jnp.exp(s2) + LAM_INIT


def _proj_kernel(x_ref, w_in_ref, lng_ref, lnb_ref, wsp_ref, bsp_ref,
                 k_ref, v_ref, qb_ref, kb_ref, vt_ref, a_ref, gv_ref,
                 wbf_ref, wcat_ref, *, tiles_per_seq):
    i = pl.program_id(0)
    tm = x_ref.shape[0]
    d_a = lng_ref.shape[1]

    @pl.when(i == 0)
    def _init():
        rows = w_in_ref.shape[0]
        step = 128

        def body(r, c):
            sl = pl.ds(pl.multiple_of(r * step, step), step)
            wbf_ref[sl, :] = w_in_ref[sl, :].astype(BF16)
            return c

        lax.fori_loop(0, rows // step, body, 0)
        row = lax.broadcasted_iota(jnp.int32, (CHUNK, CHUNK), 0)
        col = lax.broadcasted_iota(jnp.int32, (CHUNK, CHUNK), 1)
        for j in range(A_HEADS // 2):
            w0 = jnp.where(col <= row, wsp_ref[2 * j], 0.0)
            w1 = jnp.where(col <= row, wsp_ref[2 * j + 1], 0.0)
            wcat_ref[j] = jnp.concatenate([w0, w1], axis=1).astype(BF16)

    xb = x_ref[...].astype(BF16)

    def seg(s):
        return jnp.dot(xb, wbf_ref[:, s * d_a:(s + 1) * d_a], preferred_element_type=F32)

    k = seg(3)
    k_ref[...] = k
    kb_ref[...] = k.astype(BF16)
    v = seg(4)
    v_ref[...] = v
    vt_ref[0] = v.T.astype(BF16)
    qb_ref[...] = (seg(2) * (B_HEAD_DIM ** -0.5)).astype(BF16)

    va = _layer_norm(_gelu(seg(1)), lng_ref[...], lnb_ref[...])

    @pl.when(i % tiles_per_seq == tiles_per_seq - 1)
    def _gv():
        gv_ref[0] = va[tm - CHUNK:, :]

    u = _gelu(seg(0))
    vab = va.astype(BF16)
    lane = lax.broadcasted_iota(jnp.int32, (CHUNK, LANES), 1)
    zero = jnp.zeros((CHUNK, LANES), BF16)
    for c in range(tm // CHUNK):
        r0 = c * CHUNK
        for j in range(A_HEADS // 2):
            vp = vab[r0:r0 + CHUNK, j * LANES:(j + 1) * LANES]
            stacked = jnp.concatenate([jnp.where(lane < 64, vp, zero), jnp.where(lane >= 64, vp, zero)], axis=0)
            sg = jnp.dot(wcat_ref[j], stacked, preferred_element_type=F32) + bsp_ref[:, j * LANES:(j + 1) * LANES]
            a_ref[r0:r0 + CHUNK, j * LANES:(j + 1) * LANES] = (
                u[r0:r0 + CHUNK, j * LANES:(j + 1) * LANES] * sg).astype(BF16)


def _proj_prompt(x, w_in, lng, lnb, w_sp, b_sp_full, seq):
    t, d = x.shape
    d_in = w_in.shape[1]
    d_a = lng.shape[1]
    tm = min(ROW_TILE, seq)
    tiles_per_seq = seq // tm
    n_seq = t // seq
    row = lambda w: pl.BlockSpec((tm, w), lambda i: (i, 0))
    out_shape = (
        jax.ShapeDtypeStruct((t, d_a), F32), jax.ShapeDtypeStruct((t, d_a), F32),
        jax.ShapeDtypeStruct((t, d_a), BF16), jax.ShapeDtypeStruct((t, d_a), BF16),
        jax.ShapeDtypeStruct((t // tm, d_a, tm), BF16), jax.ShapeDtypeStruct((t, d_a), BF16),
        jax.ShapeDtypeStruct((n_seq, CHUNK, d_a), F32),
    )
    return pl.pallas_call(
        functools.partial(_proj_kernel, tiles_per_seq=tiles_per_seq),
        grid=(t // tm,),
        in_specs=[row(d), _const_spec((d, d_in)), _const_spec((1, d_a)), _const_spec((1, d_a)),
                  _const_spec((A_HEADS, CHUNK, CHUNK)), _const_spec((CHUNK, d_a))],
        out_specs=(row(d_a), row(d_a), row(d_a), row(d_a), pl.BlockSpec((1, d_a, tm), lambda i: (i, 0, 0)), row(d_a),
                   pl.BlockSpec((1, CHUNK, d_a), lambda i: (i // tiles_per_seq, 0, 0))),
        out_shape=out_shape,
        scratch_shapes=[pltpu.VMEM((d, d_in), BF16), pltpu.VMEM((A_HEADS // 2, CHUNK, 2 * CHUNK), BF16)],
        compiler_params=pltpu.CompilerParams(dimension_semantics=("arbitrary",), vmem_limit_bytes=VMEM_LIMIT),
        name="proj_prompt",
    )(x, w_in, lng, lnb, w_sp, b_sp_full)


def _attn_kernel(lam_ref, gcol_ref, q_ref, k_ref, vt_ref, o_ref, acc, m, l):
    i = pl.program_id(2)
    tq = q_ref.shape[0]
    per_wide = max(1, min(ATTN_TK // tq, vt_ref.shape[1]))
    qb = q_ref[...]
    lane = lax.broadcasted_iota(jnp.int32, qb.shape, 1)
    zero = jnp.zeros(qb.shape, BF16)
    q2 = jnp.concatenate([jnp.where(lane < B_HEAD_DIM, qb, zero), jnp.where(lane >= B_HEAD_DIM, qb, zero)], axis=0)

    acc[...] = jnp.zeros(acc.shape, F32)
    m[...] = jnp.full(m.shape, -jnp.inf, F32)
    l[...] = jnp.zeros(l.shape, F32)

    def block(c0, n_ch, causal):
        width = n_ch * tq
        kb = k_ref[pl.ds(pl.multiple_of(c0 * tq, tq), width), :]
        s = lax.dot_general(kb, q2, _NT, preferred_element_type=F32)
        if causal:
            key = lax.broadcasted_iota(jnp.int32, (width, 2 * tq), 0)
            qry = lax.broadcasted_iota(jnp.int32, (width, 2 * tq), 1) % tq
            s = jnp.where(key <= qry, s, -jnp.inf)
        m_old = m[...]
        m_new = jnp.maximum(m_old, jnp.max(s, axis=0, keepdims=True))
        alpha = jnp.exp(m_old - m_new)
        p = jnp.exp(s - m_new)
        l[...] = alpha * l[...] + jnp.sum(p, axis=0, keepdims=True)
        pb = p.astype(BF16)
        pv = jnp.dot(vt_ref[0, c0], pb[:tq], preferred_element_type=F32)
        for ch in range(1, n_ch):
            pv = pv + jnp.dot(vt_ref[0, c0 + ch], pb[ch * tq:(ch + 1) * tq], preferred_element_type=F32)
        acc[...] = alpha * acc[...] + pv
        m[...] = m_new

    n_wide = i // per_wide

    def wide_body(j, c):
        block(j * per_wide, per_wide, False)
        return c

    lax.fori_loop(0, n_wide, wide_body, 0)
    for r in range(per_wide - 1):
        @pl.when(n_wide * per_wide + r < i)
        def _():
            block(n_wide * per_wide + r, 1, False)
    block(i, 1, True)

    lam = _lam(lam_ref)
    w = acc[...] / l[...]
    o = w[:, :tq] - lam * w[:, tq:]
    ms = jnp.mean(o * o, axis=0, keepdims=True)
    o = o * lax.rsqrt(ms + LN_EPS) * gcol_ref[...] * (1.0 - LAM_INIT)
    o_ref[...] = o.T.astype(o_ref.dtype)


def _attn_prompt(lam_rows, subln_g_col, qb, kb, vt, n_seq, seq):
    t, d_b = qb.shape
    tq = vt.shape[2]
    nq = seq // tq
    vt4 = vt.reshape(n_seq, nq, d_b, tq)
    return pl.pallas_call(
        _attn_kernel,
        grid=(n_seq, B_HEADS, nq),
        in_specs=[_const_spec((4, B_HEAD_DIM)), _const_spec((B_V_DIM, 1)),
                  pl.BlockSpec((tq, LANES), lambda b, h, i: (b * nq + i, h)),
                  pl.BlockSpec((seq, LANES), lambda b, h, i: (b, h)),
                  pl.BlockSpec((1, nq, B_V_DIM, tq), lambda b, h, i: (b, 0, h, 0))],
        out_specs=pl.BlockSpec((tq, LANES), lambda b, h, i: (b * nq + i, h)),
        out_shape=jax.ShapeDtypeStruct((t, d_b), BF16),
        scratch_shapes=[pltpu.VMEM((B_V_DIM, 2 * tq), F32), pltpu.VMEM((1, 2 * tq), F32),
                        pltpu.VMEM((1, 2 * tq), F32)],
        compiler_params=pltpu.CompilerParams(dimension_semantics=("arbitrary", "arbitrary", "arbitrary"),
                                             vmem_limit_bytes=VMEM_LIMIT),
        name="attn_prompt",
    )(lam_rows, subln_g_col, qb, kb, vt4)


def _route(logits_t, bias_col):
    e, n = logits_t.shape
    scores = jax.nn.sigmoid(logits_t)
    biased = scores + bias_col
    ninf = jnp.float32(-jnp.inf)
    gs = []
    sub = lax.broadcasted_iota(jnp.int32, (GROUP_SIZE, n), 0)
    for g in range(N_GROUPS):
        blk = biased[g * GROUP_SIZE:(g + 1) * GROUP_SIZE]
        t1 = jnp.max(blk, axis=0, keepdims=True)
        i1 = jnp.min(jnp.where(blk == t1, sub, GROUP_SIZE), axis=0, keepdims=True)
        t2 = jnp.max(jnp.where(sub == i1, ninf, blk), axis=0, keepdims=True)
        gs.append(t1 + t2)
    keep = []
    for g in range(N_GROUPS):
        cnt = jnp.zeros((1, n), jnp.int32)
        for o in range(N_GROUPS):
            if o == g:
                continue
            beats = (gs[o] >= gs[g]) if o < g else (gs[o] > gs[g])
            cnt = cnt + beats.astype(jnp.int32)
        keep.append(cnt < TOPK_GROUPS)
    masked = jnp.concatenate(
        [jnp.where(keep[g], biased[g * GROUP_SIZE:(g + 1) * GROUP_SIZE], ninf) for g in range(N_GROUPS)], axis=0)
    rid = lax.broadcasted_iota(jnp.int32, (e, n), 0)
    idxs, vals, hits = [], [], []
    for _ in range(TOP_K):
        top = jnp.max(masked, axis=0, keepdims=True)
        idx = jnp.min(jnp.where(masked == top, rid, e), axis=0, keepdims=True)
        hit = rid == idx
        vals.append(jnp.sum(jnp.where(hit, scores, 0.0), axis=0, keepdims=True))
        idxs.append(idx)
        hits.append(hit)
        masked = jnp.where(hit, ninf, masked)
    eidx = jnp.concatenate(idxs, axis=0)
    gw = jnp.concatenate(vals, axis=0)
    gw = gw / jnp.sum(gw, axis=0, keepdims=True) * ROUTED_SCALE
    return eidx, gw, hits


def _mix_kernel(a_ref, b_ref, x_ref, wout_ref, g_ref, bb_ref, wrt_ref, rb_ref, base_ref,
                h1_ref, eidx_ref, gw_ref, rank_ref, cnt_ref, wo_s, wr_s, before_s, seen_s):
    i = pl.program_id(0)
    d_a = a_ref.shape[1]
    tm = x_ref.shape[0]

    @pl.when(i == 0)
    def _init():
        wo_s[...] = wout_ref[...].astype(BF16)
        wr_s[...] = wrt_ref[...].astype(BF16)
        r = lax.broadcasted_iota(jnp.int32, (tm, tm), 0)
        c = lax.broadcasted_iota(jnp.int32, (tm, tm), 1)
        before_s[...] = jnp.where(r < c, 1.0, 0.0).astype(BF16)
        seen_s[...] = base_ref[...]

    dot = functools.partial(jnp.dot, preferred_element_type=F32)
    mix = dot(a_ref[...], wo_s[:d_a, :]) + dot(b_ref[...], wo_s[d_a:, :])
    h1 = _layer_norm(DEEPNORM_ALPHA * x_ref[...] + mix, g_ref[...], bb_ref[...])
    h1_ref[...] = h1
    logits_t = lax.dot_general(wr_s[...], h1.astype(BF16), _NT, preferred_element_type=F32)
    eidx, gw, hits = _route(logits_t, rb_ref[...])
    eidx_ref[...] = eidx
    gw_ref[...] = gw
    onehot = jnp.zeros(logits_t.shape, F32)
    for hit in hits:
        onehot = jnp.where(hit, 1.0, onehot)
    earlier = dot(onehot.astype(BF16), before_s[...]) + seen_s[...]
    rank_ref[...] = jnp.concatenate(
        [jnp.sum(jnp.where(hit, earlier, 0.0), axis=0, keepdims=True) for hit in hits], axis=0).astype(jnp.int32)
    seen = seen_s[...] + jnp.sum(onehot, axis=1, keepdims=True)
    seen_s[...] = seen
    cnt_ref[...] = seen


def _mix_route(a, b, x, w_out, g, bb, w_router_t, rbias_col, base_counts):
    t, d = x.shape
    d_a = a.shape[1]
    tm = min(ROW_TILE, t)
    row = lambda w: pl.BlockSpec((tm, w), lambda i: (i, 0))
    colb = pl.BlockSpec((TOP_K, tm), lambda i: (0, i))
    return pl.pallas_call(
        _mix_kernel,
        grid=(t // tm,),
        in_specs=[row(d_a), row(d_a), row(d), _const_spec((d, d)), _const_spec((1, d)), _const_spec((1, d)),
                  _const_spec((N_EXPERTS, d)), _const_spec((N_EXPERTS, 1)), _const_spec((N_EXPERTS, 1))],
        out_specs=(row(d), colb, colb, colb, _const_spec((N_EXPERTS, 1))),
        out_shape=(jax.ShapeDtypeStruct((t, d), F32),
                   jax.ShapeDtypeStruct((TOP_K, t), jnp.int32), jax.ShapeDtypeStruct((TOP_K, t), F32),
                   jax.ShapeDtypeStruct((TOP_K, t), jnp.int32), jax.ShapeDtypeStruct((N_EXPERTS, 1), F32)),
        scratch_shapes=[pltpu.VMEM((d, d), BF16), pltpu.VMEM((N_EXPERTS, d), BF16), pltpu.VMEM((tm, tm), BF16),
                        pltpu.VMEM((N_EXPERTS, 1), F32)],
        compiler_params=pltpu.CompilerParams(dimension_semantics=("arbitrary",), vmem_limit_bytes=VMEM_LIMIT),
        name="mix_route",
    )(a, b, x, w_out, g, bb, w_router_t, rbias_col, base_counts)


def _pack_bf16(x):
    w = x.shape[1] // 2
    u = pltpu.bitcast(x.astype(BF16).astype(F32), jnp.uint32)
    return (u[:, :w] >> 16) | u[:, w:]


def _unpack_bf16(u):
    lo = pltpu.bitcast(u << 16, F32)
    hi = pltpu.bitcast(u & jnp.uint32(0xFFFF0000), F32)
    return jnp.concatenate([lo, hi], axis=1)


def _to_row_tiles(x, ref):
    n, pieces = x.shape[0], x.shape[1] // LANES
    for c in range(pieces):
        ref[pl.ds(c, n, stride=pieces), :] = x[:, c * LANES:(c + 1) * LANES]


def _from_row_tiles(ref, n):
    pieces = ref.shape[0] // n
    return jnp.concatenate([ref[pl.ds(c, n, stride=pieces), :] for c in range(pieces)], axis=1)


def _tile_copy(src, src_row, dst, dst_row, sem):
    def tile(ref, r):
        return ref.at[pl.ds(pl.multiple_of(r * ROW_WORDS, ROW_WORDS), ROW_WORDS), :]
    return pltpu.make_async_copy(tile(src, src_row), tile(dst, dst_row), sem)


def _dispatch_rows(pstart_ref, eidx_ref, rank_ref, h1_ref, xs_ref, row_ref, stage, sems):
    i = pl.program_id(0)
    tm = h1_ref.shape[0]
    slot = i % 2
    _to_row_tiles(_pack_bf16(h1_ref[...]), stage.at[slot])

    def start_token(j, c):
        for k in range(TOP_K):
            row = pstart_ref[eidx_ref[k, j]] + rank_ref[k, j]
            row_ref[k, j] = row
            _tile_copy(stage.at[slot], j, xs_ref, row, sems.at[slot]).start(priority=k % 2)
        return c

    def wait_all(s):
        def wait_token(j, c):
            for k in range(TOP_K):
                _tile_copy(stage.at[s], 0, xs_ref, 0, sems.at[s]).wait()
            return c
        lax.fori_loop(0, tm, wait_token, 0)

    lax.fori_loop(0, tm, start_token, 0)

    @pl.when(i > 0)
    def _():
        wait_all(1 - slot)

    @pl.when(i == pl.num_programs(0) - 1)
    def _():
        wait_all(slot)


def _dispatch_first_kernel(pstart_ref, cnt_ref, eidx_ref, rank_ref, h1_ref, xs_ref, row_ref, stage, zero_s, sems,
                           zsem):
    @pl.when(pl.program_id(0) == 0)
    def _pad():
        zero_s[...] = jnp.zeros(zero_s.shape, zero_s.dtype)

        def per_expert(action):
            def body(e, c):
                row = pstart_ref[e] + cnt_ref[e]
                n_pad = (MOE_TILE - lax.rem(cnt_ref[e], MOE_TILE)) & (MOE_TILE - 1)
                p = MOE_TILE // 2
                while p >= 1:
                    take = (n_pad & p) != 0

                    @pl.when(take)
                    def _(row=row, p=p):
                        action(pltpu.make_async_copy(
                            zero_s.at[pl.ds(0, p * ROW_WORDS), :],
                            xs_ref.at[pl.ds(pl.multiple_of(row * ROW_WORDS, ROW_WORDS), p * ROW_WORDS), :], zsem))

                    row = row + jnp.where(take, p, 0)
                    p //= 2
                return c
            return body

        lax.fori_loop(0, N_EXPERTS, per_expert(lambda cp: cp.start()), 0)
        lax.fori_loop(0, N_EXPERTS, per_expert(lambda cp: cp.wait()), 0)

    _dispatch_rows(pstart_ref, eidx_ref, rank_ref, h1_ref, xs_ref, row_ref, stage, sems)


def _dispatch_more_kernel(pstart_ref, cnt_ref, eidx_ref, rank_ref, h1_ref, xs_in_ref, xs_ref, row_ref, stage, sems):
    del cnt_ref, xs_in_ref
    _dispatch_rows(pstart_ref, eidx_ref, rank_ref, h1_ref, xs_ref, row_ref, stage, sems)


def _dispatch(pstart, counts, eidx, rank, h1, xs, m_rows):
    t, d = h1.shape
    assert d == 2 * ROW_WORDS * LANES
    tm = min(ROW_TILE, t)
    first = xs is None
    smem_blk = pl.BlockSpec((TOP_K, tm), lambda i, *_: (0, i), memory_space=pltpu.SMEM)
    any_spec = pl.BlockSpec(memory_space=pl.ANY)
    in_specs = [smem_blk, smem_blk, pl.BlockSpec((tm, d), lambda i, *_: (i, 0))]
    stage = [pltpu.VMEM((2, tm * ROW_WORDS, LANES), jnp.uint32)]
    scratch = [pltpu.SemaphoreType.DMA((2,))]
    return pl.pallas_call(
        _dispatch_first_kernel if first else _dispatch_more_kernel,
        grid_spec=pltpu.PrefetchScalarGridSpec(
            num_scalar_prefetch=2,
            grid=(t // tm,),
            in_specs=in_specs if first else in_specs + [any_spec],
            out_specs=(any_spec, smem_blk),
            scratch_shapes=(stage + [pltpu.VMEM((MOE_TILE // 2 * ROW_WORDS, LANES), jnp.uint32)] + scratch
                            + [pltpu.SemaphoreType.DMA(())] if first else stage + scratch)),
        out_shape=(jax.ShapeDtypeStruct((m_rows * ROW_WORDS, LANES), jnp.uint32),
                   jax.ShapeDtypeStruct((TOP_K, t), jnp.int32)),
        input_output_aliases={} if first else {5: 0},
        compiler_params=pltpu.CompilerParams(dimension_semantics=("arbitrary",), vmem_limit_bytes=VMEM_LIMIT),
        name="dispatch_first" if first else "dispatch_more",
    )(*((pstart, counts, eidx, rank, h1) if first else (pstart, counts, eidx, rank, h1, xs)))


def _experts_kernel(te_ref, nu_ref, x_ref, wg_ref, wu_ref, wd_ref, y_ref, wgu_s, wd_s):
    t = pl.program_id(0)
    prev = te_ref[jnp.maximum(t - 1, 0)]
    d_e = wg_ref.shape[2]

    @pl.when((t < nu_ref[0]) & ((t == 0) | (te_ref[t] != prev)))
    def _cast():
        wgu_s[:, :d_e] = wg_ref[0].astype(BF16)
        wgu_s[:, d_e:] = wu_ref[0].astype(BF16)
        wd_s[...] = wd_ref[0].astype(BF16)

    @pl.when(t < nu_ref[0])
    def _compute():
        x = _unpack_bf16(_from_row_tiles(x_ref, MOE_TILE)).astype(BF16)
        gu = jnp.dot(x, wgu_s[...], preferred_element_type=F32)
        hid = jax.nn.silu(gu[:, :d_e]) * gu[:, d_e:]
        _to_row_tiles(_pack_bf16(jnp.dot(hid.astype(BF16), wd_s[...], preferred_element_type=F32)), y_ref)


def _experts(tile_expert, n_used, xs, w_g, w_u, w_d):
    d, d_e = w_g.shape[1], w_g.shape[2]
    n_tiles = xs.shape[0] // (MOE_TILE * ROW_WORDS)

    def rows(t, te, nu):
        return (jnp.minimum(t, nu[0] - 1), 0)

    def wmap(t, te, nu):
        return (te[jnp.minimum(t, nu[0] - 1)], 0, 0)

    return pl.pallas_call(
        _experts_kernel,
        grid_spec=pltpu.PrefetchScalarGridSpec(
            num_scalar_prefetch=2,
            grid=(n_tiles,),
            in_specs=[pl.BlockSpec((MOE_TILE * ROW_WORDS, LANES), rows),
                      pl.BlockSpec((1, d, d_e), wmap), pl.BlockSpec((1, d, d_e), wmap),
                      pl.BlockSpec((1, d_e, d), wmap)],
            out_specs=pl.BlockSpec((MOE_TILE * ROW_WORDS, LANES), rows),
            scratch_shapes=[pltpu.VMEM((d, 2 * d_e), BF16), pltpu.VMEM((d_e, d), BF16)]),
        out_shape=jax.ShapeDtypeStruct(xs.shape, jnp.uint32),
        compiler_params=pltpu.CompilerParams(dimension_semantics=("arbitrary",), vmem_limit_bytes=VMEM_LIMIT),
        name="experts",
    )(tile_expert, n_used, xs, w_g, w_u, w_d)


def _final_kernel(rows_ref, rows_next_ref, h1_ref, gw_ref, p_ref, ys_ref, wsg_ref, wsu_ref, wsd_ref, g_ref, bb_ref,
                  wpg_ref, bpg_ref, wp_ref, y_ref, wgu_s, wd_s, wpg_s, wp_s, ybuf, sems):
    i = pl.program_id(0)
    n = pl.num_programs(0)
    d_s = wsg_ref.shape[1]
    tm = h1_ref.shape[0]

    def gather(idx_ref, slot, action):
        def body(j, c):
            for k in range(TOP_K):
                src_row = idx_ref[k, j] if action == "start" else 0
                cp = _tile_copy(ys_ref, src_row, ybuf.at[slot, k], j, sems.at[slot])
                cp.start(priority=k % 2) if action == "start" else cp.wait()
            return c
        lax.fori_loop(0, tm, body, 0)

    @pl.when(i == 0)
    def _init():
        gather(rows_ref, 0, "start")
        wgu_s[:, :d_s] = wsg_ref[...].astype(BF16)
        wgu_s[:, d_s:] = wsu_ref[...].astype(BF16)
        wd_s[...] = wsd_ref[...].astype(BF16)
        wpg_s[...] = wpg_ref[...].astype(BF16)
        wp_s[...] = wp_ref[...].astype(BF16)

    slot = i % 2

    @pl.when(i + 1 < n)
    def _prefetch():
        gather(rows_next_ref, 1 - slot, "start")

    dot = functools.partial(jnp.dot, preferred_element_type=F32)
    h1 = h1_ref[...]
    gu = dot(h1.astype(BF16), wgu_s[...])
    hid = jax.nn.silu(gu[:, :d_s]) * gu[:, d_s:]
    shared = dot(hid.astype(BF16), wd_s[...])

    gather(rows_ref, slot, "wait")
    gwv = gw_ref[...]
    routed = gwv[:, 0:1] * _unpack_bf16(_from_row_tiles(ybuf.at[slot, 0], tm))
    for k in range(1, TOP_K):
        routed = routed + gwv[:, k:k + 1] * _unpack_bf16(_from_row_tiles(ybuf.at[slot, k], tm))

    h2 = _layer_norm(DEEPNORM_ALPHA * h1 + (routed + shared), g_ref[...], bb_ref[...])
    gate = jax.nn.sigmoid(dot(h2.astype(BF16), wpg_s[...]) + bpg_ref[...])
    y_ref[...] = h2 + gate * dot(p_ref[...].astype(BF16), wp_s[...])


def _final(row_of, h1, gw, p, ys, w_sg, w_su, w_sd, g, bb, w_pg, b_pg, w_p):
    t, d = h1.shape
    d_s = w_sg.shape[1]
    d_p = p.shape[1]
    tm = min(ROW_TILE, t)
    n = t // tm
    row = lambda w: pl.BlockSpec((tm, w), lambda i: (i, 0))
    return pl.pallas_call(
        _final_kernel,
        grid=(n,),
        in_specs=[pl.BlockSpec((TOP_K, tm), lambda i: (0, i), memory_space=pltpu.SMEM),
                  pl.BlockSpec((TOP_K, tm), lambda i: (0, jnp.minimum(i + 1, n - 1)), memory_space=pltpu.SMEM),
                  row(d), row(TOP_K), row(d_p), pl.BlockSpec(memory_space=pl.ANY),
                  _const_spec((d, d_s)), _const_spec((d, d_s)), _const_spec((d_s, d)),
                  _const_spec((1, d)), _const_spec((1, d)), _const_spec((d, d)), _const_spec((1, d)),
                  _const_spec((d_p, d))],
        out_specs=row(d),
        out_shape=jax.ShapeDtypeStruct((t, d), F32),
        scratch_shapes=[pltpu.VMEM((d, 2 * d_s), BF16), pltpu.VMEM((d_s, d), BF16), pltpu.VMEM((d, d), BF16),
                        pltpu.VMEM((d_p, d), BF16), pltpu.VMEM((2, TOP_K, tm * ROW_WORDS, LANES), jnp.uint32),
                        pltpu.SemaphoreType.DMA((2,))],
        compiler_params=pltpu.CompilerParams(dimension_semantics=("arbitrary",), vmem_limit_bytes=VMEM_LIMIT),
        name="final",
    )(row_of, row_of, h1, gw, p, ys, w_sg, w_su, w_sd, g, bb, w_pg, b_pg, w_p)


def _proj_sample_kernel(x_ref, w_in_ref, lng_ref, lnb_ref, w00_ref, b0_ref,
                        k_ref, v_ref, q_ref, a_ref, gv_ref):
    d_a = lng_ref.shape[1]
    xb = x_ref[...].astype(BF16)

    def seg(s):
        return jnp.dot(xb, w_in_ref[:, s * d_a:(s + 1) * d_a].astype(BF16), preferred_element_type=F32)

    k_ref[...] = seg(3)
    v_ref[...] = seg(4)
    q_ref[...] = seg(2) * (B_HEAD_DIM ** -0.5)
    va = _layer_norm(_gelu_tail_exact(seg(1)), lng_ref[...], lnb_ref[...])
    gv_ref[...] = va
    sg = va * w00_ref[...] + b0_ref[...]
    a_ref[...] = (_gelu_tail_exact(seg(0)) * sg).astype(a_ref.dtype)


def _proj_sample(x, w_in, lng, lnb, w00_row, b0_row):
    n, d = x.shape
    d_in = w_in.shape[1]
    d_a = lng.shape[1]
    full = lambda w: _const_spec((n, w))
    f32 = jax.ShapeDtypeStruct((n, d_a), F32)
    return pl.pallas_call(
        _proj_sample_kernel,
        grid=(1,),
        in_specs=[full(d), _const_spec((d, d_in)), _const_spec((1, d_a)), _const_spec((1, d_a)),
                  _const_spec((1, d_a)), _const_spec((1, d_a))],
        out_specs=(full(d_a),) * 5,
        out_shape=(f32, f32, f32, jax.ShapeDtypeStruct((n, d_a), BF16), f32),
        compiler_params=pltpu.CompilerParams(dimension_semantics=("arbitrary",), vmem_limit_bytes=VMEM_LIMIT),
        name="proj_sample",
    )(x, w_in, lng, lnb, w00_row, b0_row)


def _decode_kernel(pt_ref, lam_ref, g_ref, q_ref, kn_ref, vn_ref, kt_ref, vr_ref, o_ref,
                   kbuf, vbuf, ksem, vsem, s_scr, acc):
    b = pl.program_id(0)
    n_seq = pl.num_programs(0)
    n_maps = 2 * B_HEADS
    d_b = q_ref.shape[2]
    chunk = kbuf.shape[1]
    n_chunks = s_scr.shape[0] // chunk
    page = s_scr.shape[2]

    def chunk_copies(cache_ref, buf, sem, seq, c, slot):
        return [pltpu.make_async_copy(cache_ref.at[pt_ref[seq, c * chunk + j]], buf.at[slot, j], sem.at[slot])
                for j in range(chunk)]

    def start(copies):
        for cp in copies:
            cp.start()

    def wait(copies):
        for cp in copies:
            cp.wait()

    @pl.when(b == 0)
    def _first_chunk():
        start(chunk_copies(kt_ref, kbuf, ksem, 0, 0, 0))

    lane = lax.broadcasted_iota(jnp.int32, (n_maps, d_b), 1)
    mrow = lax.broadcasted_iota(jnp.int32, (n_maps, d_b), 0)
    own_map = jnp.where(mrow < B_HEADS, 2 * mrow, 2 * (mrow - B_HEADS) + 1)
    qrow = jnp.where(lane // B_HEAD_DIM == own_map, jnp.broadcast_to(q_ref[0], (n_maps, d_b)), 0.0).astype(BF16)

    def score_chunk(c, carry):
        slot = c % 2

        @pl.when(c + 1 < n_chunks)
        def _():
            start(chunk_copies(kt_ref, kbuf, ksem, b, c + 1, 1 - slot))

        @pl.when(c + 1 == n_chunks)
        def _():
            start(chunk_copies(vr_ref, vbuf, vsem, b, 0, 0))

        wait(chunk_copies(kt_ref, kbuf, ksem, b, c, slot))
        for j in range(chunk):
            s_scr[c * chunk + j] = jnp.dot(qrow, kbuf[slot, j].astype(BF16), preferred_element_type=F32)
        return carry

    lax.fori_loop(0, n_chunks, score_chunk, 0)

    lam = _lam(lam_ref)

    def diff_weights(w):
        top = w[:B_HEADS] - lam * w[B_HEADS:]
        return jnp.concatenate([top, jnp.zeros_like(top)], axis=0).astype(BF16)

    kn = kn_ref[0].astype(BF16).astype(F32)
    snew = jnp.sum(qrow.astype(F32) * kn, axis=-1, keepdims=True)
    s_all = s_scr[...]
    m = jnp.maximum(jnp.max(jnp.max(s_all, axis=0), axis=-1, keepdims=True), snew)
    e_new = jnp.exp(snew - m)
    z = jnp.sum(_tree_sum0(jnp.exp(s_all - m)), axis=-1, keepdims=True) + e_new
    wd_new = diff_weights(e_new / z).astype(F32)
    vn = vn_ref[0].astype(BF16).astype(F32)
    for h in range(B_HEADS):
        acc[h:h + 1, :] = wd_new[h:h + 1] * vn[:, h * B_V_DIM:(h + 1) * B_V_DIM]

    def value_chunk(c, carry):
        slot = c % 2

        @pl.when(c + 1 < n_chunks)
        def _():
            start(chunk_copies(vr_ref, vbuf, vsem, b, c + 1, 1 - slot))

        @pl.when((c + 1 == n_chunks) & (b + 1 < n_seq))
        def _():
            start(chunk_copies(kt_ref, kbuf, ksem, b + 1, 0, 0))

        wait(chunk_copies(vr_ref, vbuf, vsem, b, c, slot))
        for j in range(chunk):
            wd = diff_weights(jnp.exp(s_scr[c * chunk + j] - m) / z)
            for h in range(B_HEADS):
                vh = vbuf[slot, j, pl.ds(h, page, stride=B_HEADS), :].astype(BF16)
                acc[h:h + 1, :] += jnp.dot(wd, vh, preferred_element_type=F32)[h:h + 1]
        return carry

    lax.fori_loop(0, n_chunks, value_chunk, 0)

    for h in range(B_HEADS):
        oh = acc[h:h + 1, :]
        ms = jnp.mean(oh * oh, axis=-1, keepdims=True)
        o_ref[0, :, h * B_V_DIM:(h + 1) * B_V_DIM] = (
            oh * lax.rsqrt(ms + LN_EPS) * g_ref[...] * (1.0 - LAM_INIT)).astype(o_ref.dtype)


def _decode_attn(page_table, lam_rows, subln_g, q, k_new, v_new, cache_k, cache_v):
    n, d_b = q.shape
    n_pages = page_table.shape[1]
    page = cache_k.shape[2]
    chunk = math.gcd(n_pages, DECODE_CHUNK)
    q3, k3, v3 = (a.reshape(n, 1, d_b) for a in (q, k_new, v_new))
    per_seq = pl.BlockSpec((1, 1, d_b), lambda b, pt: (b, 0, 0))
    any_spec = pl.BlockSpec(memory_space=pl.ANY)
    out = pl.pallas_call(
        _decode_kernel,
        grid_spec=pltpu.PrefetchScalarGridSpec(
            num_scalar_prefetch=1,
            grid=(n,),
            in_specs=[pl.BlockSpec((4, B_HEAD_DIM), lambda b, pt: (0, 0)),
                      pl.BlockSpec((1, B_V_DIM), lambda b, pt: (0, 0)),
                      per_seq, per_seq, per_seq, any_spec, any_spec],
            out_specs=per_seq,
            scratch_shapes=[pltpu.VMEM((2, chunk) + cache_k.shape[1:], F32),
                            pltpu.VMEM((2, chunk) + cache_v.shape[1:], F32),
                            pltpu.SemaphoreType.DMA((2,)), pltpu.SemaphoreType.DMA((2,)),
                            pltpu.VMEM((n_pages, 2 * B_HEADS, page), F32), pltpu.VMEM((B_HEADS, B_V_DIM), F32)]),
        out_shape=jax.ShapeDtypeStruct((n, 1, d_b), BF16),
        compiler_params=pltpu.CompilerParams(dimension_semantics=("arbitrary",), vmem_limit_bytes=VMEM_LIMIT),
        name="decode_attn",
    )(page_table, lam_rows, subln_g, q3, k3, v3, cache_k, cache_v)
    return out.reshape(n, d_b)


def _tile_plan(counts, m_rows):
    padded = (counts + MOE_TILE - 1) // MOE_TILE * MOE_TILE
    pends = jnp.cumsum(padded)
    n_tiles = m_rows // MOE_TILE
    tile_expert = jnp.minimum(
        jnp.searchsorted(pends, jnp.arange(n_tiles, dtype=jnp.int32) * MOE_TILE, side='right'),
        N_EXPERTS - 1).astype(jnp.int32)
    n_used = (pends[-1:] // MOE_TILE).astype(jnp.int32)
    return (pends - padded).astype(jnp.int32), tile_expert, n_used


def kernel(x_prompt, x_sample, cache_k, cache_v, page_table, p_prompt, p_sample, w_in, ln_v_g, ln_v_b, w_spatial,
           b_spatial, lambda_q1, lambda_k1, lambda_q2, lambda_k2, subln_g, w_out, ln1_g, ln1_b, w_router,
           router_bias, w_exp_gate, w_exp_up, w_exp_down, w_sh_gate, w_sh_up, w_sh_down, ln2_g, ln2_b, w_ple,
           w_ple_gate, b_ple_gate):
    assert w_in.shape[0] == DEPTH and x_sample.shape[1] == 1
    n_seq, seq, d = x_prompt.shape
    n_dec = x_sample.shape[0]
    t = n_seq * seq
    d_a = ln_v_g.shape[1]
    d_b = B_HEADS * B_V_DIM
    row1 = lambda a: a.reshape(1, -1)

    lam_rows = jnp.concatenate([lambda_q1, lambda_k1, lambda_q2, lambda_k2], axis=0)
    lng, lnb = row1(ln_v_g[0]), row1(ln_v_b[0])
    b_sp_full = jnp.repeat(b_spatial[0].T, d_a // A_HEADS, axis=1)
    w00_row = row1(jnp.repeat(w_spatial[0, :, 0, 0], d_a // A_HEADS))
    b0_row = b_sp_full[0:1]
    w_router_t = w_router[0].T
    rbias_col = router_bias[0].reshape(N_EXPERTS, 1)

    xp = x_prompt.reshape(t, d)
    k_p, v_p, qb, kb, vt, a_p, gv_p = _proj_prompt(xp, w_in[0], lng, lnb, w_spatial[0], b_sp_full, seq)
    b_p = _attn_prompt(lam_rows, subln_g[0].reshape(B_V_DIM, 1), qb, kb, vt, n_seq, seq)
    mix = functools.partial(_mix_route, w_out=w_out[0], g=row1(ln1_g[0]), bb=row1(ln1_b[0]),
                            w_router_t=w_router_t, rbias_col=rbias_col)
    h1_p, eidx_p, gw_p, rank_p, counts_p = mix(a_p, b_p, xp, base_counts=jnp.zeros((N_EXPERTS, 1), F32))

    xs_ = x_sample.reshape(n_dec, d)
    k_s, v_s, q_s, a_s, gv_s = _proj_sample(xs_, w_in[0], lng, lnb, w00_row, b0_row)
    n_phys, page = cache_k.shape[1], cache_k.shape[2]
    ck = cache_k[0].transpose(0, 2, 3, 1).reshape(n_phys, 2 * B_HEADS * B_HEAD_DIM, page)
    cv = cache_v[0].reshape(n_phys, page * B_HEADS, B_V_DIM)
    b_s = _decode_attn(page_table, lam_rows, row1(subln_g[0]), q_s, k_s, v_s, ck, cv)
    h1_s, eidx_s, gw_s, rank_s, counts = mix(a_s, b_s, xs_, base_counts=counts_p)

    m_rows = -(-((t + n_dec) * TOP_K + N_EXPERTS * (MOE_TILE - 1)) // MOE_TILE) * MOE_TILE
    counts = counts.reshape(N_EXPERTS).astype(jnp.int32)
    pstart, tile_expert, n_used = _tile_plan(counts, m_rows)
    xs_sorted, rows_p = _dispatch(pstart, counts, eidx_p, rank_p, h1_p, None, m_rows)
    xs_sorted, rows_s = _dispatch(pstart, counts, eidx_s, rank_s, h1_s, xs_sorted, m_rows)
    ys = _experts(tile_expert, n_used, xs_sorted, w_exp_gate[0], w_exp_up[0], w_exp_down[0])

    fin = functools.partial(_final, ys=ys, w_sg=w_sh_gate[0], w_su=w_sh_up[0], w_sd=w_sh_down[0], g=row1(ln2_g[0]),
                            bb=row1(ln2_b[0]), w_pg=w_ple_gate[0], b_pg=row1(b_ple_gate[0]), w_p=w_ple[0])
    y_p = fin(rows_p, h1_p, gw_p.T, p_prompt[0].reshape(t, -1))
    y_s = fin(rows_s, h1_s, gw_s.T, p_sample[0].reshape(n_dec, -1))

    return (y_p.reshape(n_seq, seq, d), y_s.reshape(n_dec, 1, d),
            k_p.reshape(1, n_seq, seq, 2 * B_HEADS, B_HEAD_DIM), v_p.reshape(1, n_seq, seq, B_HEADS, B_V_DIM),
            gv_p.reshape(1, n_seq, CHUNK, d_a),
            k_s.reshape(1, n_dec, 1, 2 * B_HEADS, B_HEAD_DIM), v_s.reshape(1, n_dec, 1, B_HEADS, B_V_DIM),
            gv_s.reshape(1, n_dec, 1, d_a))
```
